```python
import math
import jax
import jax.numpy as jnp
from jax import lax
import numpy as np

D_MODEL = 1024
BATCH = 16
SEQ = 2048
DEPTH = 2

GRID_W = 64
CTX_LEN = 256
N_EVEN = (DEPTH + 1) // 2
N_ODD = DEPTH // 2
EPS = 1e-6
ROPE_THETA = 10000.0
Q_BLOCK = 128
ADA_CHUNKS = 6

SSD_HEADS = 16
SSD_HEADDIM = 64
SSD_INNER = SSD_HEADS * SSD_HEADDIM
SSD_GROUPS = 4
SSD_STATE = 128
SSD_CONV = 3
SSD_CHUNK = 128
SSD_CONV_DIM = SSD_INNER + 2 * SSD_GROUPS * SSD_STATE
SSD_COLS = SSD_INNER + SSD_CONV_DIM + 2 * SSD_HEADS

ATT_HEADS = 16
ATT_KV_HEADS = 4
ATT_HEADDIM = 64
ATT_Q = ATT_HEADS * ATT_HEADDIM
ATT_KV = ATT_KV_HEADS * ATT_HEADDIM
ATT_COLS = ATT_Q + 2 * ATT_KV

AB_IN = SSD_COLS + ATT_COLS
AB_OUT = SSD_INNER + ATT_Q

MLA_HEADS = 16
MLA_NOPE = 64
MLA_ROPE = 32
MLA_V = 64
MLA_Q_RANK = 384
MLA_KV_RANK = 256
MLA_IN = MLA_Q_RANK + MLA_KV_RANK + MLA_ROPE

FFN_HIDDEN = -(-(8 * D_MODEL) // (3 * 256)) * 256

kernel_name = 'hybrid_ssd_gqa_mla_prefix_dit'


def rmsnorm(x, g):
    xf = x.astype(jnp.float32)
    xf = xf * lax.rsqrt(jnp.mean(xf * xf, axis=-1, keepdims=True) + EPS)
    return (xf * g.astype(jnp.float32)).astype(x.dtype)


def modulate(h, shift, scale):
    return h * (1 + scale) + shift


def axial_rope(rows, rot_dim, dtype):
    n_freq = rot_dim // 4
    row = jnp.repeat(jnp.arange(rows, dtype=jnp.float32), GRID_W)
    col = jnp.tile(jnp.arange(GRID_W, dtype=jnp.float32), rows)
    inv = ROPE_THETA ** (-jnp.arange(n_freq, dtype=jnp.float32) / n_freq)
    ang = jnp.concatenate([row[:, None] * inv, col[:, None] * inv], axis=-1)
    return jnp.cos(ang).astype(dtype), jnp.sin(ang).astype(dtype)


def apply_rope(x, cos, sin):
    shape = (cos.shape[0],) + (1,) * (x.ndim - 3) + (cos.shape[1],)
    cos = cos.reshape(shape)
    sin = sin.reshape(shape)
    x1, x2 = jnp.split(x, 2, axis=-1)
    return jnp.concatenate([x1 * cos - x2 * sin, x1 * sin + x2 * cos], axis=-1)


def dwconv_centred(x, w, b):
    pad = w.shape[0] // 2
    y = lax.conv_general_dilated(x, w[:, None, :].astype(x.dtype), window_strides=(1,),
                                 padding=[(pad, pad)], dimension_numbers=('NWC', 'WIO', 'NWC'),
                                 feature_group_count=x.shape[-1])
    return y + b


def swiglu(h, w_up, w_down):
    g, u = jnp.split(h @ w_up, 2, axis=-1)
    return (jax.nn.silu(g) * u) @ w_down


def gqa_attend(q, k, v):
    scale = q.shape[-1] ** -0.5
    s = jnp.einsum('bqhgd,bkhd->bhgqk', q, k).astype(jnp.float32) * scale
    p = jax.nn.softmax(s, axis=-1).astype(v.dtype)
    return jnp.einsum('bhgqk,bkhd->bqhgd', p, v)


def blocked_attend(q, k, v):
    b, t = q.shape[:2]
    nb = t // Q_BLOCK
    qb = jnp.moveaxis(q.reshape((b, nb, Q_BLOCK) + q.shape[2:]), 1, 0)
    ob = lax.map(lambda qi: gqa_attend(qi, k, v), qb)
    return jnp.moveaxis(ob, 0, 1).reshape((b, t) + ob.shape[3:])


def ssd_chunked(xs, dt, a, bm, cm, h0):
    bsz, length, nh, p = xs.shape
    g, n = bm.shape[2], bm.shape[3]
    hpg = nh // g
    q = SSD_CHUNK
    nc = length // q
    x = xs.reshape(bsz, nc, q, g, hpg, p)
    dtc = dt.reshape(bsz, nc, q, g, hpg).astype(jnp.float32)
    bc = bm.reshape(bsz, nc, q, g, n)
    cc = cm.reshape(bsz, nc, q, g, n)
    acs = jnp.cumsum(dtc * a.reshape(g, hpg), axis=2)
    seg = acs[:, :, :, None] - acs[:, :, None, :]
    mask = jnp.tril(jnp.ones((q, q), dtype=bool))[None, None, :, :, None, None]
    lmat = jnp.exp(jnp.where(mask, seg, -jnp.inf))
    cb = jnp.einsum('bcign,bcjgn->bcijg', cc, bc)
    xdt = x * dtc[..., None]
    y_diag = jnp.einsum('bcijgh,bcjghp->bcighp', cb[..., None] * lmat, xdt)
    decay_end = jnp.exp(acs[:, :, -1:] - acs)
    states = jnp.einsum('bcjgn,bcjghp->bcghpn', bc, xdt * decay_end[..., None])
    chunk_decay = jnp.exp(acs[:, :, -1])

    def step(h, inp):
        s, d = inp
        return h * d[..., None, None] + s, h

    h_final, h_starts = lax.scan(step, h0.reshape(bsz, g, hpg, p, n),
                                 (jnp.moveaxis(states, 1, 0), jnp.moveaxis(chunk_decay, 1, 0)))
    h_starts = jnp.moveaxis(h_starts, 0, 1)
    y_off = jnp.einsum('bcign,bcghpn->bcighp', cc, h_starts) * jnp.exp(acs)[..., None]
    y = (y_diag + y_off).reshape(bsz, length, nh, p)
    return y, h_final.reshape(bsz, nh, p, n)


def ssd_prep(p, conv_w, conv_b, dt_bias):
    b, t = p.shape[:2]
    gn = SSD_GROUPS * SSD_STATE
    z = p[..., :SSD_INNER]
    xbc = jax.nn.silu(dwconv_centred(p[..., SSD_INNER:SSD_INNER + SSD_CONV_DIM], conv_w, conv_b))
    xs = xbc[..., :SSD_INNER].reshape(b, t, SSD_HEADS, SSD_HEADDIM)
    bm = xbc[..., SSD_INNER:SSD_INNER + gn].reshape(b, t, SSD_GROUPS, SSD_STATE)
    cm = xbc[..., SSD_INNER + gn:].reshape(b, t, SSD_GROUPS, SSD_STATE)
    dt_raw = p[..., SSD_INNER + SSD_CONV_DIM:].reshape(b, t, 2, SSD_HEADS)
    dt = jax.nn.softplus((dt_raw + dt_bias).astype(jnp.float32))
    return z, xs, bm, cm, dt


def ssd_branch(p_lat, p_ctx, conv_w, conv_b, a_log, dt_bias, d_skip, norm_g):
    a = -jnp.exp(a_log.astype(jnp.float32))
    z_l, x_l, b_l, c_l, dt_l = ssd_prep(p_lat, conv_w, conv_b, dt_bias)
    z_c, x_c, b_c, c_c, dt_c = ssd_prep(p_ctx, conv_w, conv_b, dt_bias)
    h0 = jnp.zeros((p_lat.shape[0], SSD_HEADS, SSD_HEADDIM, SSD_STATE), jnp.float32)
    flip = lambda u: u[:, ::-1]
    yc_f, hc_f = ssd_chunked(x_c, dt_c[:, :, 0], a[0], b_c, c_c, h0)
    yl_f, _ = ssd_chunked(x_l, dt_l[:, :, 0], a[0], b_l, c_l, hc_f)
    yc_b, hc_b = ssd_chunked(flip(x_c), flip(dt_c[:, :, 1]), a[1], flip(b_c), flip(c_c), h0)
    yl_b, _ = ssd_chunked(flip(x_l), flip(dt_l[:, :, 1]), a[1], flip(b_l), flip(c_l), hc_b)

    def finish(yf, yb, xs, z):
        y = (yf + flip(yb)).astype(xs.dtype) + xs * d_skip[:, None].astype(xs.dtype)
        b, t = y.shape[:2]
        return rmsnorm(y.reshape(b, t, SSD_INNER) * jax.nn.silu(z), norm_g)

    return finish(yl_f, yl_b, x_l, z_l), finish(yc_f, yc_b, x_c, z_c)


def gqa_q(p, q_g, rope):
    b, t = p.shape[:2]
    q = rmsnorm(p[..., :ATT_Q].reshape(b, t, ATT_KV_HEADS, ATT_HEADS // ATT_KV_HEADS, ATT_HEADDIM), q_g)
    return q if rope is None else apply_rope(q, *rope)


def gqa_kv(p, k_g, rope):
    b, t = p.shape[:2]
    k = rmsnorm(p[..., ATT_Q:ATT_Q + ATT_KV].reshape(b, t, ATT_KV_HEADS, ATT_HEADDIM), k_g)
    v = p[..., ATT_Q + ATT_KV:].reshape(b, t, ATT_KV_HEADS, ATT_HEADDIM)
    return (k if rope is None else apply_rope(k, *rope)), v


def ab_mixer(h_lat, h_ctx, w_in, w_out, conv_w, conv_b, a_log, dt_bias, d_skip, ssd_g, q_g, k_g,
             rope, ctx_out):
    b, t = h_lat.shape[:2]
    p_lat = h_lat @ w_in
    p_ctx = h_ctx @ w_in
    y_lat, y_ctx = ssd_branch(p_lat[..., :SSD_COLS], p_ctx[..., :SSD_COLS], conv_w, conv_b,
                              a_log, dt_bias, d_skip, ssd_g)
    a_lat, a_ctx = p_lat[..., SSD_COLS:], p_ctx[..., SSD_COLS:]
    k_c, v_c = gqa_kv(a_ctx, k_g, None)
    k_l, v_l = gqa_kv(a_lat, k_g, rope)
    k_all = jnp.concatenate([k_c, k_l], axis=1)
    v_all = jnp.concatenate([v_c, v_l], axis=1)
    o_att = blocked_attend(gqa_q(a_lat, q_g, rope), k_all, v_all).reshape(b, t, ATT_Q)
    o_lat = jnp.concatenate([y_lat, o_att], axis=-1) @ w_out
    if not ctx_out:
        return o_lat, None
    o_att_c = gqa_attend(gqa_q(a_ctx, q_g, None), k_c, v_c).reshape(b, h_ctx.shape[1], ATT_Q)
    o_ctx = jnp.concatenate([y_ctx, o_att_c], axis=-1) @ w_out
    return o_lat, o_ctx


def mla_latents(h, w_in):
    p = h @ w_in
    return p[..., :MLA_Q_RANK], p[..., MLA_Q_RANK:MLA_Q_RANK + MLA_KV_RANK], p[..., MLA_Q_RANK + MLA_KV_RANK:]


def mla_q(cq, q_norm_g, w_uq, rope):
    b, t = cq.shape[:2]
    q = (rmsnorm(cq, q_norm_g) @ w_uq).reshape(b, t, MLA_HEADS, MLA_NOPE + MLA_ROPE)
    q_nope, q_pe = q[..., :MLA_NOPE], q[..., MLA_NOPE:]
    if rope is not None:
        q_pe = apply_rope(q_pe, *rope)
    return jnp.concatenate([q_nope, q_pe], axis=-1)[:, :, :, None, :]


def mla_kv(ckv, k_pe, kv_norm_g, w_ukv, rope):
    b, t = ckv.shape[:2]
    kv = (rmsnorm(ckv, kv_norm_g) @ w_ukv).reshape(b, t, MLA_HEADS, MLA_NOPE + MLA_V)
    k_nope, v = kv[..., :MLA_NOPE], kv[..., MLA_NOPE:]
    k_pe = k_pe[:, :, None, :]
    if rope is not None:
        k_pe = apply_rope(k_pe, *rope)
    k = jnp.concatenate([k_nope, jnp.broadcast_to(k_pe, (b, t, MLA_HEADS, MLA_ROPE))], axis=-1)
    return k, v


def mla_mixer(h_lat, h_ctx, w_in, q_norm_g, w_uq, kv_norm_g, w_ukv, w_o, rope, ctx_out):
    b, t = h_lat.shape[:2]
    cq_l, ckv_l, kpe_l = mla_latents(h_lat, w_in)
    cq_c, ckv_c, kpe_c = mla_latents(h_ctx, w_in)
    k_c, v_c = mla_kv(ckv_c, kpe_c, kv_norm_g, w_ukv, None)
    k_l, v_l = mla_kv(ckv_l, kpe_l, kv_norm_g, w_ukv, rope)
    k_all = jnp.concatenate([k_c, k_l], axis=1)
    v_all = jnp.concatenate([v_c, v_l], axis=1)
    o = blocked_attend(mla_q(cq_l, q_norm_g, w_uq, rope), k_all, v_all)
    o_lat = o.reshape(b, t, MLA_HEADS * MLA_V) @ w_o
    if not ctx_out:
        return o_lat, None
    o_c = gqa_attend(mla_q(cq_c, q_norm_g, w_uq, None), k_c, v_c)
    return o_lat, o_c.reshape(b, h_ctx.shape[1], MLA_HEADS * MLA_V) @ w_o


def setup_inputs(seed: int = 0) -> dict:
    key = jax.random.key(seed)
    ks = iter(jax.random.split(key, 40))
    f32 = jnp.float32

    def nrm(shape, scale):
        return jax.random.normal(next(ks), shape, f32) * scale

    def gain(shape):
        return 1.0 + nrm(shape, 0.02)

    dt0 = jnp.exp(jax.random.uniform(next(ks), (N_EVEN, 2, SSD_HEADS), f32, math.log(1e-3), math.log(1e-1)))
    return {
        'x': nrm((BATCH, SEQ, D_MODEL), 1.0),
        'c': nrm((BATCH, D_MODEL), 1.0),
        'ctx': nrm((BATCH, CTX_LEN, D_MODEL), 1.0),
        'c_ctx': nrm((D_MODEL,), 1.0),
        'ada_w': nrm((DEPTH, D_MODEL, ADA_CHUNKS * D_MODEL), 0.5 * D_MODEL ** -0.5),
        'ada_b': nrm((DEPTH, ADA_CHUNKS * D_MODEL), 0.02),
        'norm1_g': gain((DEPTH, D_MODEL)),
        'norm2_g': gain((DEPTH, D_MODEL)),
        'ffn_w_up': nrm((DEPTH, D_MODEL, 2 * FFN_HIDDEN), D_MODEL ** -0.5),
        'ffn_w_down': nrm((DEPTH, FFN_HIDDEN, D_MODEL), FFN_HIDDEN ** -0.5),
        'ab_w_in': nrm((N_EVEN, D_MODEL, AB_IN), D_MODEL ** -0.5),
        'ab_w_out': nrm((N_EVEN, AB_OUT, D_MODEL), AB_OUT ** -0.5),
        'ssd_conv_w': nrm((N_EVEN, SSD_CONV, SSD_CONV_DIM), SSD_CONV ** -0.5),
        'ssd_conv_b': nrm((N_EVEN, SSD_CONV_DIM), 0.02),
        'ssd_a_log': jnp.log(jax.random.uniform(next(ks), (N_EVEN, 2, SSD_HEADS), f32, 1.0, 16.0)),
        'ssd_dt_bias': dt0 + jnp.log(-jnp.expm1(-dt0)),
        'ssd_d': gain((N_EVEN, SSD_HEADS)),
        'ssd_norm_g': gain((N_EVEN, SSD_INNER)),
        'att_q_g': gain((N_EVEN, ATT_HEADDIM)),
        'att_k_g': gain((N_EVEN, ATT_HEADDIM)),
        'mla_w_in': nrm((N_ODD, D_MODEL, MLA_IN), D_MODEL ** -0.5),
        'mla_q_norm_g': gain((N_ODD, MLA_Q_RANK)),
        'mla_w_uq': nrm((N_ODD, MLA_Q_RANK, MLA_HEADS * (MLA_NOPE + MLA_ROPE)), MLA_Q_RANK ** -0.5),
        'mla_kv_norm_g': gain((N_ODD, MLA_KV_RANK)),
        'mla_w_ukv': nrm((N_ODD, MLA_KV_RANK, MLA_HEADS * (MLA_NOPE + MLA_V)), MLA_KV_RANK ** -0.5),
        'mla_w_o': nrm((N_ODD, MLA_HEADS * MLA_V, D_MODEL), (MLA_HEADS * MLA_V) ** -0.5),
        'final_norm_g': gain((D_MODEL,)),
    }


def reference(x, c, ctx, c_ctx, ada_w, ada_b, norm1_g, norm2_g, ffn_w_up, ffn_w_down,
              ab_w_in, ab_w_out, ssd_conv_w, ssd_conv_b, ssd_a_log, ssd_dt_bias, ssd_d, ssd_norm_g,
              att_q_g, att_k_g, mla_w_in, mla_q_norm_g, mla_w_uq, mla_kv_norm_g, mla_w_ukv, mla_w_o,
              final_norm_g):
    rows = x.shape[1] // GRID_W
    rope_att = axial_rope(rows, ATT_HEADDIM, x.dtype)
    rope_mla = axial_rope(rows, MLA_ROPE, x.dtype)
    sc = jax.nn.silu(c)
    scc = jax.nn.silu(c_ctx)
    for i in range(DEPTH):
        last = i == DEPTH - 1
        j = i // 2
        mods = jnp.split((sc @ ada_w[i] + ada_b[i])[:, None, :], ADA_CHUNKS, axis=-1)
        mods_c = jnp.split(scc @ ada_w[i] + ada_b[i], ADA_CHUNKS, axis=-1)
        h_lat = modulate(rmsnorm(x, norm1_g[i]), mods[0], mods[1])
        h_ctx = modulate(rmsnorm(ctx, norm1_g[i]), mods_c[0], mods_c[1])
        if i % 2 == 0:
            o_lat, o_ctx = ab_mixer(h_lat, h_ctx, ab_w_in[j], ab_w_out[j], ssd_conv_w[j], ssd_conv_b[j],
                                    ssd_a_log[j], ssd_dt_bias[j], ssd_d[j], ssd_norm_g[j],
                                    att_q_g[j], att_k_g[j], rope_att, not last)
        else:
            o_lat, o_ctx = mla_mixer(h_lat, h_ctx, mla_w_in[j], mla_q_norm_g[j], mla_w_uq[j],
                                     mla_kv_norm_g[j], mla_w_ukv[j], mla_w_o[j], rope_mla, not last)
        x = x + mods[2] * o_lat
        h = modulate(rmsnorm(x, norm2_g[i]), mods[3], mods[4])
        x = x + mods[5] * swiglu(h, ffn_w_up[i], ffn_w_down[i])
        if not last:
            ctx = ctx + mods_c[2] * o_ctx
            hc = modulate(rmsnorm(ctx, norm2_g[i]), mods_c[3], mods_c[4])
            ctx = ctx + mods_c[5] * swiglu(hc, ffn_w_up[i], ffn_w_down[i])
    return rmsnorm(x, final_norm_g)
```

```python
import functools
import math

import jax
import jax.numpy as jnp
from jax import lax
from jax.experimental import pallas as pl
from jax.experimental.pallas import tpu as pltpu

F32 = jnp.float32
BF16 = jnp.bfloat16

EPS = 1e-6
ROPE_THETA = 10000.0
GRID_W = 64
ADA_CHUNKS = 6

SSD_HEADS = 16
SSD_HEADDIM = 64
SSD_GROUPS = 4
SSD_STATE = 128
SSD_CHUNK = 128
SSD_INNER = SSD_HEADS * SSD_HEADDIM
SSD_GN = SSD_GROUPS * SSD_STATE
SSD_CONV_DIM = SSD_INNER + 2 * SSD_GN

ATT_HEADS = 16
ATT_KV_HEADS = 4
ATT_HEADDIM = 64
ATT_Q = ATT_HEADS * ATT_HEADDIM
ATT_KV = ATT_KV_HEADS * ATT_HEADDIM

MLA_HEADS = 16
MLA_NOPE = 64
MLA_ROPE = 32
MLA_V = 64
MLA_QK = MLA_NOPE + MLA_ROPE

LANES = 128
HEAD_BLOCK = 4
VMEM_LIMIT = 56 * 1024 * 1024


def _pick(n, candidates):
    for c in candidates:
        if n % c == 0:
            return c
    raise ValueError(f"no tile for {n} in {candidates}")


def _params(sem):
    return pltpu.CompilerParams(dimension_semantics=sem, vmem_limit_bytes=VMEM_LIMIT)


def _sigmoid(x):
    return 1.0 / (1.0 + jnp.exp(-x))


def _silu(x):
    return x * _sigmoid(x)


def _rms(x, g):
    ms = jnp.mean(x * x, axis=-1, keepdims=True)
    return x * lax.rsqrt(ms + EPS) * g


def _mod_row(ml_ref, mc_ref, idx, is_lat):
    return jnp.where(is_lat, ml_ref[0, idx:idx + 1, :], mc_ref[idx:idx + 1, :])


def _is_lat(t, tm, n_lat):
    rows = t * tm + lax.broadcasted_iota(jnp.int32, (tm, 1), 0)
    return rows < n_lat


def _ada_kernel(c_ref, w_ref, b_ref, o_ref):
    s = _silu(c_ref[...]).astype(BF16)
    o_ref[0] = jnp.dot(s, w_ref[0].astype(BF16), preferred_element_type=F32) + b_ref[0]


def _ada_mods(cvec, ada_w, ada_b):
    depth, d, n = ada_w.shape
    bp = cvec.shape[0]
    tn = _pick(n, (1536, 1024, 512, 256, 128))
    return pl.pallas_call(
        _ada_kernel,
        grid=(depth, n // tn),
        in_specs=[
            pl.BlockSpec((bp, d), lambda i, j: (0, 0)),
            pl.BlockSpec((1, d, tn), lambda i, j: (i, 0, j)),
            pl.BlockSpec((1, 1, tn), lambda i, j: (i, 0, j)),
        ],
        out_specs=pl.BlockSpec((1, bp, tn), lambda i, j: (i, 0, j)),
        out_shape=jax.ShapeDtypeStruct((depth, bp, n), F32),
        compiler_params=_params(("arbitrary", "arbitrary")),
        name="ada_mods",
    )(cvec, ada_w, ada_b.reshape(depth, 1, n))


def _in_proj_kernel(x_ref, ml_ref, mc_ref, g_ref, w_ref, ws_ref, o_ref, os_ref, h_ref, *, tm, n_lat):
    t = pl.program_id(1)
    j = pl.program_id(2)

    @pl.when(j == 0)
    def _():
        is_lat = _is_lat(t, tm, n_lat)
        h = _rms(x_ref[0], g_ref[...])
        h = h * (1.0 + _mod_row(ml_ref, mc_ref, 1, is_lat)) + _mod_row(ml_ref, mc_ref, 0, is_lat)
        hb = h.astype(BF16)
        h_ref[...] = hb
        os_ref[0] = jnp.dot(hb, ws_ref[...], preferred_element_type=F32)

    o_ref[0] = jnp.dot(h_ref[...], w_ref[...], preferred_element_type=F32).astype(o_ref.dtype)


def _in_proj(xa, mods_l, mods_c, g, w_main, w_side, n_lat):
    b, n, d = xa.shape
    nm = w_main.shape[1]
    ns = w_side.shape[1]
    tm = _pick(n, (768, 384, 256, 128))
    tn = _pick(nm, (1536, 1024, 768, 640, 512, 256, 128))
    kern = functools.partial(_in_proj_kernel, tm=tm, n_lat=n_lat)
    return pl.pallas_call(
        kern,
        grid=(b, n // tm, nm // tn),
        in_specs=[
            pl.BlockSpec((1, tm, d), lambda i, t, j: (i, t, 0)),
            pl.BlockSpec((1, ADA_CHUNKS, d), lambda i, t, j: (i, 0, 0)),
            pl.BlockSpec((ADA_CHUNKS, d), lambda i, t, j: (0, 0)),
            pl.BlockSpec((1, d), lambda i, t, j: (0, 0)),
            pl.BlockSpec((d, tn), lambda i, t, j: (0, j)),
            pl.BlockSpec((d, ns), lambda i, t, j: (0, 0)),
        ],
        out_specs=[
            pl.BlockSpec((1, tm, tn), lambda i, t, j: (i, t, j)),
            pl.BlockSpec((1, tm, ns), lambda i, t, j: (i, t, 0)),
        ],
        out_shape=[
            jax.ShapeDtypeStruct((b, n, nm), BF16),
            jax.ShapeDtypeStruct((b, n, ns), F32),
        ],
        scratch_shapes=[pltpu.VMEM((tm, d), BF16)],
        compiler_params=_params(("arbitrary", "arbitrary", "arbitrary")),
        name="in_proj",
    )(xa, mods_l, mods_c, g, w_main, w_side)


def _cumsum_rows(v, reverse):
    n = v.shape[0]
    row = lax.broadcasted_iota(jnp.int32, v.shape, 0)
    k = 1
    while k < n:
        if reverse:
            v = v + jnp.where(row < n - k, pltpu.roll(v, n - k, axis=0), 0.0)
        else:
            v = v + jnp.where(row >= k, pltpu.roll(v, k, axis=0), 0.0)
        k *= 2
    return v


def _ssd_chunk_index(s, nl, nc, reverse):
    if reverse:
        return nc - 1 - s
    return lax.rem(s + nl, nc)


def _ssd_kernel(*refs, reverse, finish, nl, nc):
    if finish:
        (xm_ref, xp_ref, xn_ref, dt_ref, cw_ref, cb_ref, dtb_ref, alog_ref,
         z_ref, yf_ref, dsk_ref, ng_ref, o_ref, st_ref) = refs
    else:
        (xm_ref, xp_ref, xn_ref, dt_ref, cw_ref, cb_ref, dtb_ref, alog_ref, o_ref, st_ref) = refs
    q = SSD_CHUNK
    p = SSD_HEADDIM
    hpg = SSD_HEADS // SSD_GROUPS
    gw = hpg * p
    s = pl.program_id(1)
    chunk = _ssd_chunk_index(s, nl, nc, reverse)

    @pl.when(s == 0)
    def _():
        st_ref[...] = jnp.zeros_like(st_ref)

    xm = xm_ref[0].astype(F32)
    halo = xp_ref.shape[1]
    seq_first = jnp.logical_or(chunk == 0, chunk == nl)
    seq_last = jnp.logical_or(chunk == nl - 1, chunk == nc - 1)
    prev_row = xp_ref[0, halo - 1:halo, :].astype(F32) * jnp.where(seq_first, 0.0, 1.0)
    next_row = xn_ref[0, 0:1, :].astype(F32) * jnp.where(seq_last, 0.0, 1.0)
    row = lax.broadcasted_iota(jnp.int32, (q, 1), 0)
    x_prev = jnp.where(row == 0, prev_row, pltpu.roll(xm, 1, axis=0))
    x_next = jnp.where(row == q - 1, next_row, pltpu.roll(xm, q - 1, axis=0))
    xc = x_prev * cw_ref[0:1, :] + xm * cw_ref[1:2, :] + x_next * cw_ref[2:3, :] + cb_ref[...]
    act = _silu(xc)
    xs = act[:, :SSD_INNER]
    bm = act[:, SSD_INNER:SSD_INNER + SSD_GN]
    cm = act[:, SSD_INNER + SSD_GN:]

    dtr = dt_ref[0] + dtb_ref[...]
    dt = jnp.maximum(dtr, 0.0) + jnp.log1p(jnp.exp(-jnp.abs(dtr)))
    a = -jnp.exp(alog_ref[...])
    acs = _cumsum_rows(dt * a, reverse)
    acs_t = acs.T
    edge = acs[0:1, :] if reverse else acs[q - 1:q, :]
    eacs = jnp.exp(acs)
    dtd = dt * jnp.exp(edge - acs)
    cdec = jnp.exp(edge)

    ii = lax.broadcasted_iota(jnp.int32, (q, q), 0)
    jj = lax.broadcasted_iota(jnp.int32, (q, q), 1)
    mask = (ii <= jj) if reverse else (ii >= jj)

    def expand(v, g):
        return jnp.concatenate(
            [jnp.broadcast_to(v[:, g * hpg + k:g * hpg + k + 1], (v.shape[0], p)) for k in range(hpg)], axis=1)

    ys = []
    for g in range(SSD_GROUPS):
        cg = cm[:, g * SSD_STATE:(g + 1) * SSD_STATE].astype(BF16)
        bg32 = bm[:, g * SSD_STATE:(g + 1) * SSD_STATE]
        bg = bg32.astype(BF16)
        cb = lax.dot_general(cg, bg, (((1,), (1,)), ((), ())), preferred_element_type=F32)
        xg = xs[:, g * gw:(g + 1) * gw]
        xdt = (xg * expand(dt, g)).astype(BF16)
        yd = []
        for k in range(hpg):
            h = g * hpg + k
            seg = jnp.broadcast_to(acs[:, h:h + 1], (q, q)) - jnp.broadcast_to(acs_t[h:h + 1, :], (q, q))
            lmat = jnp.exp(jnp.where(mask, seg, -jnp.inf))
            m = (cb * lmat).astype(BF16)
            yd.append(jnp.dot(m, xdt[:, k * p:(k + 1) * p], preferred_element_type=F32))
        st = st_ref[g]
        y_off = jnp.dot(cg, st.astype(BF16), preferred_element_type=F32) * expand(eacs, g)
        xdd = (xg * expand(dtd, g)).astype(BF16)
        new = jnp.dot(bg32.T.astype(BF16), xdd, preferred_element_type=F32)
        st_ref[g] = st * expand(cdec, g) + new
        ys.append(jnp.concatenate(yd, axis=1) + y_off)
    y = jnp.concatenate(ys, axis=1)

    if finish:
        y = yf_ref[0] + y + xs * dsk_ref[...]
        y = y * _silu(z_ref[0].astype(F32))
        o_ref[0] = _rms(y, ng_ref[...]).astype(o_ref.dtype)
    else:
        o_ref[0] = y


def _ssd_direction(p0, s0, conv_w, conv_b, dt_bias, a_log, n_lat, reverse, extra=None):
    b, n, _ = p0.shape
    q = SSD_CHUNK
    nc = n // q
    nl = n_lat // q
    halo = 16
    hpc = q // halo
    nh = n // halo
    d = 1 if reverse else 0
    cidx = functools.partial(_ssd_chunk_index, nl=nl, nc=nc, reverse=reverse)
    finish = extra is not None
    in_specs = [
        pl.BlockSpec((1, q, SSD_CONV_DIM), lambda i, s: (i, cidx(s), 0)),
        pl.BlockSpec((1, halo, SSD_CONV_DIM), lambda i, s: (i, jnp.maximum(cidx(s) * hpc - 1, 0), 0)),
        pl.BlockSpec((1, halo, SSD_CONV_DIM), lambda i, s: (i, jnp.minimum(cidx(s) * hpc + hpc, nh - 1), 0)),
        pl.BlockSpec((1, q, LANES), lambda i, s: (i, cidx(s), d)),
        pl.BlockSpec((3, SSD_CONV_DIM), lambda i, s: (0, 0)),
        pl.BlockSpec((1, SSD_CONV_DIM), lambda i, s: (0, 0)),
        pl.BlockSpec((1, LANES), lambda i, s: (0, 0)),
        pl.BlockSpec((1, LANES), lambda i, s: (0, 0)),
    ]
    args = [p0, p0, p0, s0, conv_w, conv_b, dt_bias, a_log]
    if finish:
        yf, d_skip, norm_g = extra
        in_specs += [
            pl.BlockSpec((1, q, SSD_INNER), lambda i, s: (i, cidx(s), SSD_CONV_DIM // SSD_INNER)),
            pl.BlockSpec((1, q, SSD_INNER), lambda i, s: (i, cidx(s), 0)),
            pl.BlockSpec((1, SSD_INNER), lambda i, s: (0, 0)),
            pl.BlockSpec((1, SSD_INNER), lambda i, s: (0, 0)),
        ]
        args += [p0, yf, d_skip, norm_g]
    kern = functools.partial(_ssd_kernel, reverse=reverse, finish=finish, nl=nl, nc=nc)
    return pl.pallas_call(
        kern,
        grid=(b, nc),
        in_specs=in_specs,
        out_specs=pl.BlockSpec((1, q, SSD_INNER), lambda i, s: (i, cidx(s), 0)),
        out_shape=jax.ShapeDtypeStruct((b, n, SSD_INNER), BF16 if finish else F32),
        scratch_shapes=[pltpu.VMEM((SSD_GROUPS, SSD_STATE, (SSD_HEADS // SSD_GROUPS) * SSD_HEADDIM), F32)],
        compiler_params=_params(("arbitrary", "arbitrary")),
        name="ssd_bwd_finish" if finish else "ssd_fwd",
    )(*args)


def _rope_tables(rows, rot_dim, n_ctx):
    n_freq = rot_dim // 4
    row = jnp.repeat(jnp.arange(rows, dtype=F32), GRID_W)
    col = jnp.tile(jnp.arange(GRID_W, dtype=F32), rows)
    inv = ROPE_THETA ** (-jnp.arange(n_freq, dtype=F32) / n_freq)
    ang = jnp.concatenate([row[:, None] * inv, col[:, None] * inv], axis=-1)
    cos, sin = jnp.cos(ang), jnp.sin(ang)
    reps = LANES // rot_dim
    cos_t = jnp.tile(jnp.concatenate([cos, cos], axis=-1), (1, reps))
    sin_t = jnp.tile(jnp.concatenate([-sin, sin], axis=-1), (1, reps))
    cos_t = jnp.concatenate([cos_t, jnp.ones((n_ctx, LANES), F32)], axis=0)
    sin_t = jnp.concatenate([sin_t, jnp.zeros((n_ctx, LANES), F32)], axis=0)
    return cos_t, sin_t


def _rope_block(xb, cos, sin, rot_dim):
    half = rot_dim // 2
    lane = lax.broadcasted_iota(jnp.int32, xb.shape, 1)
    first = jnp.bitwise_and(lane, rot_dim - 1) < half
    partner = jnp.where(first, pltpu.roll(xb, LANES - half, axis=1), pltpu.roll(xb, half, axis=1))
    return xb * cos + partner * sin


def _segment_mean_matrix(seg):
    sh = seg.bit_length() - 1
    i = jnp.right_shift(lax.broadcasted_iota(jnp.int32, (LANES, LANES), 0), sh)
    j = jnp.right_shift(lax.broadcasted_iota(jnp.int32, (LANES, LANES), 1), sh)
    return jnp.where(i == j, 1.0 / seg, 0.0).astype(BF16)


def _segment_mean_sq(xb, smat):
    x2 = xb * xb
    hi = x2.astype(BF16)
    lo = (x2 - hi.astype(F32)).astype(BF16)
    return (jnp.dot(hi, smat, preferred_element_type=F32) + jnp.dot(lo, smat, preferred_element_type=F32))


def _qk_prep_kernel(q_ref, k_ref, v_ref, cos_ref, sin_ref, qg_ref, kg_ref, qo_ref, ko_ref, vo_ref, *, scale):
    smat = _segment_mean_matrix(ATT_HEADDIM)
    cos = cos_ref[...]
    sin = sin_ref[...]
    hpb = LANES // ATT_HEADDIM

    def prep(src_ref, g_ref, dst_ref, n_heads, mul):
        for c in range(n_heads // hpb):
            xb = src_ref[0, :, c * LANES:(c + 1) * LANES].astype(F32)
            xn = xb * lax.rsqrt(_segment_mean_sq(xb, smat) + EPS) * g_ref[...]
            r = _rope_block(xn, cos, sin, ATT_HEADDIM)
            if mul != 1.0:
                r = r * mul
            r = r.astype(dst_ref.dtype)
            for k in range(hpb):
                dst_ref[0, c * hpb + k] = r[:, k * ATT_HEADDIM:(k + 1) * ATT_HEADDIM]

    prep(q_ref, qg_ref, qo_ref, ATT_HEADS, scale)
    prep(k_ref, kg_ref, ko_ref, ATT_KV_HEADS, 1.0)
    for h in range(ATT_KV_HEADS):
        vo_ref[0, h] = v_ref[0, :, h * ATT_HEADDIM:(h + 1) * ATT_HEADDIM]


def _qk_prep(p0, cos_t, sin_t, q_g, k_g, q_col, k_col, v_col):
    b, n, _ = p0.shape
    tm = _pick(n, (768, 384, 256, 128))
    kern = functools.partial(_qk_prep_kernel, scale=ATT_HEADDIM ** -0.5)
    return pl.pallas_call(
        kern,
        grid=(b, n // tm),
        in_specs=[
            pl.BlockSpec((1, tm, ATT_Q), lambda i, t: (i, t, q_col // ATT_Q)),
            pl.BlockSpec((1, tm, ATT_KV), lambda i, t: (i, t, k_col // ATT_KV)),
            pl.BlockSpec((1, tm, ATT_KV), lambda i, t: (i, t, v_col // ATT_KV)),
            pl.BlockSpec((tm, LANES), lambda i, t: (t, 0)),
            pl.BlockSpec((tm, LANES), lambda i, t: (t, 0)),
            pl.BlockSpec((1, LANES), lambda i, t: (0, 0)),
            pl.BlockSpec((1, LANES), lambda i, t: (0, 0)),
        ],
        out_specs=[
            pl.BlockSpec((1, ATT_HEADS, tm, ATT_HEADDIM), lambda i, t: (i, 0, t, 0)),
            pl.BlockSpec((1, ATT_KV_HEADS, tm, ATT_HEADDIM), lambda i, t: (i, 0, t, 0)),
            pl.BlockSpec((1, ATT_KV_HEADS, tm, ATT_HEADDIM), lambda i, t: (i, 0, t, 0)),
        ],
        out_shape=[
            jax.ShapeDtypeStruct((b, ATT_HEADS, n, ATT_HEADDIM), BF16),
            jax.ShapeDtypeStruct((b, ATT_KV_HEADS, n, ATT_HEADDIM), BF16),
            jax.ShapeDtypeStruct((b, ATT_KV_HEADS, n, ATT_HEADDIM), BF16),
        ],
        compiler_params=_params(("arbitrary", "arbitrary")),
        name="gqa_qk_prep",
    )(p0, p0, p0, cos_t, sin_t, q_g, k_g)


def _attn_kernel(q_ref, k_ref, v_ref, o_ref, *, group, n_lat, n_all, lat_tiles, ctx_self):
    t = pl.program_id(2)

    def run(k_lo, k_len):
        outs = []
        for u in range(HEAD_BLOCK):
            q = q_ref[0, u]
            kk = k_ref[0, u // group, k_lo:k_lo + k_len, :]
            vv = v_ref[0, u // group, k_lo:k_lo + k_len, :]
            s = lax.dot_general(q, kk, (((1,), (1,)), ((), ())), preferred_element_type=F32)
            m = jnp.max(s, axis=-1, keepdims=True)
            e = jnp.exp(s - m)
            l = jnp.sum(e, axis=-1, keepdims=True)
            o = jnp.dot(e.astype(BF16), vv, preferred_element_type=F32)
            outs.append(o * (1.0 / l))
        o_ref[0] = jnp.concatenate(outs, axis=-1).astype(o_ref.dtype)

    if ctx_self:
        @pl.when(t < lat_tiles)
        def _():
            run(0, n_all)

        @pl.when(t >= lat_tiles)
        def _():
            run(n_lat, n_all - n_lat)
    else:
        run(0, n_all)


def _attention(q, k, v, n_lat, ctx_self):
    b, h, nq, dqk = q.shape
    hkv, n_all, dv = v.shape[1], v.shape[2], v.shape[3]
    group = h // hkv
    kvb = max(HEAD_BLOCK // group, 1)
    tq = _pick(math.gcd(n_lat, n_all - n_lat) if ctx_self else n_lat, (256, 128))
    assert nq == (n_all if ctx_self else n_lat)
    kern = functools.partial(_attn_kernel, group=group, n_lat=n_lat, n_all=n_all,
                             lat_tiles=n_lat // tq, ctx_self=ctx_self)
    return pl.pallas_call(
        kern,
        grid=(b, h // HEAD_BLOCK, nq // tq),
        in_specs=[
            pl.BlockSpec((1, HEAD_BLOCK, tq, dqk), lambda i, hb, t: (i, hb, t, 0)),
            pl.BlockSpec((1, kvb, n_all, dqk), lambda i, hb, t: (i, hb, 0, 0)),
            pl.BlockSpec((1, kvb, n_all, dv), lambda i, hb, t: (i, hb, 0, 0)),
        ],
        out_specs=pl.BlockSpec((1, tq, HEAD_BLOCK * dv), lambda i, hb, t: (i, t, hb)),
        out_shape=jax.ShapeDtypeStruct((b, nq, h * dv), BF16),
        compiler_params=_params(("arbitrary", "arbitrary", "arbitrary")),
        name="attention",
    )(q, k, v)


def _out_proj_kernel(*refs, n_in, tm, n_lat):
    x_ref, ml_ref, mc_ref = refs[:3]
    a_refs = refs[3:3 + n_in]
    w_refs = refs[3 + n_in:3 + 2 * n_in]
    o_ref = refs[3 + 2 * n_in]
    t = pl.program_id(1)
    acc = jnp.dot(a_refs[0][0], w_refs[0][...], preferred_element_type=F32)
    for a_ref, w_ref in zip(a_refs[1:], w_refs[1:]):
        acc = acc + jnp.dot(a_ref[0], w_ref[...], preferred_element_type=F32)
    gate = _mod_row(ml_ref, mc_ref, 2, _is_lat(t, tm, n_lat))
    o_ref[0] = x_ref[0] + gate * acc


def _out_proj(xa, mods_l, mods_c, acts, weights, n_rows, n_lat):
    b, _, d = xa.shape
    tm = _pick(n_rows, (1024, 768, 512, 384, 256, 128))
    n_in = len(acts)
    kern = functools.partial(_out_proj_kernel, n_in=n_in, tm=tm, n_lat=n_lat)
    in_specs = [
        pl.BlockSpec((1, tm, d), lambda i, t: (i, t, 0)),
        pl.BlockSpec((1, ADA_CHUNKS, d), lambda i, t: (i, 0, 0)),
        pl.BlockSpec((ADA_CHUNKS, d), lambda i, t: (0, 0)),
    ]
    in_specs += [pl.BlockSpec((1, tm, a.shape[2]), lambda i, t: (i, t, 0)) for a in acts]
    in_specs += [pl.BlockSpec(w.shape, lambda i, t: (0, 0)) for w in weights]
    return pl.pallas_call(
        kern,
        grid=(b, n_rows // tm),
        in_specs=in_specs,
        out_specs=pl.BlockSpec((1, tm, d), lambda i, t: (i, t, 0)),
        out_shape=jax.ShapeDtypeStruct((b, n_rows, d), F32),
        compiler_params=_params(("arbitrary", "arbitrary")),
        name="out_proj",
    )(xa, mods_l, mods_c, *acts, *weights)


def _hidden_chunks(hidden):
    chunks, lo = [], 0
    while lo < hidden:
        w = min(1024, hidden - lo)
        chunks.append((lo, w))
        lo += w
    return chunks


def _ffn_kernel(x_ref, ml_ref, mc_ref, g_ref, wg_ref, wu_ref, wd_ref, fg_ref, o_ref, *, tm, n_lat, final):
    t = pl.program_id(1)
    is_lat = _is_lat(t, tm, n_lat)
    x = x_ref[0]
    h = _rms(x, g_ref[...])
    h = (h * (1.0 + _mod_row(ml_ref, mc_ref, 4, is_lat)) + _mod_row(ml_ref, mc_ref, 3, is_lat)).astype(BF16)
    acc = None
    for lo, w in _hidden_chunks(wd_ref.shape[0]):
        gate = jnp.dot(h, wg_ref[:, lo:lo + w], preferred_element_type=F32)
        up = jnp.dot(h, wu_ref[:, lo:lo + w], preferred_element_type=F32)
        a = (_silu(gate) * up).astype(BF16)
        part = jnp.dot(a, wd_ref[lo:lo + w, :], preferred_element_type=F32)
        acc = part if acc is None else acc + part
    y = x + _mod_row(ml_ref, mc_ref, 5, is_lat) * acc
    if final:
        y = _rms(y, fg_ref[...])
    o_ref[0] = y


def _ffn(xa, mods_l, mods_c, g, w_gate, w_up, w_down, final_g, n_lat, final):
    b, n, d = xa.shape
    hid = w_down.shape[0]
    tm = _pick(n, (768, 512, 384, 256, 128))
    kern = functools.partial(_ffn_kernel, tm=tm, n_lat=n_lat, final=final)
    resident = dict(pipeline_mode=pl.Buffered(1))
    return pl.pallas_call(
        kern,
        grid=(b, n // tm),
        in_specs=[
            pl.BlockSpec((1, tm, d), lambda i, t: (i, t, 0)),
            pl.BlockSpec((1, ADA_CHUNKS, d), lambda i, t: (i, 0, 0)),
            pl.BlockSpec((ADA_CHUNKS, d), lambda i, t: (0, 0)),
            pl.BlockSpec((1, d), lambda i, t: (0, 0)),
            pl.BlockSpec((d, hid), lambda i, t: (0, 0), **resident),
            pl.BlockSpec((d, hid), lambda i, t: (0, 0), **resident),
            pl.BlockSpec((hid, d), lambda i, t: (0, 0), **resident),
            pl.BlockSpec((1, d), lambda i, t: (0, 0)),
        ],
        out_specs=pl.BlockSpec((1, tm, d), lambda i, t: (i, t, 0)),
        out_shape=jax.ShapeDtypeStruct((b, n, d), F32),
        compiler_params=_params(("arbitrary", "arbitrary")),
        name="ffn_final" if final else "ffn",
    )(xa, mods_l, mods_c, g, w_gate, w_up, w_down, final_g)


def _mla_q_kernel(p_ref, g_ref, w_ref, cos_ref, sin_ref, o_ref, *, q_rank, scale):
    cq = p_ref[0, :, :q_rank].astype(F32)
    qn = _rms(cq, g_ref[...]).astype(BF16)
    qq = jnp.dot(qn, w_ref[...], preferred_element_type=F32)
    n_nope = MLA_HEADS * MLA_NOPE
    hpb = LANES // MLA_ROPE
    pes = []
    for c in range(MLA_HEADS // hpb):
        pes.append(_rope_block(qq[:, n_nope + c * LANES:n_nope + (c + 1) * LANES],
                               cos_ref[...], sin_ref[...], MLA_ROPE))
    for h in range(MLA_HEADS):
        pe = pes[h // hpb][:, (h % hpb) * MLA_ROPE:(h % hpb + 1) * MLA_ROPE]
        qh = jnp.concatenate([qq[:, h * MLA_NOPE:(h + 1) * MLA_NOPE], pe], axis=1) * scale
        o_ref[0, h] = qh.astype(o_ref.dtype)


def _mla_q(p1, g, w_uq, cos_t, sin_t, n_lat, q_rank):
    b = p1.shape[0]
    cols = p1.shape[2]
    tm = _pick(n_lat, (512, 256, 128))
    kern = functools.partial(_mla_q_kernel, q_rank=q_rank, scale=MLA_QK ** -0.5)
    return pl.pallas_call(
        kern,
        grid=(b, n_lat // tm),
        in_specs=[
            pl.BlockSpec((1, tm, cols), lambda i, t: (i, t, 0)),
            pl.BlockSpec((1, q_rank), lambda i, t: (0, 0)),
            pl.BlockSpec(w_uq.shape, lambda i, t: (0, 0)),
            pl.BlockSpec((tm, LANES), lambda i, t: (t, 0)),
            pl.BlockSpec((tm, LANES), lambda i, t: (t, 0)),
        ],
        out_specs=pl.BlockSpec((1, MLA_HEADS, tm, MLA_QK), lambda i, t: (i, 0, t, 0)),
        out_shape=jax.ShapeDtypeStruct((b, MLA_HEADS, n_lat, MLA_QK), BF16),
        compiler_params=_params(("arbitrary", "arbitrary")),
        name="mla_q",
    )(p1, g, w_uq, cos_t, sin_t)


def _mla_kv_kernel(p_ref, pe_ref, g_ref, w_ref, cos_ref, sin_ref, ko_ref, vo_ref, *, q_rank):
    ckv = p_ref[0, :, q_rank:].astype(F32)
    kvn = _rms(ckv, g_ref[...]).astype(BF16)
    kv = jnp.dot(kvn, w_ref[...], preferred_element_type=F32)
    pe = _rope_block(pe_ref[0], cos_ref[...], sin_ref[...], MLA_ROPE)[:, :MLA_ROPE]
    n_nope = MLA_HEADS * MLA_NOPE
    for h in range(MLA_HEADS):
        kh = jnp.concatenate([kv[:, h * MLA_NOPE:(h + 1) * MLA_NOPE], pe], axis=1)
        ko_ref[0, h] = kh.astype(ko_ref.dtype)
        vo_ref[0, h] = kv[:, n_nope + h * MLA_V:n_nope + (h + 1) * MLA_V].astype(vo_ref.dtype)


def _mla_kv(p1, s1, g, w_ukv, cos_t, sin_t, q_rank):
    b, n, cols = p1.shape
    kv_rank = cols - q_rank
    tm = _pick(n, (768, 384, 256, 128))
    kern = functools.partial(_mla_kv_kernel, q_rank=q_rank)
    return pl.pallas_call(
        kern,
        grid=(b, n // tm),
        in_specs=[
            pl.BlockSpec((1, tm, cols), lambda i, t: (i, t, 0)),
            pl.BlockSpec((1, tm, LANES), lambda i, t: (i, t, 0)),
            pl.BlockSpec((1, kv_rank), lambda i, t: (0, 0)),
            pl.BlockSpec(w_ukv.shape, lambda i, t: (0, 0)),
            pl.BlockSpec((tm, LANES), lambda i, t: (t, 0)),
            pl.BlockSpec((tm, LANES), lambda i, t: (t, 0)),
        ],
        out_specs=[
            pl.BlockSpec((1, MLA_HEADS, tm, MLA_QK), lambda i, t: (i, 0, t, 0)),
            pl.BlockSpec((1, MLA_HEADS, tm, MLA_V), lambda i, t: (i, 0, t, 0)),
        ],
        out_shape=[
            jax.ShapeDtypeStruct((b, MLA_HEADS, n, MLA_QK), BF16),
            jax.ShapeDtypeStruct((b, MLA_HEADS, n, MLA_V), BF16),
        ],
        compiler_params=_params(("arbitrary", "arbitrary")),
        name="mla_kv",
    )(p1, s1, g, w_ukv, cos_t, sin_t)


def _pad_lanes(v, width=LANES):
    v = v.reshape(1, -1).astype(F32)
    return jnp.pad(v, ((0, 0), (0, width - v.shape[1])))


def kernel(x, c, ctx, c_ctx, ada_w, ada_b, norm1_g, norm2_g, ffn_w_up, ffn_w_down, ab_w_in, ab_w_out, ssd_conv_w, ssd_conv_b, ssd_a_log, ssd_dt_bias, ssd_d, ssd_norm_g, att_q_g, att_k_g, mla_w_in, mla_q_norm_g, mla_w_uq, mla_kv_norm_g, mla_w_ukv, mla_w_o, final_norm_g):
    b, t_lat, d = x.shape
    n_ctx = ctx.shape[1]
    rows = t_lat // GRID_W
    hid = ffn_w_down.shape[1]
    row2 = lambda v: v.reshape(1, -1).astype(F32)

    bp = -(-(b + 1) // 8) * 8
    cvec = jnp.concatenate([c, c_ctx[None, :], jnp.zeros((bp - b - 1, d), F32)], axis=0)
    mods = _ada_mods(cvec, ada_w, ada_b)
    mods_l = [mods[i, :b].reshape(b, ADA_CHUNKS, d) for i in range(2)]
    mods_c = [mods[i, b].reshape(ADA_CHUNKS, d) for i in range(2)]

    xa = jnp.concatenate([x, ctx], axis=1)

    w_in = ab_w_in[0]
    o_xbc, o_dt = SSD_INNER, SSD_INNER + SSD_CONV_DIM
    o_att = o_dt + 2 * SSD_HEADS
    w_main = jnp.concatenate([w_in[:, o_xbc:o_dt], w_in[:, :SSD_INNER], w_in[:, o_att:]], axis=1).astype(BF16)
    w_dt = jnp.zeros((d, 2 * LANES), F32)
    w_dt = w_dt.at[:, :SSD_HEADS].set(w_in[:, o_dt:o_dt + SSD_HEADS])
    w_dt = w_dt.at[:, LANES:LANES + SSD_HEADS].set(w_in[:, o_dt + SSD_HEADS:o_att]).astype(BF16)
    p0, s0 = _in_proj(xa, mods_l[0], mods_c[0], row2(norm1_g[0]), w_main, w_dt, t_lat)

    conv_w = ssd_conv_w[0].astype(F32)
    conv_b = row2(ssd_conv_b[0])
    yf = _ssd_direction(p0, s0, conv_w, conv_b, _pad_lanes(ssd_dt_bias[0, 0]), _pad_lanes(ssd_a_log[0, 0]),
                        t_lat, reverse=False)
    d_skip = jnp.repeat(ssd_d[0].astype(F32), SSD_HEADDIM).reshape(1, SSD_INNER)
    y_ssd = _ssd_direction(p0, s0, conv_w, conv_b, _pad_lanes(ssd_dt_bias[0, 1]), _pad_lanes(ssd_a_log[0, 1]),
                           t_lat, reverse=True, extra=(yf, d_skip, row2(ssd_norm_g[0])))

    q_col = SSD_CONV_DIM + SSD_INNER
    k_col = q_col + ATT_Q
    cos_a, sin_a = _rope_tables(rows, ATT_HEADDIM, n_ctx)
    hpb = LANES // ATT_HEADDIM
    qh, kh, vh = _qk_prep(p0, cos_a, sin_a, jnp.tile(row2(att_q_g[0]), (1, hpb)),
                          jnp.tile(row2(att_k_g[0]), (1, hpb)), q_col, k_col, k_col + ATT_KV)
    o_att_l0 = _attention(qh, kh, vh, t_lat, ctx_self=True)

    w_out = ab_w_out[0].astype(BF16)
    xa = _out_proj(xa, mods_l[0], mods_c[0], [y_ssd, o_att_l0], [w_out[:SSD_INNER], w_out[SSD_INNER:]],
                   t_lat + n_ctx, t_lat)
    w_up = ffn_w_up[0].astype(BF16)
    xa = _ffn(xa, mods_l[0], mods_c[0], row2(norm2_g[0]), w_up[:, :hid], w_up[:, hid:],
              ffn_w_down[0].astype(BF16), row2(final_norm_g), t_lat, final=False)

    q_rank = mla_q_norm_g.shape[1]
    kv_rank = mla_kv_norm_g.shape[1]
    w_in1 = mla_w_in[0]
    w_pe = jnp.pad(w_in1[:, q_rank + kv_rank:], ((0, 0), (0, LANES - MLA_ROPE))).astype(BF16)
    p1, s1 = _in_proj(xa, mods_l[1], mods_c[1], row2(norm1_g[1]), w_in1[:, :q_rank + kv_rank].astype(BF16), w_pe,
                      t_lat)
    cos_m, sin_m = _rope_tables(rows, MLA_ROPE, n_ctx)
    w_uq = mla_w_uq[0].reshape(q_rank, MLA_HEADS, MLA_QK)
    w_uq = jnp.concatenate([w_uq[:, :, :MLA_NOPE].reshape(q_rank, -1), w_uq[:, :, MLA_NOPE:].reshape(q_rank, -1)],
                           axis=1).astype(BF16)
    w_ukv = mla_w_ukv[0].reshape(kv_rank, MLA_HEADS, MLA_NOPE + MLA_V)
    w_ukv = jnp.concatenate([w_ukv[:, :, :MLA_NOPE].reshape(kv_rank, -1), w_ukv[:, :, MLA_NOPE:].reshape(kv_rank, -1)],
                            axis=1).astype(BF16)
    qm = _mla_q(p1, row2(mla_q_norm_g[0]), w_uq, cos_m, sin_m, t_lat, q_rank)
    km, vm = _mla_kv(p1, s1, row2(mla_kv_norm_g[0]), w_ukv, cos_m, sin_m, q_rank)
    o_mla = _attention(qm, km, vm, t_lat, ctx_self=False)

    xl = _out_proj(xa, mods_l[1], mods_c[1], [o_mla], [mla_w_o[0].astype(BF16)], t_lat, t_lat)
    w_up = ffn_w_up[1].astype(BF16)
    return _ffn(xl, mods_l[1], mods_c[1], row2(norm2_g[1]), w_up[:, :hid], w_up[:, hid:],
                ffn_w_down[1].astype(BF16), row2(final_norm_g), t_lat, final=True)
```

```python
import functools
import math

import jax
import jax.numpy as jnp
from jax import lax
from jax.experimental import pallas as pl
from jax.experimental.pallas import tpu as pltpu

F32 = jnp.float32
BF16 = jnp.bfloat16

EPS = 1e-6
ROPE_THETA = 10000.0
GRID_W = 64
ADA_CHUNKS = 6

SSD_HEADS = 16
SSD_HEADDIM = 64
SSD_GROUPS = 4
SSD_STATE = 128
SSD_CHUNK = 128
SSD_INNER = SSD_HEADS * SSD_HEADDIM
SSD_GN = SSD_GROUPS * SSD_STATE
SSD_CONV_DIM = SSD_INNER + 2 * SSD_GN

ATT_HEADS = 16
ATT_KV_HEADS = 4
ATT_HEADDIM = 64
ATT_Q = ATT_HEADS * ATT_HEADDIM
ATT_KV = ATT_KV_HEADS * ATT_HEADDIM

MLA_HEADS = 16
MLA_NOPE = 64
MLA_ROPE = 32
MLA_V = 64
MLA_QK = MLA_NOPE + MLA_ROPE

LANES = 128
HEAD_BLOCK = 8
ATTN_ROW_BLOCK = 256
LOG2E = math.log2(math.e)
VMEM_LIMIT = 56 * 1024 * 1024


def _pick(n, candidates):
    for c in candidates:
        if n % c == 0:
            return c
    raise ValueError(f"no tile for {n} in {candidates}")


def _params(sem):
    return pltpu.CompilerParams(dimension_semantics=sem, vmem_limit_bytes=VMEM_LIMIT)


def _sigmoid(x):
    return 1.0 / (1.0 + jnp.exp(-x))


def _silu(x):
    return x * _sigmoid(x)


def _rms(x, g):
    ms = jnp.mean(x * x, axis=-1, keepdims=True)
    return x * lax.rsqrt(ms + EPS) * g


def _mod_row(ml_ref, mc_ref, idx, is_lat):
    return jnp.where(is_lat, ml_ref[0, idx:idx + 1, :], mc_ref[idx:idx + 1, :])


def _is_lat(t, tm, n_lat):
    rows = t * tm + lax.broadcasted_iota(jnp.int32, (tm, 1), 0)
    return rows < n_lat


def _ada_kernel(c_ref, w_ref, b_ref, o_ref):
    s = _silu(c_ref[...]).astype(BF16)
    o_ref[0] = jnp.dot(s, w_ref[0].astype(BF16), preferred_element_type=F32) + b_ref[0]


def _ada_mods(cvec, ada_w, ada_b):
    depth, d, n = ada_w.shape
    bp = cvec.shape[0]
    tn = _pick(n, (1536, 1024, 512, 256, 128))
    return pl.pallas_call(
        _ada_kernel,
        grid=(depth, n // tn),
        in_specs=[
            pl.BlockSpec((bp, d), lambda i, j: (0, 0)),
            pl.BlockSpec((1, d, tn), lambda i, j: (i, 0, j)),
            pl.BlockSpec((1, 1, tn), lambda i, j: (i, 0, j)),
        ],
        out_specs=pl.BlockSpec((1, bp, tn), lambda i, j: (i, 0, j)),
        out_shape=jax.ShapeDtypeStruct((depth, bp, n), F32),
        compiler_params=_params(("arbitrary", "arbitrary")),
        name="ada_mods",
    )(cvec, ada_w, ada_b.reshape(depth, 1, n))


def _in_proj_kernel(x_ref, ml_ref, mc_ref, g_ref, w_ref, ws_ref, o_ref, os_ref, h_ref, *, tm, n_lat):
    t = pl.program_id(1)
    j = pl.program_id(2)

    @pl.when(j == 0)
    def _():
        is_lat = _is_lat(t, tm, n_lat)
        h = _rms(x_ref[0], g_ref[...])
        h = h * (1.0 + _mod_row(ml_ref, mc_ref, 1, is_lat)) + _mod_row(ml_ref, mc_ref, 0, is_lat)
        hb = h.astype(BF16)
        h_ref[...] = hb
        os_ref[0] = jnp.dot(hb, ws_ref[...], preferred_element_type=F32)

    o_ref[0] = jnp.dot(h_ref[...], w_ref[...], preferred_element_type=F32).astype(o_ref.dtype)


def _in_proj(xa, mods_l, mods_c, g, w_main, w_side, n_lat):
    b, n, d = xa.shape
    nm = w_main.shape[1]
    ns = w_side.shape[1]
    tm = _pick(n, (768, 384, 256, 128))
    tn = _pick(nm, (1536, 1024, 768, 640, 512, 256, 128))
    kern = functools.partial(_in_proj_kernel, tm=tm, n_lat=n_lat)
    return pl.pallas_call(
        kern,
        grid=(b, n // tm, nm // tn),
        in_specs=[
            pl.BlockSpec((1, tm, d), lambda i, t, j: (i, t, 0)),
            pl.BlockSpec((1, ADA_CHUNKS, d), lambda i, t, j: (i, 0, 0)),
            pl.BlockSpec((ADA_CHUNKS, d), lambda i, t, j: (0, 0)),
            pl.BlockSpec((1, d), lambda i, t, j: (0, 0)),
            pl.BlockSpec((d, tn), lambda i, t, j: (0, j)),
            pl.BlockSpec((d, ns), lambda i, t, j: (0, 0)),
        ],
        out_specs=[
            pl.BlockSpec((1, tm, tn), lambda i, t, j: (i, t, j)),
            pl.BlockSpec((1, tm, ns), lambda i, t, j: (i, t, 0)),
        ],
        out_shape=[
            jax.ShapeDtypeStruct((b, n, nm), BF16),
            jax.ShapeDtypeStruct((b, n, ns), F32),
        ],
        scratch_shapes=[pltpu.VMEM((tm, d), BF16)],
        compiler_params=_params(("arbitrary", "arbitrary", "arbitrary")),
        name="in_proj",
    )(xa, mods_l, mods_c, g, w_main, w_side)


def _cumsum_rows(v, reverse):
    n = v.shape[0]
    row = lax.broadcasted_iota(jnp.int32, v.shape, 0)
    k = 1
    while k < n:
        if reverse:
            v = v + jnp.where(row < n - k, pltpu.roll(v, n - k, axis=0), 0.0)
        else:
            v = v + jnp.where(row >= k, pltpu.roll(v, k, axis=0), 0.0)
        k *= 2
    return v


def _ssd_chunk_index(s, nl, nc, reverse):
    if reverse:
        return nc - 1 - s
    return lax.rem(s + nl, nc)


def _ssd_kernel(*refs, reverse, finish, nl, nc):
    if finish:
        (xm_ref, xp_ref, xn_ref, dt_ref, cw_ref, cb_ref, dtb_ref, alog_ref,
         z_ref, yf_ref, dsk_ref, ng_ref, o_ref, st_ref) = refs
    else:
        (xm_ref, xp_ref, xn_ref, dt_ref, cw_ref, cb_ref, dtb_ref, alog_ref, o_ref, st_ref) = refs
    q = SSD_CHUNK
    p = SSD_HEADDIM
    hpg = SSD_HEADS // SSD_GROUPS
    gw = hpg * p
    s = pl.program_id(1)
    chunk = _ssd_chunk_index(s, nl, nc, reverse)

    @pl.when(s == 0)
    def _():
        st_ref[...] = jnp.zeros_like(st_ref)

    xm = xm_ref[0].astype(F32)
    halo = xp_ref.shape[1]
    seq_first = jnp.logical_or(chunk == 0, chunk == nl)
    seq_last = jnp.logical_or(chunk == nl - 1, chunk == nc - 1)
    prev_row = xp_ref[0, halo - 1:halo, :].astype(F32) * jnp.where(seq_first, 0.0, 1.0)
    next_row = xn_ref[0, 0:1, :].astype(F32) * jnp.where(seq_last, 0.0, 1.0)
    row = lax.broadcasted_iota(jnp.int32, (q, 1), 0)
    x_prev = jnp.where(row == 0, prev_row, pltpu.roll(xm, 1, axis=0))
    x_next = jnp.where(row == q - 1, next_row, pltpu.roll(xm, q - 1, axis=0))
    xc = x_prev * cw_ref[0:1, :] + xm * cw_ref[1:2, :] + x_next * cw_ref[2:3, :] + cb_ref[...]
    act = _silu(xc)
    xs = act[:, :SSD_INNER]
    bm = act[:, SSD_INNER:SSD_INNER + SSD_GN]
    cm = act[:, SSD_INNER + SSD_GN:]

    dtr = dt_ref[0] + dtb_ref[...]
    dt = jnp.maximum(dtr, 0.0) + jnp.log1p(jnp.exp(-jnp.abs(dtr)))
    a = -jnp.exp(alog_ref[...])
    acs = _cumsum_rows(dt * a, reverse)
    acs_t = acs.T
    edge = acs[0:1, :] if reverse else acs[q - 1:q, :]
    eacs = jnp.exp(acs)
    dtd = dt * jnp.exp(edge - acs)
    cdec = jnp.exp(edge)

    ii = lax.broadcasted_iota(jnp.int32, (q, q), 0)
    jj = lax.broadcasted_iota(jnp.int32, (q, q), 1)
    mask = (ii <= jj) if reverse else (ii >= jj)

    def expand(v, g):
        return jnp.concatenate(
            [jnp.broadcast_to(v[:, g * hpg + k:g * hpg + k + 1], (v.shape[0], p)) for k in range(hpg)], axis=1)

    ys = []
    for g in range(SSD_GROUPS):
        cg = cm[:, g * SSD_STATE:(g + 1) * SSD_STATE].astype(BF16)
        bg32 = bm[:, g * SSD_STATE:(g + 1) * SSD_STATE]
        bg = bg32.astype(BF16)
        cb = lax.dot_general(cg, bg, (((1,), (1,)), ((), ())), preferred_element_type=F32)
        xg = xs[:, g * gw:(g + 1) * gw]
        xdt = (xg * expand(dt, g)).astype(BF16)
        yd = []
        for k in range(hpg):
            h = g * hpg + k
            seg = jnp.broadcast_to(acs[:, h:h + 1], (q, q)) - jnp.broadcast_to(acs_t[h:h + 1, :], (q, q))
            lmat = jnp.exp(jnp.where(mask, seg, -jnp.inf))
            m = (cb * lmat).astype(BF16)
            yd.append(jnp.dot(m, xdt[:, k * p:(k + 1) * p], preferred_element_type=F32))
        st = st_ref[g]
        y_off = jnp.dot(cg, st.astype(BF16), preferred_element_type=F32) * expand(eacs, g)
        xdd = (xg * expand(dtd, g)).astype(BF16)
        new = jnp.dot(bg32.T.astype(BF16), xdd, preferred_element_type=F32)
        st_ref[g] = st * expand(cdec, g) + new
        ys.append(jnp.concatenate(yd, axis=1) + y_off)
    y = jnp.concatenate(ys, axis=1)

    if finish:
        y = yf_ref[0] + y + xs * dsk_ref[...]
        y = y * _silu(z_ref[0].astype(F32))
        o_ref[0] = _rms(y, ng_ref[...]).astype(o_ref.dtype)
    else:
        o_ref[0] = y


def _ssd_direction(p0, s0, conv_w, conv_b, dt_bias, a_log, n_lat, reverse, extra=None):
    b, n, _ = p0.shape
    q = SSD_CHUNK
    nc = n // q
    nl = n_lat // q
    halo = 16
    hpc = q // halo
    nh = n // halo
    d = 1 if reverse else 0
    cidx = functools.partial(_ssd_chunk_index, nl=nl, nc=nc, reverse=reverse)
    finish = extra is not None
    in_specs = [
        pl.BlockSpec((1, q, SSD_CONV_DIM), lambda i, s: (i, cidx(s), 0)),
        pl.BlockSpec((1, halo, SSD_CONV_DIM), lambda i, s: (i, jnp.maximum(cidx(s) * hpc - 1, 0), 0)),
        pl.BlockSpec((1, halo, SSD_CONV_DIM), lambda i, s: (i, jnp.minimum(cidx(s) * hpc + hpc, nh - 1), 0)),
        pl.BlockSpec((1, q, LANES), lambda i, s: (i, cidx(s), d)),
        pl.BlockSpec((3, SSD_CONV_DIM), lambda i, s: (0, 0)),
        pl.BlockSpec((1, SSD_CONV_DIM), lambda i, s: (0, 0)),
        pl.BlockSpec((1, LANES), lambda i, s: (0, 0)),
        pl.BlockSpec((1, LANES), lambda i, s: (0, 0)),
    ]
    args = [p0, p0, p0, s0, conv_w, conv_b, dt_bias, a_log]
    if finish:
        yf, d_skip, norm_g = extra
        in_specs += [
            pl.BlockSpec((1, q, SSD_INNER), lambda i, s: (i, cidx(s), SSD_CONV_DIM // SSD_INNER)),
            pl.BlockSpec((1, q, SSD_INNER), lambda i, s: (i, cidx(s), 0)),
            pl.BlockSpec((1, SSD_INNER), lambda i, s: (0, 0)),
            pl.BlockSpec((1, SSD_INNER), lambda i, s: (0, 0)),
        ]
        args += [p0, yf, d_skip, norm_g]
    kern = functools.partial(_ssd_kernel, reverse=reverse, finish=finish, nl=nl, nc=nc)
    return pl.pallas_call(
        kern,
        grid=(b, nc),
        in_specs=in_specs,
        out_specs=pl.BlockSpec((1, q, SSD_INNER), lambda i, s: (i, cidx(s), 0)),
        out_shape=jax.ShapeDtypeStruct((b, n, SSD_INNER), BF16 if finish else F32),
        scratch_shapes=[pltpu.VMEM((SSD_GROUPS, SSD_STATE, (SSD_HEADS // SSD_GROUPS) * SSD_HEADDIM), F32)],
        compiler_params=_params(("arbitrary", "arbitrary")),
        name="ssd_bwd_finish" if finish else "ssd_fwd",
    )(*args)


def _rope_tables(rows, rot_dim, n_ctx):
    n_freq = rot_dim // 4
    row = jnp.repeat(jnp.arange(rows, dtype=F32), GRID_W)
    col = jnp.tile(jnp.arange(GRID_W, dtype=F32), rows)
    inv = ROPE_THETA ** (-jnp.arange(n_freq, dtype=F32) / n_freq)
    ang = jnp.concatenate([row[:, None] * inv, col[:, None] * inv], axis=-1)
    cos, sin = jnp.cos(ang), jnp.sin(ang)
    reps = LANES // rot_dim
    cos_t = jnp.tile(jnp.concatenate([cos, cos], axis=-1), (1, reps))
    sin_t = jnp.tile(jnp.concatenate([-sin, sin], axis=-1), (1, reps))
    cos_t = jnp.concatenate([cos_t, jnp.ones((n_ctx, LANES), F32)], axis=0)
    sin_t = jnp.concatenate([sin_t, jnp.zeros((n_ctx, LANES), F32)], axis=0)
    return cos_t, sin_t


def _rope_block(xb, cos, sin, rot_dim):
    half = rot_dim // 2
    lane = lax.broadcasted_iota(jnp.int32, xb.shape, 1)
    first = jnp.bitwise_and(lane, rot_dim - 1) < half
    partner = jnp.where(first, pltpu.roll(xb, LANES - half, axis=1), pltpu.roll(xb, half, axis=1))
    return xb * cos + partner * sin


def _segment_mean_matrix(seg):
    sh = seg.bit_length() - 1
    i = jnp.right_shift(lax.broadcasted_iota(jnp.int32, (LANES, LANES), 0), sh)
    j = jnp.right_shift(lax.broadcasted_iota(jnp.int32, (LANES, LANES), 1), sh)
    return jnp.where(i == j, 1.0 / seg, 0.0).astype(BF16)


def _segment_mean_sq(xb, smat):
    x2 = xb * xb
    hi = x2.astype(BF16)
    lo = (x2 - hi.astype(F32)).astype(BF16)
    return (jnp.dot(hi, smat, preferred_element_type=F32) + jnp.dot(lo, smat, preferred_element_type=F32))


def _qk_prep_kernel(q_ref, k_ref, v_ref, cos_ref, sin_ref, qg_ref, kg_ref, qo_ref, ko_ref, vo_ref, *, scale):
    smat = _segment_mean_matrix(ATT_HEADDIM)
    cos = cos_ref[...]
    sin = sin_ref[...]
    hpb = LANES // ATT_HEADDIM

    def prep(src_ref, g_ref, dst_ref, n_heads, mul, transposed):
        for c in range(n_heads // hpb):
            xb = src_ref[0, :, c * LANES:(c + 1) * LANES].astype(F32)
            xn = xb * lax.rsqrt(_segment_mean_sq(xb, smat) + EPS) * g_ref[...]
            r = _rope_block(xn, cos, sin, ATT_HEADDIM)
            if mul != 1.0:
                r = r * mul
            if transposed:
                r = r.T
            r = r.astype(dst_ref.dtype)
            for k in range(hpb):
                if transposed:
                    dst_ref[0, c * hpb + k] = r[k * ATT_HEADDIM:(k + 1) * ATT_HEADDIM, :]
                else:
                    dst_ref[0, c * hpb + k] = r[:, k * ATT_HEADDIM:(k + 1) * ATT_HEADDIM]

    prep(q_ref, qg_ref, qo_ref, ATT_HEADS, scale, False)
    prep(k_ref, kg_ref, ko_ref, ATT_KV_HEADS, 1.0, True)
    for h in range(ATT_KV_HEADS):
        vo_ref[0, h] = v_ref[0, :, h * ATT_HEADDIM:(h + 1) * ATT_HEADDIM]


def _qk_prep(p0, cos_t, sin_t, q_g, k_g, q_col, k_col, v_col):
    b, n, _ = p0.shape
    tm = _pick(n, (768, 384, 256, 128))
    kern = functools.partial(_qk_prep_kernel, scale=ATT_HEADDIM ** -0.5 * LOG2E)
    return pl.pallas_call(
        kern,
        grid=(b, n // tm),
        in_specs=[
            pl.BlockSpec((1, tm, ATT_Q), lambda i, t: (i, t, q_col // ATT_Q)),
            pl.BlockSpec((1, tm, ATT_KV), lambda i, t: (i, t, k_col // ATT_KV)),
            pl.BlockSpec((1, tm, ATT_KV), lambda i, t: (i, t, v_col // ATT_KV)),
            pl.BlockSpec((tm, LANES), lambda i, t: (t, 0)),
            pl.BlockSpec((tm, LANES), lambda i, t: (t, 0)),
            pl.BlockSpec((1, LANES), lambda i, t: (0, 0)),
            pl.BlockSpec((1, LANES), lambda i, t: (0, 0)),
        ],
        out_specs=[
            pl.BlockSpec((1, ATT_HEADS, tm, ATT_HEADDIM), lambda i, t: (i, 0, t, 0)),
            pl.BlockSpec((1, ATT_KV_HEADS, ATT_HEADDIM, tm), lambda i, t: (i, 0, 0, t)),
            pl.BlockSpec((1, ATT_KV_HEADS, tm, ATT_HEADDIM), lambda i, t: (i, 0, t, 0)),
        ],
        out_shape=[
            jax.ShapeDtypeStruct((b, ATT_HEADS, n, ATT_HEADDIM), BF16),
            jax.ShapeDtypeStruct((b, ATT_KV_HEADS, ATT_HEADDIM, n), BF16),
            jax.ShapeDtypeStruct((b, ATT_KV_HEADS, n, ATT_HEADDIM), BF16),
        ],
        compiler_params=_params(("arbitrary", "arbitrary")),
        name="gqa_qk_prep",
    )(p0, p0, p0, cos_t, sin_t, q_g, k_g)


def _attn_kernel(q_ref, k_ref, v_ref, o_ref, s_scr, *, group, n_lat, n_all, lat_tiles, ctx_self):
    t = pl.program_id(2)
    tq = q_ref.shape[2]
    dv = v_ref.shape[3]
    rb = s_scr.shape[1]
    blocks = [(u, r) for u in range(HEAD_BLOCK) for r in range(tq // rb)]

    def run(k_lo, k_len):
        def scores(i):
            u, r = blocks[i]
            q = q_ref[0, u, r * rb:(r + 1) * rb, :]
            kt = k_ref[0, u // group, :, k_lo:k_lo + k_len]
            s_scr[i % 2, :, :k_len] = jnp.dot(q, kt, preferred_element_type=F32)

        def combine(i):
            u, r = blocks[i]
            s = s_scr[i % 2, :, :k_len]
            m = jnp.max(s, axis=-1, keepdims=True)
            e = jnp.exp2(s - m)
            l = jnp.sum(e, axis=-1, keepdims=True)
            vv = v_ref[0, u // group, k_lo:k_lo + k_len, :]
            o = jnp.dot(e.astype(BF16), vv, preferred_element_type=F32) * (1.0 / l)
            o_ref[0, r * rb:(r + 1) * rb, u * dv:(u + 1) * dv] = o.astype(o_ref.dtype)

        scores(0)
        for i in range(len(blocks)):
            if i + 1 < len(blocks):
                scores(i + 1)
            combine(i)

    if ctx_self:
        @pl.when(t < lat_tiles)
        def _():
            run(0, n_all)

        @pl.when(t >= lat_tiles)
        def _():
            run(n_lat, n_all - n_lat)
    else:
        run(0, n_all)


def _attention(q, k, v, n_lat, ctx_self):
    b, h, nq, dqk = q.shape
    hkv, n_all, dv = v.shape[1], v.shape[2], v.shape[3]
    group = h // hkv
    kvb = max(HEAD_BLOCK // group, 1)
    tq = _pick(math.gcd(n_lat, n_all - n_lat) if ctx_self else n_lat, (256, 128))
    assert nq == (n_all if ctx_self else n_lat)
    kern = functools.partial(_attn_kernel, group=group, n_lat=n_lat, n_all=n_all,
                             lat_tiles=n_lat // tq, ctx_self=ctx_self)
    return pl.pallas_call(
        kern,
        grid=(b, h // HEAD_BLOCK, nq // tq),
        in_specs=[
            pl.BlockSpec((1, HEAD_BLOCK, tq, dqk), lambda i, hb, t: (i, hb, t, 0)),
            pl.BlockSpec((1, kvb, dqk, n_all), lambda i, hb, t: (i, hb, 0, 0)),
            pl.BlockSpec((1, kvb, n_all, dv), lambda i, hb, t: (i, hb, 0, 0)),
        ],
        out_specs=pl.BlockSpec((1, tq, HEAD_BLOCK * dv), lambda i, hb, t: (i, t, hb)),
        out_shape=jax.ShapeDtypeStruct((b, nq, h * dv), BF16),
        scratch_shapes=[pltpu.VMEM((2, min(tq, ATTN_ROW_BLOCK), n_all), F32)],
        compiler_params=_params(("arbitrary", "arbitrary", "arbitrary")),
        name="attention",
    )(q, k, v)


def _out_proj_kernel(*refs, n_in, tm, n_lat):
    x_ref, ml_ref, mc_ref = refs[:3]
    a_refs = refs[3:3 + n_in]
    w_refs = refs[3 + n_in:3 + 2 * n_in]
    o_ref = refs[3 + 2 * n_in]
    t = pl.program_id(1)
    acc = jnp.dot(a_refs[0][0], w_refs[0][...], preferred_element_type=F32)
    for a_ref, w_ref in zip(a_refs[1:], w_refs[1:]):
        acc = acc + jnp.dot(a_ref[0], w_ref[...], preferred_element_type=F32)
    gate = _mod_row(ml_ref, mc_ref, 2, _is_lat(t, tm, n_lat))
    o_ref[0] = x_ref[0] + gate * acc


def _out_proj(xa, mods_l, mods_c, acts, weights, n_rows, n_lat):
    b, _, d = xa.shape
    tm = _pick(n_rows, (1024, 768, 512, 384, 256, 128))
    n_in = len(acts)
    kern = functools.partial(_out_proj_kernel, n_in=n_in, tm=tm, n_lat=n_lat)
    in_specs = [
        pl.BlockSpec((1, tm, d), lambda i, t: (i, t, 0)),
        pl.BlockSpec((1, ADA_CHUNKS, d), lambda i, t: (i, 0, 0)),
        pl.BlockSpec((ADA_CHUNKS, d), lambda i, t: (0, 0)),
    ]
    in_specs += [pl.BlockSpec((1, tm, a.shape[2]), lambda i, t: (i, t, 0)) for a in acts]
    in_specs += [pl.BlockSpec(w.shape, lambda i, t: (0, 0)) for w in weights]
    return pl.pallas_call(
        kern,
        grid=(b, n_rows // tm),
        in_specs=in_specs,
        out_specs=pl.BlockSpec((1, tm, d), lambda i, t: (i, t, 0)),
        out_shape=jax.ShapeDtypeStruct((b, n_rows, d), F32),
        compiler_params=_params(("arbitrary", "arbitrary")),
        name="out_proj",
    )(xa, mods_l, mods_c, *acts, *weights)


def _hidden_chunks(hidden):
    chunks, lo = [], 0
    while lo < hidden:
        w = min(1024, hidden - lo)
        chunks.append((lo, w))
        lo += w
    return chunks


def _ffn_kernel(x_ref, ml_ref, mc_ref, g_ref, wg_ref, wu_ref, wd_ref, fg_ref, o_ref, *, tm, n_lat, final):
    t = pl.program_id(1)
    is_lat = _is_lat(t, tm, n_lat)
    x = x_ref[0]
    h = _rms(x, g_ref[...])
    h = (h * (1.0 + _mod_row(ml_ref, mc_ref, 4, is_lat)) + _mod_row(ml_ref, mc_ref, 3, is_lat)).astype(BF16)
    acc = None
    for lo, w in _hidden_chunks(wd_ref.shape[0]):
        gate = jnp.dot(h, wg_ref[:, lo:lo + w], preferred_element_type=F32)
        up = jnp.dot(h, wu_ref[:, lo:lo + w], preferred_element_type=F32)
        a = (_silu(gate) * up).astype(BF16)
        part = jnp.dot(a, wd_ref[lo:lo + w, :], preferred_element_type=F32)
        acc = part if acc is None else acc + part
    y = x + _mod_row(ml_ref, mc_ref, 5, is_lat) * acc
    if final:
        y = _rms(y, fg_ref[...])
    o_ref[0] = y


def _ffn(xa, mods_l, mods_c, g, w_gate, w_up, w_down, final_g, n_lat, final):
    b, n, d = xa.shape
    hid = w_down.shape[0]
    tm = _pick(n, (768, 512, 384, 256, 128))
    kern = functools.partial(_ffn_kernel, tm=tm, n_lat=n_lat, final=final)
    resident = dict(pipeline_mode=pl.Buffered(1))
    return pl.pallas_call(
        kern,
        grid=(b, n // tm),
        in_specs=[
            pl.BlockSpec((1, tm, d), lambda i, t: (i, t, 0)),
            pl.BlockSpec((1, ADA_CHUNKS, d), lambda i, t: (i, 0, 0)),
            pl.BlockSpec((ADA_CHUNKS, d), lambda i, t: (0, 0)),
            pl.BlockSpec((1, d), lambda i, t: (0, 0)),
            pl.BlockSpec((d, hid), lambda i, t: (0, 0), **resident),
            pl.BlockSpec((d, hid), lambda i, t: (0, 0), **resident),
            pl.BlockSpec((hid, d), lambda i, t: (0, 0), **resident),
            pl.BlockSpec((1, d), lambda i, t: (0, 0)),
        ],
        out_specs=pl.BlockSpec((1, tm, d), lambda i, t: (i, t, 0)),
        out_shape=jax.ShapeDtypeStruct((b, n, d), F32),
        compiler_params=_params(("arbitrary", "arbitrary")),
        name="ffn_final" if final else "ffn",
    )(xa, mods_l, mods_c, g, w_gate, w_up, w_down, final_g)


def _mla_q_kernel(p_ref, g_ref, w_ref, cos_ref, sin_ref, o_ref, *, q_rank, scale):
    cq = p_ref[0, :, :q_rank].astype(F32)
    qn = _rms(cq, g_ref[...]).astype(BF16)
    qq = jnp.dot(qn, w_ref[...], preferred_element_type=F32)
    n_nope = MLA_HEADS * MLA_NOPE
    hpb = LANES // MLA_ROPE
    pes = []
    for c in range(MLA_HEADS // hpb):
        pes.append(_rope_block(qq[:, n_nope + c * LANES:n_nope + (c + 1) * LANES],
                               cos_ref[...], sin_ref[...], MLA_ROPE))
    for h in range(MLA_HEADS):
        pe = pes[h // hpb][:, (h % hpb) * MLA_ROPE:(h % hpb + 1) * MLA_ROPE]
        qh = jnp.concatenate([qq[:, h * MLA_NOPE:(h + 1) * MLA_NOPE], pe], axis=1) * scale
        o_ref[0, h] = qh.astype(o_ref.dtype)


def _mla_q(p1, g, w_uq, cos_t, sin_t, n_lat, q_rank):
    b = p1.shape[0]
    cols = p1.shape[2]
    tm = _pick(n_lat, (512, 256, 128))
    kern = functools.partial(_mla_q_kernel, q_rank=q_rank, scale=MLA_QK ** -0.5 * LOG2E)
    return pl.pallas_call(
        kern,
        grid=(b, n_lat // tm),
        in_specs=[
            pl.BlockSpec((1, tm, cols), lambda i, t: (i, t, 0)),
            pl.BlockSpec((1, q_rank), lambda i, t: (0, 0)),
            pl.BlockSpec(w_uq.shape, lambda i, t: (0, 0)),
            pl.BlockSpec((tm, LANES), lambda i, t: (t, 0)),
            pl.BlockSpec((tm, LANES), lambda i, t: (t, 0)),
        ],
        out_specs=pl.BlockSpec((1, MLA_HEADS, tm, MLA_QK), lambda i, t: (i, 0, t, 0)),
        out_shape=jax.ShapeDtypeStruct((b, MLA_HEADS, n_lat, MLA_QK), BF16),
        compiler_params=_params(("arbitrary", "arbitrary")),
        name="mla_q",
    )(p1, g, w_uq, cos_t, sin_t)


def _mla_kv_kernel(p_ref, pe_ref, g_ref, w_ref, cos_ref, sin_ref, ko_ref, vo_ref, *, q_rank):
    ckv = p_ref[0, :, q_rank:].astype(F32)
    kvn = _rms(ckv, g_ref[...]).astype(BF16)
    kv = jnp.dot(kvn, w_ref[...], preferred_element_type=F32)
    pe_t = _rope_block(pe_ref[0], cos_ref[...], sin_ref[...], MLA_ROPE).T[:MLA_ROPE, :]
    n_nope = MLA_HEADS * MLA_NOPE
    hpb = LANES // MLA_NOPE
    for c in range(MLA_HEADS // hpb):
        nope_t = kv[:, c * LANES:(c + 1) * LANES].T
        for k in range(hpb):
            kh = jnp.concatenate([nope_t[k * MLA_NOPE:(k + 1) * MLA_NOPE, :], pe_t], axis=0)
            ko_ref[0, c * hpb + k] = kh.astype(ko_ref.dtype)
    for h in range(MLA_HEADS):
        vo_ref[0, h] = kv[:, n_nope + h * MLA_V:n_nope + (h + 1) * MLA_V].astype(vo_ref.dtype)


def _mla_kv(p1, s1, g, w_ukv, cos_t, sin_t, q_rank):
    b, n, cols = p1.shape
    kv_rank = cols - q_rank
    tm = _pick(n, (768, 384, 256, 128))
    kern = functools.partial(_mla_kv_kernel, q_rank=q_rank)
    return pl.pallas_call(
        kern,
        grid=(b, n // tm),
        in_specs=[
            pl.BlockSpec((1, tm, cols), lambda i, t: (i, t, 0)),
            pl.BlockSpec((1, tm, LANES), lambda i, t: (i, t, 0)),
            pl.BlockSpec((1, kv_rank), lambda i, t: (0, 0)),
            pl.BlockSpec(w_ukv.shape, lambda i, t: (0, 0)),
            pl.BlockSpec((tm, LANES), lambda i, t: (t, 0)),
            pl.BlockSpec((tm, LANES), lambda i, t: (t, 0)),
        ],
        out_specs=[
            pl.BlockSpec((1, MLA_HEADS, MLA_QK, tm), lambda i, t: (i, 0, 0, t)),
            pl.BlockSpec((1, MLA_HEADS, tm, MLA_V), lambda i, t: (i, 0, t, 0)),
        ],
        out_shape=[
            jax.ShapeDtypeStruct((b, MLA_HEADS, MLA_QK, n), BF16),
            jax.ShapeDtypeStruct((b, MLA_HEADS, n, MLA_V), BF16),
        ],
        compiler_params=_params(("arbitrary", "arbitrary")),
        name="mla_kv",
    )(p1, s1, g, w_ukv, cos_t, sin_t)


def _pad_lanes(v, width=LANES):
    v = v.reshape(1, -1).astype(F32)
    return jnp.pad(v, ((0, 0), (0, width - v.shape[1])))


def kernel(x, c, ctx, c_ctx, ada_w, ada_b, norm1_g, norm2_g, ffn_w_up, ffn_w_down, ab_w_in, ab_w_out, ssd_conv_w, ssd_conv_b, ssd_a_log, ssd_dt_bias, ssd_d, ssd_norm_g, att_q_g, att_k_g, mla_w_in, mla_q_norm_g, mla_w_uq, mla_kv_norm_g, mla_w_ukv, mla_w_o, final_norm_g):
    b, t_lat, d = x.shape
    n_ctx = ctx.shape[1]
    rows = t_lat // GRID_W
    hid = ffn_w_down.shape[1]
    row2 = lambda v: v.reshape(1, -1).astype(F32)

    bp = -(-(b + 1) // 8) * 8
    cvec = jnp.concatenate([c, c_ctx[None, :], jnp.zeros((bp - b - 1, d), F32)], axis=0)
    mods = _ada_mods(cvec, ada_w, ada_b)
    mods_l = [mods[i, :b].reshape(b, ADA_CHUNKS, d) for i in range(2)]
    mods_c = [mods[i, b].reshape(ADA_CHUNKS, d) for i in range(2)]

    xa = jnp.concatenate([x, ctx], axis=1)

    w_in = ab_w_in[0]
    o_xbc, o_dt = SSD_INNER, SSD_INNER + SSD_CONV_DIM
    o_att = o_dt + 2 * SSD_HEADS
    w_main = jnp.concatenate([w_in[:, o_xbc:o_dt], w_in[:, :SSD_INNER], w_in[:, o_att:]], axis=1).astype(BF16)
    w_dt = jnp.zeros((d, 2 * LANES), F32)
    w_dt = w_dt.at[:, :SSD_HEADS].set(w_in[:, o_dt:o_dt + SSD_HEADS])
    w_dt = w_dt.at[:, LANES:LANES + SSD_HEADS].set(w_in[:, o_dt + SSD_HEADS:o_att]).astype(BF16)
    p0, s0 = _in_proj(xa, mods_l[0], mods_c[0], row2(norm1_g[0]), w_main, w_dt, t_lat)

    conv_w = ssd_conv_w[0].astype(F32)
    conv_b = row2(ssd_conv_b[0])
    yf = _ssd_direction(p0, s0, conv_w, conv_b, _pad_lanes(ssd_dt_bias[0, 0]), _pad_lanes(ssd_a_log[0, 0]),
                        t_lat, reverse=False)
    d_skip = jnp.repeat(ssd_d[0].astype(F32), SSD_HEADDIM).reshape(1, SSD_INNER)
    y_ssd = _ssd_direction(p0, s0, conv_w, conv_b, _pad_lanes(ssd_dt_bias[0, 1]), _pad_lanes(ssd_a_log[0, 1]),
                           t_lat, reverse=True, extra=(yf, d_skip, row2(ssd_norm_g[0])))

    q_col = SSD_CONV_DIM + SSD_INNER
    k_col = q_col + ATT_Q
    cos_a, sin_a = _rope_tables(rows, ATT_HEADDIM, n_ctx)
    hpb = LANES // ATT_HEADDIM
    qh, kh, vh = _qk_prep(p0, cos_a, sin_a, jnp.tile(row2(att_q_g[0]), (1, hpb)),
                          jnp.tile(row2(att_k_g[0]), (1, hpb)), q_col, k_col, k_col + ATT_KV)
    o_att_l0 = _attention(qh, kh, vh, t_lat, ctx_self=True)

    w_out = ab_w_out[0].astype(BF16)
    xa = _out_proj(xa, mods_l[0], mods_c[0], [y_ssd, o_att_l0], [w_out[:SSD_INNER], w_out[SSD_INNER:]],
                   t_lat + n_ctx, t_lat)
    w_up = ffn_w_up[0].astype(BF16)
    xa = _ffn(xa, mods_l[0], mods_c[0], row2(norm2_g[0]), w_up[:, :hid], w_up[:, hid:],
              ffn_w_down[0].astype(BF16), row2(final_norm_g), t_lat, final=False)

    q_rank = mla_q_norm_g.shape[1]
    kv_rank = mla_kv_norm_g.shape[1]
    w_in1 = mla_w_in[0]
    w_pe = jnp.pad(w_in1[:, q_rank + kv_rank:], ((0, 0), (0, LANES - MLA_ROPE))).astype(BF16)
    p1, s1 = _in_proj(xa, mods_l[1], mods_c[1], row2(norm1_g[1]), w_in1[:, :q_rank + kv_rank].astype(BF16), w_pe,
                      t_lat)
    cos_m, sin_m = _rope_tables(rows, MLA_ROPE, n_ctx)
    w_uq = mla_w_uq[0].reshape(q_rank, MLA_HEADS, MLA_QK)
    w_uq = jnp.concatenate([w_uq[:, :, :MLA_NOPE].reshape(q_rank, -1), w_uq[:, :, MLA_NOPE:].reshape(q_rank, -1)],
                           axis=1).astype(BF16)
    w_ukv = mla_w_ukv[0].reshape(kv_rank, MLA_HEADS, MLA_NOPE + MLA_V)
    w_ukv = jnp.concatenate([w_ukv[:, :, :MLA_NOPE].reshape(kv_rank, -1), w_ukv[:, :, MLA_NOPE:].reshape(kv_rank, -1)],
                            axis=1).astype(BF16)
    qm = _mla_q(p1, row2(mla_q_norm_g[0]), w_uq, cos_m, sin_m, t_lat, q_rank)
    km, vm = _mla_kv(p1, s1, row2(mla_kv_norm_g[0]), w_ukv, cos_m, sin_m, q_rank)
    o_mla = _attention(qm, km, vm, t_lat, ctx_self=False)

    xl = _out_proj(xa, mods_l[1], mods_c[1], [o_mla], [mla_w_o[0].astype(BF16)], t_lat, t_lat)
    w_up = ffn_w_up[1].astype(BF16)
    return _ffn(xl, mods_l[1], mods_c[1], row2(norm2_g[1]), w_up[:, :hid], w_up[:, hid:],
                ffn_w_down[1].astype(BF16), row2(final_norm_g), t_lat, final=True)
```

```python
import functools
import math

import jax
import jax.numpy as jnp
from jax import lax
from jax.experimental import pallas as pl
from jax.experimental.pallas import tpu as pltpu

F32 = jnp.float32
BF16 = jnp.bfloat16

EPS = 1e-6
ROPE_THETA = 10000.0
GRID_W = 64
ADA_CHUNKS = 6

SSD_HEADS = 16
SSD_HEADDIM = 64
SSD_GROUPS = 4
SSD_STATE = 128
SSD_CHUNK = 128
SSD_INNER = SSD_HEADS * SSD_HEADDIM
SSD_GN = SSD_GROUPS * SSD_STATE
SSD_CONV_DIM = SSD_INNER + 2 * SSD_GN

ATT_HEADS = 16
ATT_KV_HEADS = 4
ATT_HEADDIM = 64
ATT_Q = ATT_HEADS * ATT_HEADDIM
ATT_KV = ATT_KV_HEADS * ATT_HEADDIM

MLA_HEADS = 16
MLA_NOPE = 64
MLA_ROPE = 32
MLA_V = 64
MLA_QK = MLA_NOPE + MLA_ROPE

LANES = 128
SUBLANES = 8
HEAD_BLOCK = 8
ATTN_ROW_BLOCK = 256
LOG2E = math.log2(math.e)
VMEM_LIMIT = 56 * 1024 * 1024


def _pick(n, candidates):
    for c in candidates:
        if n % c == 0:
            return c
    raise ValueError(f"no tile for {n} in {candidates}")


def _params(sem):
    return pltpu.CompilerParams(dimension_semantics=sem, vmem_limit_bytes=VMEM_LIMIT)


def _silu(x):
    hx = 0.5 * x
    return hx + hx * jnp.tanh(hx)


def _rms(x, g):
    ms = jnp.mean(x * x, axis=-1, keepdims=True)
    return x * lax.rsqrt(ms + EPS) * g


def _mod_row(ml_ref, mc_ref, idx, is_lat):
    return jnp.where(is_lat, ml_ref[0, idx:idx + 1, :], mc_ref[idx:idx + 1, :])


def _is_lat(t, tm, n_lat):
    rows = t * tm + lax.broadcasted_iota(jnp.int32, (tm, 1), 0)
    return rows < n_lat


def _ada_kernel(c_ref, w_ref, b_ref, o_ref):
    s = _silu(c_ref[...]).astype(BF16)
    o_ref[0] = jnp.dot(s, w_ref[0].astype(BF16), preferred_element_type=F32) + b_ref[0]


def _ada_mods(cvec, ada_w, ada_b):
    depth, d, n = ada_w.shape
    bp = cvec.shape[0]
    tn = _pick(n, (1536, 1024, 512, 256, 128))
    return pl.pallas_call(
        _ada_kernel,
        grid=(depth, n // tn),
        in_specs=[
            pl.BlockSpec((bp, d), lambda i, j: (0, 0)),
            pl.BlockSpec((1, d, tn), lambda i, j: (i, 0, j)),
            pl.BlockSpec((1, 1, tn), lambda i, j: (i, 0, j)),
        ],
        out_specs=pl.BlockSpec((1, bp, tn), lambda i, j: (i, 0, j)),
        out_shape=jax.ShapeDtypeStruct((depth, bp, n), F32),
        compiler_params=_params(("arbitrary", "arbitrary")),
        name="ada_mods",
    )(cvec, ada_w, ada_b.reshape(depth, 1, n))


def _in_proj_kernel(x_ref, ml_ref, mc_ref, g_ref, w_ref, ws_ref, o_ref, os_ref, h_ref, *, tm, n_lat):
    t = pl.program_id(1)
    j = pl.program_id(2)

    @pl.when(j == 0)
    def _():
        is_lat = _is_lat(t, tm, n_lat)
        h = _rms(x_ref[0], g_ref[...])
        h = h * (1.0 + _mod_row(ml_ref, mc_ref, 1, is_lat)) + _mod_row(ml_ref, mc_ref, 0, is_lat)
        hb = h.astype(BF16)
        h_ref[...] = hb
        os_ref[0] = jnp.dot(hb, ws_ref[...], preferred_element_type=F32)

    o_ref[0] = jnp.dot(h_ref[...], w_ref[...], preferred_element_type=F32).astype(o_ref.dtype)


def _in_proj(xa, mods_l, mods_c, g, w_main, w_side, n_lat):
    b, n, d = xa.shape
    nm = w_main.shape[1]
    ns = w_side.shape[1]
    tm = _pick(n, (768, 384, 256, 128))
    tn = _pick(nm, (1536, 1024, 768, 640, 512, 256, 128))
    kern = functools.partial(_in_proj_kernel, tm=tm, n_lat=n_lat)
    return pl.pallas_call(
        kern,
        grid=(b, n // tm, nm // tn),
        in_specs=[
            pl.BlockSpec((1, tm, d), lambda i, t, j: (i, t, 0)),
            pl.BlockSpec((1, ADA_CHUNKS, d), lambda i, t, j: (i, 0, 0)),
            pl.BlockSpec((ADA_CHUNKS, d), lambda i, t, j: (0, 0)),
            pl.BlockSpec((1, d), lambda i, t, j: (0, 0)),
            pl.BlockSpec((d, tn), lambda i, t, j: (0, j)),
            pl.BlockSpec((d, ns), lambda i, t, j: (0, 0)),
        ],
        out_specs=[
            pl.BlockSpec((1, tm, tn), lambda i, t, j: (i, t, j)),
            pl.BlockSpec((1, tm, ns), lambda i, t, j: (i, t, 0)),
        ],
        out_shape=[
            jax.ShapeDtypeStruct((b, n, nm), BF16),
            jax.ShapeDtypeStruct((b, n, ns), F32),
        ],
        scratch_shapes=[pltpu.VMEM((tm, d), BF16)],
        compiler_params=_params(("arbitrary", "arbitrary", "arbitrary")),
        name="in_proj",
    )(xa, mods_l, mods_c, g, w_main, w_side)


def _cumsum_rows(v, reverse):
    n = v.shape[0]
    row = lax.broadcasted_iota(jnp.int32, v.shape, 0)
    k = 1
    while k < n:
        if reverse:
            v = v + jnp.where(row < n - k, pltpu.roll(v, n - k, axis=0), 0.0)
        else:
            v = v + jnp.where(row >= k, pltpu.roll(v, k, axis=0), 0.0)
        k *= 2
    return v


def _ssd_chunk_index(s, nl, nc, reverse):
    if reverse:
        return nc - 1 - s
    return lax.rem(s + nl, nc)


def _ssd_kernel(*refs, reverse, finish, nl, nc):
    if finish:
        (xm_ref, xp_ref, xn_ref, dt_ref, cw_ref, cb_ref, dtb_ref, alog_ref,
         z_ref, yf_ref, dsk_ref, ng_ref, o_ref, st_ref) = refs
    else:
        (xm_ref, xp_ref, xn_ref, dt_ref, cw_ref, cb_ref, dtb_ref, alog_ref, o_ref, st_ref) = refs
    q = SSD_CHUNK
    p = SSD_HEADDIM
    hpg = SSD_HEADS // SSD_GROUPS
    gw = hpg * p
    s = pl.program_id(1)
    chunk = _ssd_chunk_index(s, nl, nc, reverse)

    @pl.when(s == 0)
    def _():
        st_ref[...] = jnp.zeros_like(st_ref)

    xmb = xm_ref[0]
    xm = xmb.astype(F32)
    halo = xp_ref.shape[1]
    seq_first = jnp.logical_or(chunk == 0, chunk == nl)
    seq_last = jnp.logical_or(chunk == nl - 1, chunk == nc - 1)
    prev_row = xp_ref[0, halo - 1:halo, :].astype(F32) * jnp.where(seq_first, 0.0, 1.0)
    next_row = xn_ref[0, 0:1, :].astype(F32) * jnp.where(seq_last, 0.0, 1.0)
    ii = lax.broadcasted_iota(jnp.int32, (q, q), 0)
    jj = lax.broadcasted_iota(jnp.int32, (q, q), 1)
    shift_dn = jnp.where(ii == jj + 1, 1.0, 0.0).astype(BF16)
    shift_up = jnp.where(ii + 1 == jj, 1.0, 0.0).astype(BF16)
    x_prev = jnp.dot(shift_dn, xmb, preferred_element_type=F32)
    x_next = jnp.dot(shift_up, xmb, preferred_element_type=F32)
    row8 = lax.broadcasted_iota(jnp.int32, (SUBLANES, 1), 0)
    x_prev = jnp.concatenate(
        [x_prev[:SUBLANES] + jnp.where(row8 == 0, prev_row, 0.0), x_prev[SUBLANES:]], axis=0)
    x_next = jnp.concatenate(
        [x_next[:q - SUBLANES], x_next[q - SUBLANES:] + jnp.where(row8 == SUBLANES - 1, next_row, 0.0)], axis=0)
    xc = x_prev * cw_ref[0:1, :] + xm * cw_ref[1:2, :] + x_next * cw_ref[2:3, :] + cb_ref[...]
    act = _silu(xc)
    xs = act[:, :SSD_INNER]
    xsb = xs.astype(BF16)
    bm = act[:, SSD_INNER:SSD_INNER + SSD_GN]
    cm = act[:, SSD_INNER + SSD_GN:]

    dtr = dt_ref[0] + dtb_ref[...]
    dt = jnp.maximum(dtr, 0.0) + jnp.log(1.0 + jnp.exp(-jnp.abs(dtr)))
    a = -jnp.exp(alog_ref[...])
    acs = _cumsum_rows(dt * a, reverse)
    acs_t = acs.T
    dt_t = dt.T
    edge = acs[0:1, :] if reverse else acs[q - 1:q, :]
    dtd = dt * jnp.exp(edge - acs)
    cdec = jnp.exp(edge)
    mask = (ii <= jj) if reverse else (ii >= jj)
    lane_h = lax.broadcasted_iota(jnp.int32, (q, LANES), 1)
    lane_g = lax.broadcasted_iota(jnp.int32, (q, gw), 1)

    def expand(v, g):
        rows = v.shape[0]
        parts = []
        for k in range(0, hpg, LANES // p):
            h0 = g * hpg + k
            lo = jnp.broadcast_to(v[:, h0:h0 + 1], (rows, LANES))
            hi = jnp.broadcast_to(v[:, h0 + 1:h0 + 2], (rows, LANES))
            parts.append(jnp.where(lane_h[:rows] < p, lo, hi))
        return jnp.concatenate(parts, axis=1)

    ys = []
    for g in range(SSD_GROUPS):
        cg32 = cm[:, g * SSD_STATE:(g + 1) * SSD_STATE]
        bg32 = bm[:, g * SSD_STATE:(g + 1) * SSD_STATE]
        cb = lax.dot_general(cg32.astype(BF16), bg32.astype(BF16), (((1,), (1,)), ((), ())),
                             preferred_element_type=F32)
        st = st_ref[g]
        xg = xs[:, g * gw:(g + 1) * gw]
        rhs = jnp.concatenate([xsb[:, g * gw:(g + 1) * gw], st.astype(BF16)], axis=0)
        y_g = None
        for k in reversed(range(hpg)):
            h = g * hpg + k
            a_col = jnp.broadcast_to(acs[:, h:h + 1], (q, q))
            seg = a_col - jnp.broadcast_to(acs_t[h:h + 1, :], (q, q))
            lmat = jnp.exp(jnp.where(mask, seg, -jnp.inf))
            m_h = (cb * lmat * jnp.broadcast_to(dt_t[h:h + 1, :], (q, q))).astype(BF16)
            c_h = (cg32 * jnp.exp(a_col)).astype(BF16)
            res = jnp.dot(jnp.concatenate([m_h, c_h], axis=1), rhs, preferred_element_type=F32)
            y_g = res if y_g is None else jnp.where(lane_g < (k + 1) * p, res, y_g)
        ys.append(y_g)
        xdd = (xg * expand(dtd, g)).astype(BF16)
        new = jnp.dot(bg32.T.astype(BF16), xdd, preferred_element_type=F32)
        st_ref[g] = st * expand(cdec, g) + new
    y = jnp.concatenate(ys, axis=1)

    if finish:
        y = yf_ref[0] + y + xs * dsk_ref[...]
        y = y * _silu(z_ref[0].astype(F32))
        o_ref[0] = _rms(y, ng_ref[...]).astype(o_ref.dtype)
    else:
        o_ref[0] = y


def _ssd_direction(p0, s0, conv_w, conv_b, dt_bias, a_log, n_lat, reverse, extra=None):
    b, n, _ = p0.shape
    q = SSD_CHUNK
    nc = n // q
    nl = n_lat // q
    halo = 16
    hpc = q // halo
    nh = n // halo
    d = 1 if reverse else 0
    cidx = functools.partial(_ssd_chunk_index, nl=nl, nc=nc, reverse=reverse)
    finish = extra is not None
    in_specs = [
        pl.BlockSpec((1, q, SSD_CONV_DIM), lambda i, s: (i, cidx(s), 0)),
        pl.BlockSpec((1, halo, SSD_CONV_DIM), lambda i, s: (i, jnp.maximum(cidx(s) * hpc - 1, 0), 0)),
        pl.BlockSpec((1, halo, SSD_CONV_DIM), lambda i, s: (i, jnp.minimum(cidx(s) * hpc + hpc, nh - 1), 0)),
        pl.BlockSpec((1, q, LANES), lambda i, s: (i, cidx(s), d)),
        pl.BlockSpec((3, SSD_CONV_DIM), lambda i, s: (0, 0)),
        pl.BlockSpec((1, SSD_CONV_DIM), lambda i, s: (0, 0)),
        pl.BlockSpec((1, LANES), lambda i, s: (0, 0)),
        pl.BlockSpec((1, LANES), lambda i, s: (0, 0)),
    ]
    args = [p0, p0, p0, s0, conv_w, conv_b, dt_bias, a_log]
    if finish:
        yf, d_skip, norm_g = extra
        in_specs += [
            pl.BlockSpec((1, q, SSD_INNER), lambda i, s: (i, cidx(s), SSD_CONV_DIM // SSD_INNER)),
            pl.BlockSpec((1, q, SSD_INNER), lambda i, s: (i, cidx(s), 0)),
            pl.BlockSpec((1, SSD_INNER), lambda i, s: (0, 0)),
            pl.BlockSpec((1, SSD_INNER), lambda i, s: (0, 0)),
        ]
        args += [p0, yf, d_skip, norm_g]
    kern = functools.partial(_ssd_kernel, reverse=reverse, finish=finish, nl=nl, nc=nc)
    return pl.pallas_call(
        kern,
        grid=(b, nc),
        in_specs=in_specs,
        out_specs=pl.BlockSpec((1, q, SSD_INNER), lambda i, s: (i, cidx(s), 0)),
        out_shape=jax.ShapeDtypeStruct((b, n, SSD_INNER), BF16 if finish else F32),
        scratch_shapes=[pltpu.VMEM((SSD_GROUPS, SSD_STATE, (SSD_HEADS // SSD_GROUPS) * SSD_HEADDIM), F32)],
        compiler_params=_params(("arbitrary", "arbitrary")),
        name="ssd_bwd_finish" if finish else "ssd_fwd",
    )(*args)


def _rope_tables(rows, rot_dim, n_ctx):
    n_freq = rot_dim // 4
    row = jnp.repeat(jnp.arange(rows, dtype=F32), GRID_W)
    col = jnp.tile(jnp.arange(GRID_W, dtype=F32), rows)
    inv = ROPE_THETA ** (-jnp.arange(n_freq, dtype=F32) / n_freq)
    ang = jnp.concatenate([row[:, None] * inv, col[:, None] * inv], axis=-1)
    cos, sin = jnp.cos(ang), jnp.sin(ang)
    reps = LANES // rot_dim
    cos_t = jnp.tile(jnp.concatenate([cos, cos], axis=-1), (1, reps))
    sin_t = jnp.tile(jnp.concatenate([-sin, sin], axis=-1), (1, reps))
    cos_t = jnp.concatenate([cos_t, jnp.ones((n_ctx, LANES), F32)], axis=0)
    sin_t = jnp.concatenate([sin_t, jnp.zeros((n_ctx, LANES), F32)], axis=0)
    return cos_t, sin_t


def _rope_block(xb, cos, sin, rot_dim):
    half = rot_dim // 2
    lane = lax.broadcasted_iota(jnp.int32, xb.shape, 1)
    first = jnp.bitwise_and(lane, rot_dim - 1) < half
    partner = jnp.where(first, pltpu.roll(xb, LANES - half, axis=1), pltpu.roll(xb, half, axis=1))
    return xb * cos + partner * sin


def _segment_mean_matrix(seg):
    sh = seg.bit_length() - 1
    i = jnp.right_shift(lax.broadcasted_iota(jnp.int32, (LANES, LANES), 0), sh)
    j = jnp.right_shift(lax.broadcasted_iota(jnp.int32, (LANES, LANES), 1), sh)
    return jnp.where(i == j, 1.0 / seg, 0.0).astype(BF16)


def _segment_mean_sq(xb, smat):
    x2 = xb * xb
    hi = x2.astype(BF16)
    lo = (x2 - hi.astype(F32)).astype(BF16)
    return (jnp.dot(hi, smat, preferred_element_type=F32) + jnp.dot(lo, smat, preferred_element_type=F32))


def _qk_prep_kernel(q_ref, k_ref, v_ref, cos_ref, sin_ref, qg_ref, kg_ref, qo_ref, ko_ref, vo_ref, *, scale):
    smat = _segment_mean_matrix(ATT_HEADDIM)
    cos = cos_ref[...]
    sin = sin_ref[...]
    hpb = LANES // ATT_HEADDIM

    def prep(src_ref, g_ref, dst_ref, n_heads, mul, transposed):
        for c in range(n_heads // hpb):
            xb = src_ref[0, :, c * LANES:(c + 1) * LANES].astype(F32)
            xn = xb * lax.rsqrt(_segment_mean_sq(xb, smat) + EPS) * g_ref[...]
            r = _rope_block(xn, cos, sin, ATT_HEADDIM)
            if mul != 1.0:
                r = r * mul
            if transposed:
                r = r.T
            r = r.astype(dst_ref.dtype)
            for k in range(hpb):
                if transposed:
                    dst_ref[0, c * hpb + k] = r[k * ATT_HEADDIM:(k + 1) * ATT_HEADDIM, :]
                else:
                    dst_ref[0, c * hpb + k] = r[:, k * ATT_HEADDIM:(k + 1) * ATT_HEADDIM]

    prep(q_ref, qg_ref, qo_ref, ATT_HEADS, scale, False)
    prep(k_ref, kg_ref, ko_ref, ATT_KV_HEADS, 1.0, True)
    for h in range(ATT_KV_HEADS):
        vo_ref[0, h] = v_ref[0, :, h * ATT_HEADDIM:(h + 1) * ATT_HEADDIM]


def _qk_prep(p0, cos_t, sin_t, q_g, k_g, q_col, k_col, v_col):
    b, n, _ = p0.shape
    tm = _pick(n, (768, 384, 256, 128))
    kern = functools.partial(_qk_prep_kernel, scale=ATT_HEADDIM ** -0.5 * LOG2E)
    return pl.pallas_call(
        kern,
        grid=(b, n // tm),
        in_specs=[
            pl.BlockSpec((1, tm, ATT_Q), lambda i, t: (i, t, q_col // ATT_Q)),
            pl.BlockSpec((1, tm, ATT_KV), lambda i, t: (i, t, k_col // ATT_KV)),
            pl.BlockSpec((1, tm, ATT_KV), lambda i, t: (i, t, v_col // ATT_KV)),
            pl.BlockSpec((tm, LANES), lambda i, t: (t, 0)),
            pl.BlockSpec((tm, LANES), lambda i, t: (t, 0)),
            pl.BlockSpec((1, LANES), lambda i, t: (0, 0)),
            pl.BlockSpec((1, LANES), lambda i, t: (0, 0)),
        ],
        out_specs=[
            pl.BlockSpec((1, ATT_HEADS, tm, ATT_HEADDIM), lambda i, t: (i, 0, t, 0)),
            pl.BlockSpec((1, ATT_KV_HEADS, ATT_HEADDIM, tm), lambda i, t: (i, 0, 0, t)),
            pl.BlockSpec((1, ATT_KV_HEADS, tm, ATT_HEADDIM), lambda i, t: (i, 0, t, 0)),
        ],
        out_shape=[
            jax.ShapeDtypeStruct((b, ATT_HEADS, n, ATT_HEADDIM), BF16),
            jax.ShapeDtypeStruct((b, ATT_KV_HEADS, ATT_HEADDIM, n), BF16),
            jax.ShapeDtypeStruct((b, ATT_KV_HEADS, n, ATT_HEADDIM), BF16),
        ],
        compiler_params=_params(("arbitrary", "arbitrary")),
        name="gqa_qk_prep",
    )(p0, p0, p0, cos_t, sin_t, q_g, k_g)


def _attn_kernel(q_ref, k_ref, v_ref, o_ref, s_scr, *, group, n_lat, n_all, lat_tiles, ctx_self):
    t = pl.program_id(2)
    tq = q_ref.shape[2]
    dv = v_ref.shape[3]
    rb = s_scr.shape[1]
    blocks = [(u, r) for u in range(HEAD_BLOCK) for r in range(tq // rb)]

    def run(k_lo, k_len):
        def scores(i):
            u, r = blocks[i]
            q = q_ref[0, u, r * rb:(r + 1) * rb, :]
            kt = k_ref[0, u // group, :, k_lo:k_lo + k_len]
            s_scr[i % 2, :, :k_len] = jnp.dot(q, kt, preferred_element_type=F32)

        def combine(i):
            u, r = blocks[i]
            s = s_scr[i % 2, :, :k_len]
            m = jnp.max(s, axis=-1, keepdims=True)
            e = jnp.exp2(s - m)
            l = jnp.sum(e, axis=-1, keepdims=True)
            vv = v_ref[0, u // group, k_lo:k_lo + k_len, :]
            o = jnp.dot(e.astype(BF16), vv, preferred_element_type=F32) * (1.0 / l)
            o_ref[0, r * rb:(r + 1) * rb, u * dv:(u + 1) * dv] = o.astype(o_ref.dtype)

        scores(0)
        for i in range(len(blocks)):
            if i + 1 < len(blocks):
                scores(i + 1)
            combine(i)

    if ctx_self:
        @pl.when(t < lat_tiles)
        def _():
            run(0, n_all)

        @pl.when(t >= lat_tiles)
        def _():
            run(n_lat, n_all - n_lat)
    else:
        run(0, n_all)


def _attention(q, k, v, n_lat, ctx_self):
    b, h, nq, dqk = q.shape
    hkv, n_all, dv = v.shape[1], v.shape[2], v.shape[3]
    group = h // hkv
    kvb = max(HEAD_BLOCK // group, 1)
    tq = _pick(math.gcd(n_lat, n_all - n_lat) if ctx_self else n_lat, (256, 128))
    assert nq == (n_all if ctx_self else n_lat)
    kern = functools.partial(_attn_kernel, group=group, n_lat=n_lat, n_all=n_all,
                             lat_tiles=n_lat // tq, ctx_self=ctx_self)
    return pl.pallas_call(
        kern,
        grid=(b, h // HEAD_BLOCK, nq // tq),
        in_specs=[
            pl.BlockSpec((1, HEAD_BLOCK, tq, dqk), lambda i, hb, t: (i, hb, t, 0)),
            pl.BlockSpec((1, kvb, dqk, n_all), lambda i, hb, t: (i, hb, 0, 0)),
            pl.BlockSpec((1, kvb, n_all, dv), lambda i, hb, t: (i, hb, 0, 0)),
        ],
        out_specs=pl.BlockSpec((1, tq, HEAD_BLOCK * dv), lambda i, hb, t: (i, t, hb)),
        out_shape=jax.ShapeDtypeStruct((b, nq, h * dv), BF16),
        scratch_shapes=[pltpu.VMEM((2, min(tq, ATTN_ROW_BLOCK), n_all), F32)],
        compiler_params=_params(("arbitrary", "arbitrary", "arbitrary")),
        name="attention",
    )(q, k, v)


def _out_proj_kernel(*refs, n_in, tm, n_lat):
    x_ref, ml_ref, mc_ref = refs[:3]
    a_refs = refs[3:3 + n_in]
    w_refs = refs[3 + n_in:3 + 2 * n_in]
    o_ref = refs[3 + 2 * n_in]
    t = pl.program_id(1)
    acc = jnp.dot(a_refs[0][0], w_refs[0][...], preferred_element_type=F32)
    for a_ref, w_ref in zip(a_refs[1:], w_refs[1:]):
        acc = acc + jnp.dot(a_ref[0], w_ref[...], preferred_element_type=F32)
    gate = _mod_row(ml_ref, mc_ref, 2, _is_lat(t, tm, n_lat))
    o_ref[0] = x_ref[0] + gate * acc


def _out_proj(xa, mods_l, mods_c, acts, weights, n_rows, n_lat):
    b, _, d = xa.shape
    tm = _pick(n_rows, (1024, 768, 512, 384, 256, 128))
    n_in = len(acts)
    kern = functools.partial(_out_proj_kernel, n_in=n_in, tm=tm, n_lat=n_lat)
    in_specs = [
        pl.BlockSpec((1, tm, d), lambda i, t: (i, t, 0)),
        pl.BlockSpec((1, ADA_CHUNKS, d), lambda i, t: (i, 0, 0)),
        pl.BlockSpec((ADA_CHUNKS, d), lambda i, t: (0, 0)),
    ]
    in_specs += [pl.BlockSpec((1, tm, a.shape[2]), lambda i, t: (i, t, 0)) for a in acts]
    in_specs += [pl.BlockSpec(w.shape, lambda i, t: (0, 0)) for w in weights]
    return pl.pallas_call(
        kern,
        grid=(b, n_rows // tm),
        in_specs=in_specs,
        out_specs=pl.BlockSpec((1, tm, d), lambda i, t: (i, t, 0)),
        out_shape=jax.ShapeDtypeStruct((b, n_rows, d), F32),
        compiler_params=_params(("arbitrary", "arbitrary")),
        name="out_proj",
    )(xa, mods_l, mods_c, *acts, *weights)


def _hidden_chunks(hidden):
    chunks, lo = [], 0
    while lo < hidden:
        w = min(1024, hidden - lo)
        chunks.append((lo, w))
        lo += w
    return chunks


def _ffn_kernel(x_ref, ml_ref, mc_ref, g_ref, wg_ref, wu_ref, wd_ref, fg_ref, o_ref, *, tm, n_lat, final):
    t = pl.program_id(1)
    is_lat = _is_lat(t, tm, n_lat)
    x = x_ref[0]
    h = _rms(x, g_ref[...])
    h = (h * (1.0 + _mod_row(ml_ref, mc_ref, 4, is_lat)) + _mod_row(ml_ref, mc_ref, 3, is_lat)).astype(BF16)
    acc = None
    for lo, w in _hidden_chunks(wd_ref.shape[0]):
        gate = jnp.dot(h, wg_ref[:, lo:lo + w], preferred_element_type=F32)
        up = jnp.dot(h, wu_ref[:, lo:lo + w], preferred_element_type=F32)
        a = (_silu(gate) * up).astype(BF16)
        part = jnp.dot(a, wd_ref[lo:lo + w, :], preferred_element_type=F32)
        acc = part if acc is None else acc + part
    y = x + _mod_row(ml_ref, mc_ref, 5, is_lat) * acc
    if final:
        y = _rms(y, fg_ref[...])
    o_ref[0] = y


def _ffn(xa, mods_l, mods_c, g, w_gate, w_up, w_down, final_g, n_lat, final):
    b, n, d = xa.shape
    hid = w_down.shape[0]
    tm = _pick(n, (768, 512, 384, 256, 128))
    kern = functools.partial(_ffn_kernel, tm=tm, n_lat=n_lat, final=final)
    resident = dict(pipeline_mode=pl.Buffered(1))
    return pl.pallas_call(
        kern,
        grid=(b, n // tm),
        in_specs=[
            pl.BlockSpec((1, tm, d), lambda i, t: (i, t, 0)),
            pl.BlockSpec((1, ADA_CHUNKS, d), lambda i, t: (i, 0, 0)),
            pl.BlockSpec((ADA_CHUNKS, d), lambda i, t: (0, 0)),
            pl.BlockSpec((1, d), lambda i, t: (0, 0)),
            pl.BlockSpec((d, hid), lambda i, t: (0, 0), **resident),
            pl.BlockSpec((d, hid), lambda i, t: (0, 0), **resident),
            pl.BlockSpec((hid, d), lambda i, t: (0, 0), **resident),
            pl.BlockSpec((1, d), lambda i, t: (0, 0)),
        ],
        out_specs=pl.BlockSpec((1, tm, d), lambda i, t: (i, t, 0)),
        out_shape=jax.ShapeDtypeStruct((b, n, d), F32),
        compiler_params=_params(("arbitrary", "arbitrary")),
        name="ffn_final" if final else "ffn",
    )(xa, mods_l, mods_c, g, w_gate, w_up, w_down, final_g)


def _mla_q_kernel(p_ref, g_ref, w_ref, cos_ref, sin_ref, o_ref, *, q_rank, scale):
    cq = p_ref[0, :, :q_rank].astype(F32)
    qn = _rms(cq, g_ref[...]).astype(BF16)
    qq = jnp.dot(qn, w_ref[...], preferred_element_type=F32)
    n_nope = MLA_HEADS * MLA_NOPE
    hpb = LANES // MLA_ROPE
    pes = []
    for c in range(MLA_HEADS // hpb):
        pes.append(_rope_block(qq[:, n_nope + c * LANES:n_nope + (c + 1) * LANES],
                               cos_ref[...], sin_ref[...], MLA_ROPE))
    for h in range(MLA_HEADS):
        pe = pes[h // hpb][:, (h % hpb) * MLA_ROPE:(h % hpb + 1) * MLA_ROPE]
        qh = jnp.concatenate([qq[:, h * MLA_NOPE:(h + 1) * MLA_NOPE], pe], axis=1) * scale
        o_ref[0, h] = qh.astype(o_ref.dtype)


def _mla_q(p1, g, w_uq, cos_t, sin_t, n_lat, q_rank):
    b = p1.shape[0]
    cols = p1.shape[2]
    tm = _pick(n_lat, (512, 256, 128))
    kern = functools.partial(_mla_q_kernel, q_rank=q_rank, scale=MLA_QK ** -0.5 * LOG2E)
    return pl.pallas_call(
        kern,
        grid=(b, n_lat // tm),
        in_specs=[
            pl.BlockSpec((1, tm, cols), lambda i, t: (i, t, 0)),
            pl.BlockSpec((1, q_rank), lambda i, t: (0, 0)),
            pl.BlockSpec(w_uq.shape, lambda i, t: (0, 0)),
            pl.BlockSpec((tm, LANES), lambda i, t: (t, 0)),
            pl.BlockSpec((tm, LANES), lambda i, t: (t, 0)),
        ],
        out_specs=pl.BlockSpec((1, MLA_HEADS, tm, MLA_QK), lambda i, t: (i, 0, t, 0)),
        out_shape=jax.ShapeDtypeStruct((b, MLA_HEADS, n_lat, MLA_QK), BF16),
        compiler_params=_params(("arbitrary", "arbitrary")),
        name="mla_q",
    )(p1, g, w_uq, cos_t, sin_t)


def _mla_kv_kernel(p_ref, pe_ref, g_ref, w_ref, cos_ref, sin_ref, ko_ref, vo_ref, *, q_rank):
    ckv = p_ref[0, :, q_rank:].astype(F32)
    kvn = _rms(ckv, g_ref[...]).astype(BF16)
    kv = jnp.dot(kvn, w_ref[...], preferred_element_type=F32)
    pe_t = _rope_block(pe_ref[0], cos_ref[...], sin_ref[...], MLA_ROPE).T[:MLA_ROPE, :]
    n_nope = MLA_HEADS * MLA_NOPE
    hpb = LANES // MLA_NOPE
    for c in range(MLA_HEADS // hpb):
        nope_t = kv[:, c * LANES:(c + 1) * LANES].T
        for k in range(hpb):
            kh = jnp.concatenate([nope_t[k * MLA_NOPE:(k + 1) * MLA_NOPE, :], pe_t], axis=0)
            ko_ref[0, c * hpb + k] = kh.astype(ko_ref.dtype)
    for h in range(MLA_HEADS):
        vo_ref[0, h] = kv[:, n_nope + h * MLA_V:n_nope + (h + 1) * MLA_V].astype(vo_ref.dtype)


def _mla_kv(p1, s1, g, w_ukv, cos_t, sin_t, q_rank):
    b, n, cols = p1.shape
    kv_rank = cols - q_rank
    tm = _pick(n, (768, 384, 256, 128))
    kern = functools.partial(_mla_kv_kernel, q_rank=q_rank)
    return pl.pallas_call(
        kern,
        grid=(b, n // tm),
        in_specs=[
            pl.BlockSpec((1, tm, cols), lambda i, t: (i, t, 0)),
            pl.BlockSpec((1, tm, LANES), lambda i, t: (i, t, 0)),
            pl.BlockSpec((1, kv_rank), lambda i, t: (0, 0)),
            pl.BlockSpec(w_ukv.shape, lambda i, t: (0, 0)),
            pl.BlockSpec((tm, LANES), lambda i, t: (t, 0)),
            pl.BlockSpec((tm, LANES), lambda i, t: (t, 0)),
        ],
        out_specs=[
            pl.BlockSpec((1, MLA_HEADS, MLA_QK, tm), lambda i, t: (i, 0, 0, t)),
            pl.BlockSpec((1, MLA_HEADS, tm, MLA_V), lambda i, t: (i, 0, t, 0)),
        ],
        out_shape=[
            jax.ShapeDtypeStruct((b, MLA_HEADS, MLA_QK, n), BF16),
            jax.ShapeDtypeStruct((b, MLA_HEADS, n, MLA_V), BF16),
        ],
        compiler_params=_params(("arbitrary", "arbitrary")),
        name="mla_kv",
    )(p1, s1, g, w_ukv, cos_t, sin_t)


def _pad_lanes(v, width=LANES):
    v = v.reshape(1, -1).astype(F32)
    return jnp.pad(v, ((0, 0), (0, width - v.shape[1])))


def kernel(x, c, ctx, c_ctx, ada_w, ada_b, norm1_g, norm2_g, ffn_w_up, ffn_w_down, ab_w_in, ab_w_out, ssd_conv_w, ssd_conv_b, ssd_a_log, ssd_dt_bias, ssd_d, ssd_norm_g, att_q_g, att_k_g, mla_w_in, mla_q_norm_g, mla_w_uq, mla_kv_norm_g, mla_w_ukv, mla_w_o, final_norm_g):
    b, t_lat, d = x.shape
    n_ctx = ctx.shape[1]
    rows = t_lat // GRID_W
    hid = ffn_w_down.shape[1]
    row2 = lambda v: v.reshape(1, -1).astype(F32)

    bp = -(-(b + 1) // 8) * 8
    cvec = jnp.concatenate([c, c_ctx[None, :], jnp.zeros((bp - b - 1, d), F32)], axis=0)
    mods = _ada_mods(cvec, ada_w, ada_b)
    mods_l = [mods[i, :b].reshape(b, ADA_CHUNKS, d) for i in range(2)]
    mods_c = [mods[i, b].reshape(ADA_CHUNKS, d) for i in range(2)]

    xa = jnp.concatenate([x, ctx], axis=1)

    w_in = ab_w_in[0]
    o_xbc, o_dt = SSD_INNER, SSD_INNER + SSD_CONV_DIM
    o_att = o_dt + 2 * SSD_HEADS
    w_main = jnp.concatenate([w_in[:, o_xbc:o_dt], w_in[:, :SSD_INNER], w_in[:, o_att:]], axis=1).astype(BF16)
    w_dt = jnp.zeros((d, 2 * LANES), F32)
    w_dt = w_dt.at[:, :SSD_HEADS].set(w_in[:, o_dt:o_dt + SSD_HEADS])
    w_dt = w_dt.at[:, LANES:LANES + SSD_HEADS].set(w_in[:, o_dt + SSD_HEADS:o_att]).astype(BF16)
    p0, s0 = _in_proj(xa, mods_l[0], mods_c[0], row2(norm1_g[0]), w_main, w_dt, t_lat)

    conv_w = ssd_conv_w[0].astype(F32)
    conv_b = row2(ssd_conv_b[0])
    yf = _ssd_direction(p0, s0, conv_w, conv_b, _pad_lanes(ssd_dt_bias[0, 0]), _pad_lanes(ssd_a_log[0, 0]),
                        t_lat, reverse=False)
    d_skip = jnp.repeat(ssd_d[0].astype(F32), SSD_HEADDIM).reshape(1, SSD_INNER)
    y_ssd = _ssd_direction(p0, s0, conv_w, conv_b, _pad_lanes(ssd_dt_bias[0, 1]), _pad_lanes(ssd_a_log[0, 1]),
                           t_lat, reverse=True, extra=(yf, d_skip, row2(ssd_norm_g[0])))

    q_col = SSD_CONV_DIM + SSD_INNER
    k_col = q_col + ATT_Q
    cos_a, sin_a = _rope_tables(rows, ATT_HEADDIM, n_ctx)
    hpb = LANES // ATT_HEADDIM
    qh, kh, vh = _qk_prep(p0, cos_a, sin_a, jnp.tile(row2(att_q_g[0]), (1, hpb)),
                          jnp.tile(row2(att_k_g[0]), (1, hpb)), q_col, k_col, k_col + ATT_KV)
    o_att_l0 = _attention(qh, kh, vh, t_lat, ctx_self=True)

    w_out = ab_w_out[0].astype(BF16)
    xa = _out_proj(xa, mods_l[0], mods_c[0], [y_ssd, o_att_l0], [w_out[:SSD_INNER], w_out[SSD_INNER:]],
                   t_lat + n_ctx, t_lat)
    w_up = ffn_w_up[0].astype(BF16)
    xa = _ffn(xa, mods_l[0], mods_c[0], row2(norm2_g[0]), w_up[:, :hid], w_up[:, hid:],
              ffn_w_down[0].astype(BF16), row2(final_norm_g), t_lat, final=False)

    q_rank = mla_q_norm_g.shape[1]
    kv_rank = mla_kv_norm_g.shape[1]
    w_in1 = mla_w_in[0]
    w_pe = jnp.pad(w_in1[:, q_rank + kv_rank:], ((0, 0), (0, LANES - MLA_ROPE))).astype(BF16)
    p1, s1 = _in_proj(xa, mods_l[1], mods_c[1], row2(norm1_g[1]), w_in1[:, :q_rank + kv_rank].astype(BF16), w_pe,
                      t_lat)
    cos_m, sin_m = _rope_tables(rows, MLA_ROPE, n_ctx)
    w_uq = mla_w_uq[0].reshape(q_rank, MLA_HEADS, MLA_QK)
    w_uq = jnp.concatenate([w_uq[:, :, :MLA_NOPE].reshape(q_rank, -1), w_uq[:, :, MLA_NOPE:].reshape(q_rank, -1)],
                           axis=1).astype(BF16)
    w_ukv = mla_w_ukv[0].reshape(kv_rank, MLA_HEADS, MLA_NOPE + MLA_V)
    w_ukv = jnp.concatenate([w_ukv[:, :, :MLA_NOPE].reshape(kv_rank, -1), w_ukv[:, :, MLA_NOPE:].reshape(kv_rank, -1)],
                            axis=1).astype(BF16)
    qm = _mla_q(p1, row2(mla_q_norm_g[0]), w_uq, cos_m, sin_m, t_lat, q_rank)
    km, vm = _mla_kv(p1, s1, row2(mla_kv_norm_g[0]), w_ukv, cos_m, sin_m, q_rank)
    o_mla = _attention(qm, km, vm, t_lat, ctx_self=False)

    xl = _out_proj(xa, mods_l[1], mods_c[1], [o_mla], [mla_w_o[0].astype(BF16)], t_lat, t_lat)
    w_up = ffn_w_up[1].astype(BF16)
    return _ffn(xl, mods_l[1], mods_c[1], row2(norm2_g[1]), w_up[:, :hid], w_up[:, hid:],
                ffn_w_down[1].astype(BF16), row2(final_norm_g), t_lat, final=True)
```

```python
import functools
import math

import jax
import jax.numpy as jnp
from jax import lax
from jax.experimental import pallas as pl
from jax.experimental.pallas import tpu as pltpu

F32 = jnp.float32
BF16 = jnp.bfloat16

EPS = 1e-6
ROPE_THETA = 10000.0
GRID_W = 64
ADA_CHUNKS = 6

SSD_HEADS = 16
SSD_HEADDIM = 64
SSD_GROUPS = 4
SSD_STATE = 128
SSD_CHUNK = 128
SSD_INNER = SSD_HEADS * SSD_HEADDIM
SSD_GN = SSD_GROUPS * SSD_STATE
SSD_CONV_DIM = SSD_INNER + 2 * SSD_GN

ATT_HEADS = 16
ATT_KV_HEADS = 4
ATT_HEADDIM = 64
ATT_Q = ATT_HEADS * ATT_HEADDIM
ATT_KV = ATT_KV_HEADS * ATT_HEADDIM

MLA_HEADS = 16
MLA_NOPE = 64
MLA_ROPE = 32
MLA_V = 64
MLA_QK = MLA_NOPE + MLA_ROPE

LANES = 128
SUBLANES = 8
BF16_ROWS = 16
HEAD_BLOCK = 8
ATTN_BLOCK_LANES = 512
LOG2E = math.log2(math.e)
VMEM_LIMIT = 56 * 1024 * 1024


def _pick(n, candidates):
    for c in candidates:
        if n % c == 0:
            return c
    raise ValueError(f"no tile for {n} in {candidates}")


def _params(sem):
    return pltpu.CompilerParams(dimension_semantics=sem, vmem_limit_bytes=VMEM_LIMIT)


def _silu(x):
    hx = 0.5 * x
    return hx + hx * jnp.tanh(hx)


def _rms(x, g):
    ms = jnp.mean(x * x, axis=-1, keepdims=True)
    return x * lax.rsqrt(ms + EPS) * g


def _mod_row(ml_ref, mc_ref, idx, is_lat):
    return jnp.where(is_lat, ml_ref[0, idx:idx + 1, :], mc_ref[idx:idx + 1, :])


def _is_lat(t, tm, n_lat):
    rows = t * tm + lax.broadcasted_iota(jnp.int32, (tm, 1), 0)
    return rows < n_lat


def _ada_kernel(c_ref, w_ref, b_ref, o_ref):
    s = _silu(c_ref[...]).astype(BF16)
    o_ref[0] = jnp.dot(s, w_ref[0].astype(BF16), preferred_element_type=F32) + b_ref[0]


def _ada_mods(cvec, ada_w, ada_b):
    depth, d, n = ada_w.shape
    bp = cvec.shape[0]
    tn = _pick(n, (1536, 1024, 512, 256, 128))
    return pl.pallas_call(
        _ada_kernel,
        grid=(depth, n // tn),
        in_specs=[
            pl.BlockSpec((bp, d), lambda i, j: (0, 0)),
            pl.BlockSpec((1, d, tn), lambda i, j: (i, 0, j)),
            pl.BlockSpec((1, 1, tn), lambda i, j: (i, 0, j)),
        ],
        out_specs=pl.BlockSpec((1, bp, tn), lambda i, j: (i, 0, j)),
        out_shape=jax.ShapeDtypeStruct((depth, bp, n), F32),
        compiler_params=_params(("arbitrary", "arbitrary")),
        name="ada_mods",
    )(cvec, ada_w, ada_b.reshape(depth, 1, n))


def _in_proj_kernel(x_ref, ml_ref, mc_ref, g_ref, w_ref, ws_ref, o_ref, os_ref, h_ref, *, tm, n_lat):
    t = pl.program_id(1)
    j = pl.program_id(2)

    @pl.when(j == 0)
    def _():
        is_lat = _is_lat(t, tm, n_lat)
        h = _rms(x_ref[0], g_ref[...])
        h = h * (1.0 + _mod_row(ml_ref, mc_ref, 1, is_lat)) + _mod_row(ml_ref, mc_ref, 0, is_lat)
        hb = h.astype(BF16)
        h_ref[...] = hb
        os_ref[0] = jnp.dot(hb, ws_ref[...], preferred_element_type=F32)

    o_ref[0] = jnp.dot(h_ref[...], w_ref[...], preferred_element_type=F32).astype(o_ref.dtype)


def _in_proj(xa, mods_l, mods_c, g, w_main, w_side, n_lat):
    b, n, d = xa.shape
    nm = w_main.shape[1]
    ns = w_side.shape[1]
    tm = _pick(n, (768, 384, 256, 128))
    tn = _pick(nm, (1536, 1024, 768, 640, 512, 256, 128))
    kern = functools.partial(_in_proj_kernel, tm=tm, n_lat=n_lat)
    return pl.pallas_call(
        kern,
        grid=(b, n // tm, nm // tn),
        in_specs=[
            pl.BlockSpec((1, tm, d), lambda i, t, j: (i, t, 0)),
            pl.BlockSpec((1, ADA_CHUNKS, d), lambda i, t, j: (i, 0, 0)),
            pl.BlockSpec((ADA_CHUNKS, d), lambda i, t, j: (0, 0)),
            pl.BlockSpec((1, d), lambda i, t, j: (0, 0)),
            pl.BlockSpec((d, tn), lambda i, t, j: (0, j)),
            pl.BlockSpec((d, ns), lambda i, t, j: (0, 0)),
        ],
        out_specs=[
            pl.BlockSpec((1, tm, tn), lambda i, t, j: (i, t, j)),
            pl.BlockSpec((1, tm, ns), lambda i, t, j: (i, t, 0)),
        ],
        out_shape=[
            jax.ShapeDtypeStruct((b, n, nm), BF16),
            jax.ShapeDtypeStruct((b, n, ns), F32),
        ],
        scratch_shapes=[pltpu.VMEM((tm, d), BF16)],
        compiler_params=_params(("arbitrary", "arbitrary", "arbitrary")),
        name="in_proj",
    )(xa, mods_l, mods_c, g, w_main, w_side)


def _cumsum_rows(v, reverse):
    n = v.shape[0]
    row = lax.broadcasted_iota(jnp.int32, v.shape, 0)
    k = 1
    while k < n:
        if reverse:
            v = v + jnp.where(row < n - k, pltpu.roll(v, n - k, axis=0), 0.0)
        else:
            v = v + jnp.where(row >= k, pltpu.roll(v, k, axis=0), 0.0)
        k *= 2
    return v


def _ssd_chunk_index(s, nl, nc, reverse):
    if reverse:
        return nc - 1 - s
    return lax.rem(s + nl, nc)


def _ssd_kernel(*refs, reverse, finish, nl, nc):
    if finish:
        (xm_ref, xp_ref, xn_ref, dt_ref, cw_ref, cb_ref, dtb_ref, alog_ref,
         z_ref, yf_ref, dsk_ref, ng_ref, o_ref, st_ref) = refs
    else:
        (xm_ref, xp_ref, xn_ref, dt_ref, cw_ref, cb_ref, dtb_ref, alog_ref, o_ref, st_ref) = refs
    q = SSD_CHUNK
    p = SSD_HEADDIM
    hpg = SSD_HEADS // SSD_GROUPS
    gw = hpg * p
    s = pl.program_id(1)
    chunk = _ssd_chunk_index(s, nl, nc, reverse)

    @pl.when(s == 0)
    def _():
        st_ref[...] = jnp.zeros_like(st_ref)

    xmb = xm_ref[0]
    xm = xmb.astype(F32)
    halo = xp_ref.shape[1]
    seq_first = jnp.logical_or(chunk == 0, chunk == nl)
    seq_last = jnp.logical_or(chunk == nl - 1, chunk == nc - 1)
    prev_row = xp_ref[0, halo - 1:halo, :].astype(F32) * jnp.where(seq_first, 0.0, 1.0)
    next_row = xn_ref[0, 0:1, :].astype(F32) * jnp.where(seq_last, 0.0, 1.0)
    ii = lax.broadcasted_iota(jnp.int32, (q, q), 0)
    jj = lax.broadcasted_iota(jnp.int32, (q, q), 1)
    shift_dn = jnp.where(ii == jj + 1, 1.0, 0.0).astype(BF16)
    shift_up = jnp.where(ii + 1 == jj, 1.0, 0.0).astype(BF16)
    x_prev = jnp.dot(shift_dn, xmb, preferred_element_type=F32)
    x_next = jnp.dot(shift_up, xmb, preferred_element_type=F32)
    row8 = lax.broadcasted_iota(jnp.int32, (SUBLANES, 1), 0)
    x_prev = jnp.concatenate(
        [x_prev[:SUBLANES] + jnp.where(row8 == 0, prev_row, 0.0), x_prev[SUBLANES:]], axis=0)
    x_next = jnp.concatenate(
        [x_next[:q - SUBLANES], x_next[q - SUBLANES:] + jnp.where(row8 == SUBLANES - 1, next_row, 0.0)], axis=0)
    xc = x_prev * cw_ref[0:1, :] + xm * cw_ref[1:2, :] + x_next * cw_ref[2:3, :] + cb_ref[...]
    act = _silu(xc)
    xs = act[:, :SSD_INNER]
    xsb = xs.astype(BF16)
    bm = act[:, SSD_INNER:SSD_INNER + SSD_GN]
    cm = act[:, SSD_INNER + SSD_GN:]

    dtr = dt_ref[0] + dtb_ref[...]
    dt = jnp.maximum(dtr, 0.0) + jnp.log(1.0 + jnp.exp(-jnp.abs(dtr)))
    a = -jnp.exp(alog_ref[...])
    acs = _cumsum_rows(dt * a, reverse)
    acs_t = acs.T
    dt_t = dt.T
    edge = acs[0:1, :] if reverse else acs[q - 1:q, :]
    dtd = dt * jnp.exp(edge - acs)
    cdec = jnp.exp(edge)
    mask = (ii <= jj) if reverse else (ii >= jj)
    lane_h = lax.broadcasted_iota(jnp.int32, (q, LANES), 1)
    lane_g = lax.broadcasted_iota(jnp.int32, (q, gw), 1)

    def expand(v, g):
        rows = v.shape[0]
        parts = []
        for k in range(0, hpg, LANES // p):
            h0 = g * hpg + k
            lo = jnp.broadcast_to(v[:, h0:h0 + 1], (rows, LANES))
            hi = jnp.broadcast_to(v[:, h0 + 1:h0 + 2], (rows, LANES))
            parts.append(jnp.where(lane_h[:rows] < p, lo, hi))
        return jnp.concatenate(parts, axis=1)

    ys = []
    for g in range(SSD_GROUPS):
        cg32 = cm[:, g * SSD_STATE:(g + 1) * SSD_STATE]
        bg32 = bm[:, g * SSD_STATE:(g + 1) * SSD_STATE]
        cb = lax.dot_general(cg32.astype(BF16), bg32.astype(BF16), (((1,), (1,)), ((), ())),
                             preferred_element_type=F32)
        st = st_ref[g]
        xg = xs[:, g * gw:(g + 1) * gw]
        rhs = jnp.concatenate([xsb[:, g * gw:(g + 1) * gw], st.astype(BF16)], axis=0)
        y_g = None
        for k in reversed(range(hpg)):
            h = g * hpg + k
            a_col = jnp.broadcast_to(acs[:, h:h + 1], (q, q))
            seg = a_col - jnp.broadcast_to(acs_t[h:h + 1, :], (q, q))
            lmat = jnp.exp(jnp.where(mask, seg, -jnp.inf))
            m_h = (cb * lmat * jnp.broadcast_to(dt_t[h:h + 1, :], (q, q))).astype(BF16)
            c_h = (cg32 * jnp.exp(a_col)).astype(BF16)
            res = jnp.dot(jnp.concatenate([m_h, c_h], axis=1), rhs, preferred_element_type=F32)
            y_g = res if y_g is None else jnp.where(lane_g < (k + 1) * p, res, y_g)
        ys.append(y_g)
        xdd = (xg * expand(dtd, g)).astype(BF16)
        new = jnp.dot(bg32.T.astype(BF16), xdd, preferred_element_type=F32)
        st_ref[g] = st * expand(cdec, g) + new
    y = jnp.concatenate(ys, axis=1)

    if finish:
        y = yf_ref[0] + y + xs * dsk_ref[...]
        y = y * _silu(z_ref[0].astype(F32))
        o_ref[0] = _rms(y, ng_ref[...]).astype(o_ref.dtype)
    else:
        o_ref[0] = y


def _ssd_direction(p0, s0, conv_w, conv_b, dt_bias, a_log, n_lat, reverse, extra=None):
    b, n, _ = p0.shape
    q = SSD_CHUNK
    nc = n // q
    nl = n_lat // q
    halo = 16
    hpc = q // halo
    nh = n // halo
    d = 1 if reverse else 0
    cidx = functools.partial(_ssd_chunk_index, nl=nl, nc=nc, reverse=reverse)
    finish = extra is not None
    in_specs = [
        pl.BlockSpec((1, q, SSD_CONV_DIM), lambda i, s: (i, cidx(s), 0)),
        pl.BlockSpec((1, halo, SSD_CONV_DIM), lambda i, s: (i, jnp.maximum(cidx(s) * hpc - 1, 0), 0)),
        pl.BlockSpec((1, halo, SSD_CONV_DIM), lambda i, s: (i, jnp.minimum(cidx(s) * hpc + hpc, nh - 1), 0)),
        pl.BlockSpec((1, q, LANES), lambda i, s: (i, cidx(s), d)),
        pl.BlockSpec((3, SSD_CONV_DIM), lambda i, s: (0, 0)),
        pl.BlockSpec((1, SSD_CONV_DIM), lambda i, s: (0, 0)),
        pl.BlockSpec((1, LANES), lambda i, s: (0, 0)),
        pl.BlockSpec((1, LANES), lambda i, s: (0, 0)),
    ]
    args = [p0, p0, p0, s0, conv_w, conv_b, dt_bias, a_log]
    if finish:
        yf, d_skip, norm_g = extra
        in_specs += [
            pl.BlockSpec((1, q, SSD_INNER), lambda i, s: (i, cidx(s), SSD_CONV_DIM // SSD_INNER)),
            pl.BlockSpec((1, q, SSD_INNER), lambda i, s: (i, cidx(s), 0)),
            pl.BlockSpec((1, SSD_INNER), lambda i, s: (0, 0)),
            pl.BlockSpec((1, SSD_INNER), lambda i, s: (0, 0)),
        ]
        args += [p0, yf, d_skip, norm_g]
    kern = functools.partial(_ssd_kernel, reverse=reverse, finish=finish, nl=nl, nc=nc)
    return pl.pallas_call(
        kern,
        grid=(b, nc),
        in_specs=in_specs,
        out_specs=pl.BlockSpec((1, q, SSD_INNER), lambda i, s: (i, cidx(s), 0)),
        out_shape=jax.ShapeDtypeStruct((b, n, SSD_INNER), BF16 if finish else F32),
        scratch_shapes=[pltpu.VMEM((SSD_GROUPS, SSD_STATE, (SSD_HEADS // SSD_GROUPS) * SSD_HEADDIM), F32)],
        compiler_params=_params(("arbitrary", "arbitrary")),
        name="ssd_bwd_finish" if finish else "ssd_fwd",
    )(*args)


def _rope_tables(rows, rot_dim, n_ctx):
    n_freq = rot_dim // 4
    row = jnp.repeat(jnp.arange(rows, dtype=F32), GRID_W)
    col = jnp.tile(jnp.arange(GRID_W, dtype=F32), rows)
    inv = ROPE_THETA ** (-jnp.arange(n_freq, dtype=F32) / n_freq)
    ang = jnp.concatenate([row[:, None] * inv, col[:, None] * inv], axis=-1)
    cos, sin = jnp.cos(ang), jnp.sin(ang)
    reps = LANES // rot_dim
    cos_t = jnp.tile(jnp.concatenate([cos, cos], axis=-1), (1, reps))
    sin_t = jnp.tile(jnp.concatenate([-sin, sin], axis=-1), (1, reps))
    cos_t = jnp.concatenate([cos_t, jnp.ones((n_ctx, LANES), F32)], axis=0)
    sin_t = jnp.concatenate([sin_t, jnp.zeros((n_ctx, LANES), F32)], axis=0)
    return cos_t, sin_t


def _rope_block(xb, cos, sin, rot_dim):
    half = rot_dim // 2
    lane = lax.broadcasted_iota(jnp.int32, xb.shape, 1)
    first = jnp.bitwise_and(lane, rot_dim - 1) < half
    partner = jnp.where(first, pltpu.roll(xb, LANES - half, axis=1), pltpu.roll(xb, half, axis=1))
    return xb * cos + partner * sin


def _segment_mean_matrix(seg):
    sh = seg.bit_length() - 1
    i = jnp.right_shift(lax.broadcasted_iota(jnp.int32, (LANES, LANES), 0), sh)
    j = jnp.right_shift(lax.broadcasted_iota(jnp.int32, (LANES, LANES), 1), sh)
    return jnp.where(i == j, 1.0 / seg, 0.0).astype(BF16)


def _segment_mean_sq(xb, smat):
    x2 = xb * xb
    hi = x2.astype(BF16)
    lo = (x2 - hi.astype(F32)).astype(BF16)
    return (jnp.dot(hi, smat, preferred_element_type=F32) + jnp.dot(lo, smat, preferred_element_type=F32))


def _qk_prep_kernel(q_ref, k_ref, v_ref, cos_ref, sin_ref, qg_ref, kg_ref, qo_ref, ko_ref, vo_ref, *, scale):
    smat = _segment_mean_matrix(ATT_HEADDIM)
    cos = cos_ref[...]
    sin = sin_ref[...]
    hpb = LANES // ATT_HEADDIM

    def prep(src_ref, g_ref, dst_ref, n_heads, mul, transposed):
        for c in range(n_heads // hpb):
            xb = src_ref[0, :, c * LANES:(c + 1) * LANES].astype(F32)
            xn = xb * lax.rsqrt(_segment_mean_sq(xb, smat) + EPS) * g_ref[...]
            r = _rope_block(xn, cos, sin, ATT_HEADDIM)
            if mul != 1.0:
                r = r * mul
            if transposed:
                r = r.T
            r = r.astype(dst_ref.dtype)
            for k in range(hpb):
                if transposed:
                    dst_ref[0, c * hpb + k] = r[k * ATT_HEADDIM:(k + 1) * ATT_HEADDIM, :]
                else:
                    dst_ref[0, c * hpb + k] = r[:, k * ATT_HEADDIM:(k + 1) * ATT_HEADDIM]

    prep(q_ref, qg_ref, qo_ref, ATT_HEADS, scale, True)
    prep(k_ref, kg_ref, ko_ref, ATT_KV_HEADS, 1.0, False)
    for c in range(ATT_KV_HEADS // hpb):
        vt = v_ref[0, :, c * LANES:(c + 1) * LANES].astype(F32).T.astype(vo_ref.dtype)
        for k in range(hpb):
            vo_ref[0, c * hpb + k] = vt[k * ATT_HEADDIM:(k + 1) * ATT_HEADDIM, :]


def _qk_prep(p0, cos_t, sin_t, q_g, k_g, q_col, k_col, v_col):
    b, n, _ = p0.shape
    tm = _pick(n, (768, 384, 256, 128))
    kern = functools.partial(_qk_prep_kernel, scale=ATT_HEADDIM ** -0.5 * LOG2E)
    return pl.pallas_call(
        kern,
        grid=(b, n // tm),
        in_specs=[
            pl.BlockSpec((1, tm, ATT_Q), lambda i, t: (i, t, q_col // ATT_Q)),
            pl.BlockSpec((1, tm, ATT_KV), lambda i, t: (i, t, k_col // ATT_KV)),
            pl.BlockSpec((1, tm, ATT_KV), lambda i, t: (i, t, v_col // ATT_KV)),
            pl.BlockSpec((tm, LANES), lambda i, t: (t, 0)),
            pl.BlockSpec((tm, LANES), lambda i, t: (t, 0)),
            pl.BlockSpec((1, LANES), lambda i, t: (0, 0)),
            pl.BlockSpec((1, LANES), lambda i, t: (0, 0)),
        ],
        out_specs=[
            pl.BlockSpec((1, ATT_HEADS, ATT_HEADDIM, tm), lambda i, t: (i, 0, 0, t)),
            pl.BlockSpec((1, ATT_KV_HEADS, tm, ATT_HEADDIM), lambda i, t: (i, 0, t, 0)),
            pl.BlockSpec((1, ATT_KV_HEADS, ATT_HEADDIM, tm), lambda i, t: (i, 0, 0, t)),
        ],
        out_shape=[
            jax.ShapeDtypeStruct((b, ATT_HEADS, ATT_HEADDIM, n), BF16),
            jax.ShapeDtypeStruct((b, ATT_KV_HEADS, n, ATT_HEADDIM), BF16),
            jax.ShapeDtypeStruct((b, ATT_KV_HEADS, ATT_HEADDIM, n), BF16),
        ],
        compiler_params=_params(("arbitrary", "arbitrary")),
        name="gqa_qk_prep",
    )(p0, p0, p0, cos_t, sin_t, q_g, k_g)


def _attn_kernel(qt_ref, k_ref, vt_ref, o_ref, s_scr, m_scr, ot_scr, *, group, hpb, n_lat, n_all, lat_tiles, ctx_self):
    t = pl.program_id(2)
    dv = vt_ref.shape[2]
    tq = qt_ref.shape[3]
    n_blocks = HEAD_BLOCK // hpb

    def run(k_lo, k_len):
        def scores(i):
            kk = k_ref[0, i * hpb // group, k_lo:k_lo + k_len, :]
            qt = jnp.concatenate([qt_ref[0, i * hpb + j] for j in range(hpb)], axis=1)
            s = jnp.dot(kk, qt, preferred_element_type=F32)
            s_scr[i % 2, :k_len, :] = s
            m_scr[i % 2] = jnp.max(s, axis=0, keepdims=True)

        def combine(i):
            row = lax.broadcasted_iota(jnp.int32, (BF16_ROWS, k_len), 0)
            ones_row = jnp.where(row == 0, 1.0, 0.0).astype(BF16)
            e = jnp.exp2(s_scr[i % 2, :k_len, :] - m_scr[i % 2]).astype(BF16)
            vt = vt_ref[0, i * hpb // group, :, k_lo:k_lo + k_len]
            ot = jnp.dot(jnp.concatenate([vt, ones_row], axis=0), e, preferred_element_type=F32)
            ot = ot[:dv] * (1.0 / ot[dv:dv + 1])
            for j in range(hpb):
                ot_scr[(i * hpb + j) * dv:(i * hpb + j + 1) * dv, :] = ot[:, j * tq:(j + 1) * tq]

        scores(0)
        for i in range(n_blocks):
            if i + 1 < n_blocks:
                scores(i + 1)
            combine(i)
        o_ref[0] = ot_scr[...].T.astype(o_ref.dtype)

    if ctx_self:
        @pl.when(t < lat_tiles)
        def _():
            run(0, n_all)

        @pl.when(t >= lat_tiles)
        def _():
            run(n_lat, n_all - n_lat)
    else:
        run(0, n_all)


def _attention(qt, k, vt, n_lat, ctx_self):
    b, h, dqk, nq = qt.shape
    hkv, dv, n_all = vt.shape[1], vt.shape[2], vt.shape[3]
    group = h // hkv
    kvb = max(HEAD_BLOCK // group, 1)
    if ctx_self:
        tq = _pick(math.gcd(n_lat, n_all - n_lat), (256, 128))
    else:
        tq = _pick(n_lat, (ATTN_BLOCK_LANES, 256, 128))
    hpb = max(1, min(group, ATTN_BLOCK_LANES // tq))
    assert nq == (n_all if ctx_self else n_lat)
    kern = functools.partial(_attn_kernel, group=group, hpb=hpb, n_lat=n_lat, n_all=n_all,
                             lat_tiles=n_lat // tq, ctx_self=ctx_self)
    return pl.pallas_call(
        kern,
        grid=(b, h // HEAD_BLOCK, nq // tq),
        in_specs=[
            pl.BlockSpec((1, HEAD_BLOCK, dqk, tq), lambda i, hb, t: (i, hb, 0, t)),
            pl.BlockSpec((1, kvb, n_all, dqk), lambda i, hb, t: (i, hb, 0, 0)),
            pl.BlockSpec((1, kvb, dv, n_all), lambda i, hb, t: (i, hb, 0, 0)),
        ],
        out_specs=pl.BlockSpec((1, tq, HEAD_BLOCK * dv), lambda i, hb, t: (i, t, hb)),
        out_shape=jax.ShapeDtypeStruct((b, nq, h * dv), BF16),
        scratch_shapes=[pltpu.VMEM((2, n_all, tq * hpb), F32), pltpu.VMEM((2, 1, tq * hpb), F32),
                        pltpu.VMEM((HEAD_BLOCK * dv, tq), F32)],
        compiler_params=_params(("arbitrary", "arbitrary", "arbitrary")),
        name="attention",
    )(qt, k, vt)


def _out_proj_kernel(*refs, n_in, tm, n_lat):
    x_ref, ml_ref, mc_ref = refs[:3]
    a_refs = refs[3:3 + n_in]
    w_refs = refs[3 + n_in:3 + 2 * n_in]
    o_ref = refs[3 + 2 * n_in]
    t = pl.program_id(1)
    acc = jnp.dot(a_refs[0][0], w_refs[0][...], preferred_element_type=F32)
    for a_ref, w_ref in zip(a_refs[1:], w_refs[1:]):
        acc = acc + jnp.dot(a_ref[0], w_ref[...], preferred_element_type=F32)
    gate = _mod_row(ml_ref, mc_ref, 2, _is_lat(t, tm, n_lat))
    o_ref[0] = x_ref[0] + gate * acc


def _out_proj(xa, mods_l, mods_c, acts, weights, n_rows, n_lat):
    b, _, d = xa.shape
    tm = _pick(n_rows, (1024, 768, 512, 384, 256, 128))
    n_in = len(acts)
    kern = functools.partial(_out_proj_kernel, n_in=n_in, tm=tm, n_lat=n_lat)
    in_specs = [
        pl.BlockSpec((1, tm, d), lambda i, t: (i, t, 0)),
        pl.BlockSpec((1, ADA_CHUNKS, d), lambda i, t: (i, 0, 0)),
        pl.BlockSpec((ADA_CHUNKS, d), lambda i, t: (0, 0)),
    ]
    in_specs += [pl.BlockSpec((1, tm, a.shape[2]), lambda i, t: (i, t, 0)) for a in acts]
    in_specs += [pl.BlockSpec(w.shape, lambda i, t: (0, 0)) for w in weights]
    return pl.pallas_call(
        kern,
        grid=(b, n_rows // tm),
        in_specs=in_specs,
        out_specs=pl.BlockSpec((1, tm, d), lambda i, t: (i, t, 0)),
        out_shape=jax.ShapeDtypeStruct((b, n_rows, d), F32),
        compiler_params=_params(("arbitrary", "arbitrary")),
        name="out_proj",
    )(xa, mods_l, mods_c, *acts, *weights)


def _hidden_chunks(hidden):
    chunks, lo = [], 0
    while lo < hidden:
        w = min(1024, hidden - lo)
        chunks.append((lo, w))
        lo += w
    return chunks


def _ffn_kernel(x_ref, ml_ref, mc_ref, g_ref, wg_ref, wu_ref, wd_ref, fg_ref, o_ref, *, tm, n_lat, final):
    t = pl.program_id(1)
    is_lat = _is_lat(t, tm, n_lat)
    x = x_ref[0]
    h = _rms(x, g_ref[...])
    h = (h * (1.0 + _mod_row(ml_ref, mc_ref, 4, is_lat)) + _mod_row(ml_ref, mc_ref, 3, is_lat)).astype(BF16)
    acc = None
    for lo, w in _hidden_chunks(wd_ref.shape[0]):
        gate = jnp.dot(h, wg_ref[:, lo:lo + w], preferred_element_type=F32)
        up = jnp.dot(h, wu_ref[:, lo:lo + w], preferred_element_type=F32)
        a = (_silu(gate) * up).astype(BF16)
        part = jnp.dot(a, wd_ref[lo:lo + w, :], preferred_element_type=F32)
        acc = part if acc is None else acc + part
    y = x + _mod_row(ml_ref, mc_ref, 5, is_lat) * acc
    if final:
        y = _rms(y, fg_ref[...])
    o_ref[0] = y


def _ffn(xa, mods_l, mods_c, g, w_gate, w_up, w_down, final_g, n_lat, final):
    b, n, d = xa.shape
    hid = w_down.shape[0]
    tm = _pick(n, (768, 512, 384, 256, 128))
    kern = functools.partial(_ffn_kernel, tm=tm, n_lat=n_lat, final=final)
    resident = dict(pipeline_mode=pl.Buffered(1))
    return pl.pallas_call(
        kern,
        grid=(b, n // tm),
        in_specs=[
            pl.BlockSpec((1, tm, d), lambda i, t: (i, t, 0)),
            pl.BlockSpec((1, ADA_CHUNKS, d), lambda i, t: (i, 0, 0)),
            pl.BlockSpec((ADA_CHUNKS, d), lambda i, t: (0, 0)),
            pl.BlockSpec((1, d), lambda i, t: (0, 0)),
            pl.BlockSpec((d, hid), lambda i, t: (0, 0), **resident),
            pl.BlockSpec((d, hid), lambda i, t: (0, 0), **resident),
            pl.BlockSpec((hid, d), lambda i, t: (0, 0), **resident),
            pl.BlockSpec((1, d), lambda i, t: (0, 0)),
        ],
        out_specs=pl.BlockSpec((1, tm, d), lambda i, t: (i, t, 0)),
        out_shape=jax.ShapeDtypeStruct((b, n, d), F32),
        compiler_params=_params(("arbitrary", "arbitrary")),
        name="ffn_final" if final else "ffn",
    )(xa, mods_l, mods_c, g, w_gate, w_up, w_down, final_g)


def _mla_q_kernel(p_ref, g_ref, w_ref, cos_ref, sin_ref, o_ref, *, q_rank, scale):
    cq = p_ref[0, :, :q_rank].astype(F32)
    qn = _rms(cq, g_ref[...]).astype(BF16)
    qq = jnp.dot(qn, w_ref[...], preferred_element_type=F32)
    qq = qq * scale
    n_nope = MLA_HEADS * MLA_NOPE
    ppb = LANES // MLA_ROPE
    npb = LANES // MLA_NOPE
    pes_t = [_rope_block(qq[:, n_nope + c * LANES:n_nope + (c + 1) * LANES], cos_ref[...], sin_ref[...], MLA_ROPE).T
             for c in range(MLA_HEADS // ppb)]
    for c in range(MLA_HEADS // npb):
        nope_t = qq[:, c * LANES:(c + 1) * LANES].T
        for k in range(npb):
            h = c * npb + k
            pe_t = pes_t[h // ppb][(h % ppb) * MLA_ROPE:(h % ppb + 1) * MLA_ROPE, :]
            qh = jnp.concatenate([nope_t[k * MLA_NOPE:(k + 1) * MLA_NOPE, :], pe_t], axis=0)
            o_ref[0, h] = qh.astype(o_ref.dtype)


def _mla_q(p1, g, w_uq, cos_t, sin_t, n_lat, q_rank):
    b = p1.shape[0]
    cols = p1.shape[2]
    tm = _pick(n_lat, (512, 256, 128))
    kern = functools.partial(_mla_q_kernel, q_rank=q_rank, scale=MLA_QK ** -0.5 * LOG2E)
    return pl.pallas_call(
        kern,
        grid=(b, n_lat // tm),
        in_specs=[
            pl.BlockSpec((1, tm, cols), lambda i, t: (i, t, 0)),
            pl.BlockSpec((1, q_rank), lambda i, t: (0, 0)),
            pl.BlockSpec(w_uq.shape, lambda i, t: (0, 0)),
            pl.BlockSpec((tm, LANES), lambda i, t: (t, 0)),
            pl.BlockSpec((tm, LANES), lambda i, t: (t, 0)),
        ],
        out_specs=pl.BlockSpec((1, MLA_HEADS, MLA_QK, tm), lambda i, t: (i, 0, 0, t)),
        out_shape=jax.ShapeDtypeStruct((b, MLA_HEADS, MLA_QK, n_lat), BF16),
        compiler_params=_params(("arbitrary", "arbitrary")),
        name="mla_q",
    )(p1, g, w_uq, cos_t, sin_t)


def _mla_kv_kernel(p_ref, pe_ref, g_ref, w_ref, cos_ref, sin_ref, ko_ref, vo_ref, *, q_rank):
    ckv = p_ref[0, :, q_rank:].astype(F32)
    kvn = _rms(ckv, g_ref[...]).astype(BF16)
    kv = jnp.dot(kvn, w_ref[...], preferred_element_type=F32)
    pe = _rope_block(pe_ref[0], cos_ref[...], sin_ref[...], MLA_ROPE)[:, :MLA_ROPE]
    n_nope = MLA_HEADS * MLA_NOPE
    for h in range(MLA_HEADS):
        kh = jnp.concatenate([kv[:, h * MLA_NOPE:(h + 1) * MLA_NOPE], pe], axis=1)
        ko_ref[0, h] = kh.astype(ko_ref.dtype)
    hpb = LANES // MLA_V
    for c in range(MLA_HEADS // hpb):
        v_t = kv[:, n_nope + c * LANES:n_nope + (c + 1) * LANES].T.astype(vo_ref.dtype)
        for k in range(hpb):
            vo_ref[0, c * hpb + k] = v_t[k * MLA_V:(k + 1) * MLA_V, :]


def _mla_kv(p1, s1, g, w_ukv, cos_t, sin_t, q_rank):
    b, n, cols = p1.shape
    kv_rank = cols - q_rank
    tm = _pick(n, (768, 384, 256, 128))
    kern = functools.partial(_mla_kv_kernel, q_rank=q_rank)
    return pl.pallas_call(
        kern,
        grid=(b, n // tm),
        in_specs=[
            pl.BlockSpec((1, tm, cols), lambda i, t: (i, t, 0)),
            pl.BlockSpec((1, tm, LANES), lambda i, t: (i, t, 0)),
            pl.BlockSpec((1, kv_rank), lambda i, t: (0, 0)),
            pl.BlockSpec(w_ukv.shape, lambda i, t: (0, 0)),
            pl.BlockSpec((tm, LANES), lambda i, t: (t, 0)),
            pl.BlockSpec((tm, LANES), lambda i, t: (t, 0)),
        ],
        out_specs=[
            pl.BlockSpec((1, MLA_HEADS, tm, MLA_QK), lambda i, t: (i, 0, t, 0)),
            pl.BlockSpec((1, MLA_HEADS, MLA_V, tm), lambda i, t: (i, 0, 0, t)),
        ],
        out_shape=[
            jax.ShapeDtypeStruct((b, MLA_HEADS, n, MLA_QK), BF16),
            jax.ShapeDtypeStruct((b, MLA_HEADS, MLA_V, n), BF16),
        ],
        compiler_params=_params(("arbitrary", "arbitrary")),
        name="mla_kv",
    )(p1, s1, g, w_ukv, cos_t, sin_t)


def _pad_lanes(v, width=LANES):
    v = v.reshape(1, -1).astype(F32)
    return jnp.pad(v, ((0, 0), (0, width - v.shape[1])))


def kernel(x, c, ctx, c_ctx, ada_w, ada_b, norm1_g, norm2_g, ffn_w_up, ffn_w_down, ab_w_in, ab_w_out, ssd_conv_w, ssd_conv_b, ssd_a_log, ssd_dt_bias, ssd_d, ssd_norm_g, att_q_g, att_k_g, mla_w_in, mla_q_norm_g, mla_w_uq, mla_kv_norm_g, mla_w_ukv, mla_w_o, final_norm_g):
    b, t_lat, d = x.shape
    n_ctx = ctx.shape[1]
    rows = t_lat // GRID_W
    hid = ffn_w_down.shape[1]
    row2 = lambda v: v.reshape(1, -1).astype(F32)

    bp = -(-(b + 1) // 8) * 8
    cvec = jnp.concatenate([c, c_ctx[None, :], jnp.zeros((bp - b - 1, d), F32)], axis=0)
    mods = _ada_mods(cvec, ada_w, ada_b)
    mods_l = [mods[i, :b].reshape(b, ADA_CHUNKS, d) for i in range(2)]
    mods_c = [mods[i, b].reshape(ADA_CHUNKS, d) for i in range(2)]

    xa = jnp.concatenate([x, ctx], axis=1)

    w_in = ab_w_in[0]
    o_xbc, o_dt = SSD_INNER, SSD_INNER + SSD_CONV_DIM
    o_att = o_dt + 2 * SSD_HEADS
    w_main = jnp.concatenate([w_in[:, o_xbc:o_dt], w_in[:, :SSD_INNER], w_in[:, o_att:]], axis=1).astype(BF16)
    w_dt = jnp.zeros((d, 2 * LANES), F32)
    w_dt = w_dt.at[:, :SSD_HEADS].set(w_in[:, o_dt:o_dt + SSD_HEADS])
    w_dt = w_dt.at[:, LANES:LANES + SSD_HEADS].set(w_in[:, o_dt + SSD_HEADS:o_att]).astype(BF16)
    p0, s0 = _in_proj(xa, mods_l[0], mods_c[0], row2(norm1_g[0]), w_main, w_dt, t_lat)

    conv_w = ssd_conv_w[0].astype(F32)
    conv_b = row2(ssd_conv_b[0])
    yf = _ssd_direction(p0, s0, conv_w, conv_b, _pad_lanes(ssd_dt_bias[0, 0]), _pad_lanes(ssd_a_log[0, 0]),
                        t_lat, reverse=False)
    d_skip = jnp.repeat(ssd_d[0].astype(F32), SSD_HEADDIM).reshape(1, SSD_INNER)
    y_ssd = _ssd_direction(p0, s0, conv_w, conv_b, _pad_lanes(ssd_dt_bias[0, 1]), _pad_lanes(ssd_a_log[0, 1]),
                           t_lat, reverse=True, extra=(yf, d_skip, row2(ssd_norm_g[0])))

    q_col = SSD_CONV_DIM + SSD_INNER
    k_col = q_col + ATT_Q
    cos_a, sin_a = _rope_tables(rows, ATT_HEADDIM, n_ctx)
    hpb = LANES // ATT_HEADDIM
    qh, kh, vh = _qk_prep(p0, cos_a, sin_a, jnp.tile(row2(att_q_g[0]), (1, hpb)),
                          jnp.tile(row2(att_k_g[0]), (1, hpb)), q_col, k_col, k_col + ATT_KV)
    o_att_l0 = _attention(qh, kh, vh, t_lat, ctx_self=True)

    w_out = ab_w_out[0].astype(BF16)
    xa = _out_proj(xa, mods_l[0], mods_c[0], [y_ssd, o_att_l0], [w_out[:SSD_INNER], w_out[SSD_INNER:]],
                   t_lat + n_ctx, t_lat)
    w_up = ffn_w_up[0].astype(BF16)
    xa = _ffn(xa, mods_l[0], mods_c[0], row2(norm2_g[0]), w_up[:, :hid], w_up[:, hid:],
              ffn_w_down[0].astype(BF16), row2(final_norm_g), t_lat, final=False)

    q_rank = mla_q_norm_g.shape[1]
    kv_rank = mla_kv_norm_g.shape[1]
    w_in1 = mla_w_in[0]
    w_pe = jnp.pad(w_in1[:, q_rank + kv_rank:], ((0, 0), (0, LANES - MLA_ROPE))).astype(BF16)
    p1, s1 = _in_proj(xa, mods_l[1], mods_c[1], row2(norm1_g[1]), w_in1[:, :q_rank + kv_rank].astype(BF16), w_pe,
                      t_lat)
    cos_m, sin_m = _rope_tables(rows, MLA_ROPE, n_ctx)
    w_uq = mla_w_uq[0].reshape(q_rank, MLA_HEADS, MLA_QK)
    w_uq = jnp.concatenate([w_uq[:, :, :MLA_NOPE].reshape(q_rank, -1), w_uq[:, :, MLA_NOPE:].reshape(q_rank, -1)],
                           axis=1).astype(BF16)
    w_ukv = mla_w_ukv[0].reshape(kv_rank, MLA_HEADS, MLA_NOPE + MLA_V)
    w_ukv = jnp.concatenate([w_ukv[:, :, :MLA_NOPE].reshape(kv_rank, -1), w_ukv[:, :, MLA_NOPE:].reshape(kv_rank, -1)],
                            axis=1).astype(BF16)
    qm = _mla_q(p1, row2(mla_q_norm_g[0]), w_uq, cos_m, sin_m, t_lat, q_rank)
    km, vm = _mla_kv(p1, s1, row2(mla_kv_norm_g[0]), w_ukv, cos_m, sin_m, q_rank)
    o_mla = _attention(qm, km, vm, t_lat, ctx_self=False)

    xl = _out_proj(xa, mods_l[1], mods_c[1], [o_mla], [mla_w_o[0].astype(BF16)], t_lat, t_lat)
    w_up = ffn_w_up[1].astype(BF16)
    return _ffn(xl, mods_l[1], mods_c[1], row2(norm2_g[1]), w_up[:, :hid], w_up[:, hid:],
                ffn_w_down[1].astype(BF16), row2(final_norm_g), t_lat, final=True)
```

```python
import functools
import math

import jax
import jax.numpy as jnp
from jax import lax
from jax.experimental import pallas as pl
from jax.experimental.pallas import tpu as pltpu

F32 = jnp.float32
BF16 = jnp.bfloat16

EPS = 1e-6
ROPE_THETA = 10000.0
GRID_W = 64
ADA_CHUNKS = 6

SSD_HEADS = 16
SSD_HEADDIM = 64
SSD_GROUPS = 4
SSD_STATE = 128
SSD_CHUNK = 128
SSD_INNER = SSD_HEADS * SSD_HEADDIM
SSD_GN = SSD_GROUPS * SSD_STATE
SSD_CONV_DIM = SSD_INNER + 2 * SSD_GN

ATT_HEADS = 16
ATT_KV_HEADS = 4
ATT_HEADDIM = 64
ATT_Q = ATT_HEADS * ATT_HEADDIM
ATT_KV = ATT_KV_HEADS * ATT_HEADDIM

MLA_HEADS = 16
MLA_NOPE = 64
MLA_ROPE = 32
MLA_V = 64
MLA_QK = MLA_NOPE + MLA_ROPE

LANES = 128
SUBLANES = 8
BF16_ROWS = 16
ATTN_BLOCKS_PER_STEP = 8
ATTN_BLOCK_LANES = 512
LOG2E = math.log2(math.e)
VMEM_LIMIT = 56 * 1024 * 1024


def _pick(n, candidates):
    for c in candidates:
        if n % c == 0:
            return c
    raise ValueError(f"no tile for {n} in {candidates}")


def _params(sem):
    return pltpu.CompilerParams(dimension_semantics=sem, vmem_limit_bytes=VMEM_LIMIT)


def _silu(x):
    hx = 0.5 * x
    return hx + hx * jnp.tanh(hx)


def _rms(x, g):
    ms = jnp.mean(x * x, axis=-1, keepdims=True)
    return x * lax.rsqrt(ms + EPS) * g


def _mod_row(ml_ref, mc_ref, idx, is_lat):
    return jnp.where(is_lat, ml_ref[0, idx:idx + 1, :], mc_ref[idx:idx + 1, :])


def _is_lat(t, tm, n_lat):
    rows = t * tm + lax.broadcasted_iota(jnp.int32, (tm, 1), 0)
    return rows < n_lat


def _ada_kernel(c_ref, w_ref, b_ref, o_ref):
    s = _silu(c_ref[...]).astype(BF16)
    o_ref[0] = jnp.dot(s, w_ref[0].astype(BF16), preferred_element_type=F32) + b_ref[0]


def _ada_mods(cvec, ada_w, ada_b):
    depth, d, n = ada_w.shape
    bp = cvec.shape[0]
    tn = _pick(n, (1536, 1024, 512, 256, 128))
    return pl.pallas_call(
        _ada_kernel,
        grid=(depth, n // tn),
        in_specs=[
            pl.BlockSpec((bp, d), lambda i, j: (0, 0)),
            pl.BlockSpec((1, d, tn), lambda i, j: (i, 0, j)),
            pl.BlockSpec((1, 1, tn), lambda i, j: (i, 0, j)),
        ],
        out_specs=pl.BlockSpec((1, bp, tn), lambda i, j: (i, 0, j)),
        out_shape=jax.ShapeDtypeStruct((depth, bp, n), F32),
        compiler_params=_params(("arbitrary", "arbitrary")),
        name="ada_mods",
    )(cvec, ada_w, ada_b.reshape(depth, 1, n))


def _in_proj_kernel(x_ref, ml_ref, mc_ref, g_ref, w_ref, ws_ref, o_ref, os_ref, h_ref, *, tm, n_lat):
    t = pl.program_id(1)
    j = pl.program_id(2)

    @pl.when(j == 0)
    def _():
        rc = _pick(tm, (256, 128))
        for r in range(0, tm, rc):
            is_lat = (t * tm + r + lax.broadcasted_iota(jnp.int32, (rc, 1), 0)) < n_lat
            h = _rms(x_ref[0, r:r + rc, :], g_ref[...])
            h = h * (1.0 + _mod_row(ml_ref, mc_ref, 1, is_lat)) + _mod_row(ml_ref, mc_ref, 0, is_lat)
            hb = h.astype(BF16)
            h_ref[r:r + rc, :] = hb
            os_ref[0, r:r + rc, :] = jnp.dot(hb, ws_ref[...], preferred_element_type=F32)
            o_ref[0, r:r + rc, :] = jnp.dot(hb, w_ref[...], preferred_element_type=F32).astype(o_ref.dtype)

    @pl.when(j > 0)
    def _():
        o_ref[0] = jnp.dot(h_ref[...], w_ref[...], preferred_element_type=F32).astype(o_ref.dtype)


def _in_proj(xa, mods_l, mods_c, g, w_main, w_side, n_lat):
    b, n, d = xa.shape
    nm = w_main.shape[1]
    ns = w_side.shape[1]
    tm = _pick(n, (768, 384, 256, 128))
    tn = _pick(nm, (1536, 1024, 768, 640, 512, 256, 128))
    kern = functools.partial(_in_proj_kernel, tm=tm, n_lat=n_lat)
    return pl.pallas_call(
        kern,
        grid=(b, n // tm, nm // tn),
        in_specs=[
            pl.BlockSpec((1, tm, d), lambda i, t, j: (i, t, 0)),
            pl.BlockSpec((1, ADA_CHUNKS, d), lambda i, t, j: (i, 0, 0)),
            pl.BlockSpec((ADA_CHUNKS, d), lambda i, t, j: (0, 0)),
            pl.BlockSpec((1, d), lambda i, t, j: (0, 0)),
            pl.BlockSpec((d, tn), lambda i, t, j: (0, j)),
            pl.BlockSpec((d, ns), lambda i, t, j: (0, 0)),
        ],
        out_specs=[
            pl.BlockSpec((1, tm, tn), lambda i, t, j: (i, t, j)),
            pl.BlockSpec((1, tm, ns), lambda i, t, j: (i, t, 0)),
        ],
        out_shape=[
            jax.ShapeDtypeStruct((b, n, nm), BF16),
            jax.ShapeDtypeStruct((b, n, ns), F32),
        ],
        scratch_shapes=[pltpu.VMEM((tm, d), BF16)],
        compiler_params=_params(("arbitrary", "arbitrary", "arbitrary")),
        name="in_proj",
    )(xa, mods_l, mods_c, g, w_main, w_side)


def _cumsum_rows(v, reverse):
    n = v.shape[0]
    row = lax.broadcasted_iota(jnp.int32, v.shape, 0)
    k = 1
    while k < n:
        if reverse:
            v = v + jnp.where(row < n - k, pltpu.roll(v, n - k, axis=0), 0.0)
        else:
            v = v + jnp.where(row >= k, pltpu.roll(v, k, axis=0), 0.0)
        k *= 2
    return v


def _ssd_chunk_index(s, nl, nc, reverse):
    if reverse:
        return nc - 1 - s
    return lax.rem(s + nl, nc)


def _ssd_kernel(*refs, reverse, finish, nl, nc):
    if finish:
        (xm_ref, xp_ref, xn_ref, dt_ref, cw_ref, cb_ref, dtb_ref, alog_ref,
         z_ref, yf_ref, dsk_ref, ng_ref, o_ref, st_ref) = refs
    else:
        (xm_ref, xp_ref, xn_ref, dt_ref, cw_ref, cb_ref, dtb_ref, alog_ref, o_ref, st_ref) = refs
    q = SSD_CHUNK
    p = SSD_HEADDIM
    hpg = SSD_HEADS // SSD_GROUPS
    gw = hpg * p
    s = pl.program_id(1)
    chunk = _ssd_chunk_index(s, nl, nc, reverse)

    @pl.when(s == 0)
    def _():
        st_ref[...] = jnp.zeros_like(st_ref)

    xmb = xm_ref[0]
    xm = xmb.astype(F32)
    halo = xp_ref.shape[1]
    seq_first = jnp.logical_or(chunk == 0, chunk == nl)
    seq_last = jnp.logical_or(chunk == nl - 1, chunk == nc - 1)
    prev_row = xp_ref[0, halo - 1:halo, :].astype(F32) * jnp.where(seq_first, 0.0, 1.0)
    next_row = xn_ref[0, 0:1, :].astype(F32) * jnp.where(seq_last, 0.0, 1.0)
    ii = lax.broadcasted_iota(jnp.int32, (q, q), 0)
    jj = lax.broadcasted_iota(jnp.int32, (q, q), 1)
    shift_dn = jnp.where(ii == jj + 1, 1.0, 0.0).astype(BF16)
    shift_up = jnp.where(ii + 1 == jj, 1.0, 0.0).astype(BF16)
    x_prev = jnp.dot(shift_dn, xmb, preferred_element_type=F32)
    x_next = jnp.dot(shift_up, xmb, preferred_element_type=F32)
    row8 = lax.broadcasted_iota(jnp.int32, (SUBLANES, 1), 0)
    x_prev = jnp.concatenate(
        [x_prev[:SUBLANES] + jnp.where(row8 == 0, prev_row, 0.0), x_prev[SUBLANES:]], axis=0)
    x_next = jnp.concatenate(
        [x_next[:q - SUBLANES], x_next[q - SUBLANES:] + jnp.where(row8 == SUBLANES - 1, next_row, 0.0)], axis=0)
    xc = x_prev * cw_ref[0:1, :] + xm * cw_ref[1:2, :] + x_next * cw_ref[2:3, :] + cb_ref[...]
    act = _silu(xc)
    xs = act[:, :SSD_INNER]
    xsb = xs.astype(BF16)
    bm = act[:, SSD_INNER:SSD_INNER + SSD_GN]
    cm = act[:, SSD_INNER + SSD_GN:]

    dtr = dt_ref[0] + dtb_ref[...]
    dt = jnp.maximum(dtr, 0.0) + jnp.log(1.0 + jnp.exp(-jnp.abs(dtr)))
    a = -jnp.exp(alog_ref[...])
    acs = _cumsum_rows(dt * a, reverse)
    acs_t = acs.T
    dt_t = dt.T
    edge = acs[0:1, :] if reverse else acs[q - 1:q, :]
    dtd = dt * jnp.exp(edge - acs)
    cdec = jnp.exp(edge)
    mask = (ii <= jj) if reverse else (ii >= jj)
    lane_h = lax.broadcasted_iota(jnp.int32, (q, LANES), 1)
    lane_g = lax.broadcasted_iota(jnp.int32, (q, gw), 1)

    def expand(v, g):
        rows = v.shape[0]
        parts = []
        for k in range(0, hpg, LANES // p):
            h0 = g * hpg + k
            lo = jnp.broadcast_to(v[:, h0:h0 + 1], (rows, LANES))
            hi = jnp.broadcast_to(v[:, h0 + 1:h0 + 2], (rows, LANES))
            parts.append(jnp.where(lane_h[:rows] < p, lo, hi))
        return jnp.concatenate(parts, axis=1)

    ys = []
    for g in range(SSD_GROUPS):
        cg32 = cm[:, g * SSD_STATE:(g + 1) * SSD_STATE]
        bg32 = bm[:, g * SSD_STATE:(g + 1) * SSD_STATE]
        cb = lax.dot_general(cg32.astype(BF16), bg32.astype(BF16), (((1,), (1,)), ((), ())),
                             preferred_element_type=F32)
        st = st_ref[g]
        xg = xs[:, g * gw:(g + 1) * gw]
        rhs = jnp.concatenate([xsb[:, g * gw:(g + 1) * gw], st.astype(BF16)], axis=0)
        y_g = None
        for k in reversed(range(hpg)):
            h = g * hpg + k
            a_col = jnp.broadcast_to(acs[:, h:h + 1], (q, q))
            seg = a_col - jnp.broadcast_to(acs_t[h:h + 1, :], (q, q))
            lmat = jnp.exp(jnp.where(mask, seg, -jnp.inf))
            m_h = (cb * lmat * jnp.broadcast_to(dt_t[h:h + 1, :], (q, q))).astype(BF16)
            c_h = (cg32 * jnp.exp(a_col)).astype(BF16)
            res = jnp.dot(jnp.concatenate([m_h, c_h], axis=1), rhs, preferred_element_type=F32)
            y_g = res if y_g is None else jnp.where(lane_g < (k + 1) * p, res, y_g)
        ys.append(y_g)
        xdd = (xg * expand(dtd, g)).astype(BF16)
        new = jnp.dot(bg32.T.astype(BF16), xdd, preferred_element_type=F32)
        st_ref[g] = st * expand(cdec, g) + new
    y = jnp.concatenate(ys, axis=1)

    if finish:
        y = yf_ref[0] + y + xs * dsk_ref[...]
        y = y * _silu(z_ref[0].astype(F32))
        o_ref[0] = _rms(y, ng_ref[...]).astype(o_ref.dtype)
    else:
        o_ref[0] = y


def _ssd_direction(p0, s0, conv_w, conv_b, dt_bias, a_log, n_lat, reverse, extra=None):
    b, n, _ = p0.shape
    q = SSD_CHUNK
    nc = n // q
    nl = n_lat // q
    halo = 16
    hpc = q // halo
    nh = n // halo
    d = 1 if reverse else 0
    cidx = functools.partial(_ssd_chunk_index, nl=nl, nc=nc, reverse=reverse)
    finish = extra is not None
    in_specs = [
        pl.BlockSpec((1, q, SSD_CONV_DIM), lambda i, s: (i, cidx(s), 0)),
        pl.BlockSpec((1, halo, SSD_CONV_DIM), lambda i, s: (i, jnp.maximum(cidx(s) * hpc - 1, 0), 0)),
        pl.BlockSpec((1, halo, SSD_CONV_DIM), lambda i, s: (i, jnp.minimum(cidx(s) * hpc + hpc, nh - 1), 0)),
        pl.BlockSpec((1, q, LANES), lambda i, s: (i, cidx(s), d)),
        pl.BlockSpec((3, SSD_CONV_DIM), lambda i, s: (0, 0)),
        pl.BlockSpec((1, SSD_CONV_DIM), lambda i, s: (0, 0)),
        pl.BlockSpec((1, LANES), lambda i, s: (0, 0)),
        pl.BlockSpec((1, LANES), lambda i, s: (0, 0)),
    ]
    args = [p0, p0, p0, s0, conv_w, conv_b, dt_bias, a_log]
    if finish:
        yf, d_skip, norm_g = extra
        in_specs += [
            pl.BlockSpec((1, q, SSD_INNER), lambda i, s: (i, cidx(s), SSD_CONV_DIM // SSD_INNER)),
            pl.BlockSpec((1, q, SSD_INNER), lambda i, s: (i, cidx(s), 0)),
            pl.BlockSpec((1, SSD_INNER), lambda i, s: (0, 0)),
            pl.BlockSpec((1, SSD_INNER), lambda i, s: (0, 0)),
        ]
        args += [p0, yf, d_skip, norm_g]
    kern = functools.partial(_ssd_kernel, reverse=reverse, finish=finish, nl=nl, nc=nc)
    return pl.pallas_call(
        kern,
        grid=(b, nc),
        in_specs=in_specs,
        out_specs=pl.BlockSpec((1, q, SSD_INNER), lambda i, s: (i, cidx(s), 0)),
        out_shape=jax.ShapeDtypeStruct((b, n, SSD_INNER), BF16 if finish else F32),
        scratch_shapes=[pltpu.VMEM((SSD_GROUPS, SSD_STATE, (SSD_HEADS // SSD_GROUPS) * SSD_HEADDIM), F32)],
        compiler_params=_params(("arbitrary", "arbitrary")),
        name="ssd_bwd_finish" if finish else "ssd_fwd",
    )(*args)


def _rope_tables(rows, rot_dim, n_ctx):
    n_freq = rot_dim // 4
    row = jnp.repeat(jnp.arange(rows, dtype=F32), GRID_W)
    col = jnp.tile(jnp.arange(GRID_W, dtype=F32), rows)
    inv = ROPE_THETA ** (-jnp.arange(n_freq, dtype=F32) / n_freq)
    ang = jnp.concatenate([row[:, None] * inv, col[:, None] * inv], axis=-1)
    cos, sin = jnp.cos(ang), jnp.sin(ang)
    reps = LANES // rot_dim
    cos_t = jnp.tile(jnp.concatenate([cos, cos], axis=-1), (1, reps))
    sin_t = jnp.tile(jnp.concatenate([-sin, sin], axis=-1), (1, reps))
    cos_t = jnp.concatenate([cos_t, jnp.ones((n_ctx, LANES), F32)], axis=0)
    sin_t = jnp.concatenate([sin_t, jnp.zeros((n_ctx, LANES), F32)], axis=0)
    return cos_t, sin_t


def _rope_block(xb, cos, sin, rot_dim):
    half = rot_dim // 2
    lane = lax.broadcasted_iota(jnp.int32, xb.shape, 1)
    first = jnp.bitwise_and(lane, rot_dim - 1) < half
    partner = jnp.where(first, pltpu.roll(xb, LANES - half, axis=1), pltpu.roll(xb, half, axis=1))
    return xb * cos + partner * sin


def _segment_mean_matrix(seg):
    sh = seg.bit_length() - 1
    i = jnp.right_shift(lax.broadcasted_iota(jnp.int32, (LANES, LANES), 0), sh)
    j = jnp.right_shift(lax.broadcasted_iota(jnp.int32, (LANES, LANES), 1), sh)
    return jnp.where(i == j, 1.0 / seg, 0.0).astype(BF16)


def _segment_mean_sq(xb, smat):
    x2 = xb * xb
    hi = x2.astype(BF16)
    lo = (x2 - hi.astype(F32)).astype(BF16)
    return (jnp.dot(hi, smat, preferred_element_type=F32) + jnp.dot(lo, smat, preferred_element_type=F32))


def _qk_prep_kernel(q_ref, k_ref, v_ref, cos_ref, sin_ref, qg_ref, kg_ref, qo_ref, ko_ref, vo_ref, *, scale):
    smat = _segment_mean_matrix(ATT_HEADDIM)
    cos = cos_ref[...]
    sin = sin_ref[...]
    hpb = LANES // ATT_HEADDIM

    def prep(src_ref, g_ref, dst_ref, n_heads, mul, transposed):
        for c in range(n_heads // hpb):
            xb = src_ref[0, :, c * LANES:(c + 1) * LANES].astype(F32)
            xn = xb * lax.rsqrt(_segment_mean_sq(xb, smat) + EPS) * g_ref[...]
            r = _rope_block(xn, cos, sin, ATT_HEADDIM)
            if mul != 1.0:
                r = r * mul
            if transposed:
                r = r.T
            r = r.astype(dst_ref.dtype)
            for k in range(hpb):
                if transposed:
                    dst_ref[0, c * hpb + k] = r[k * ATT_HEADDIM:(k + 1) * ATT_HEADDIM, :]
                else:
                    dst_ref[0, c * hpb + k] = r[:, k * ATT_HEADDIM:(k + 1) * ATT_HEADDIM]

    prep(q_ref, qg_ref, qo_ref, ATT_HEADS, scale, True)
    prep(k_ref, kg_ref, ko_ref, ATT_KV_HEADS, 1.0, False)
    for c in range(ATT_KV_HEADS // hpb):
        vt = v_ref[0, :, c * LANES:(c + 1) * LANES].astype(F32).T.astype(vo_ref.dtype)
        for k in range(hpb):
            vo_ref[0, c * hpb + k] = vt[k * ATT_HEADDIM:(k + 1) * ATT_HEADDIM, :]


def _qk_prep(p0, cos_t, sin_t, q_g, k_g, q_col, k_col, v_col):
    b, n, _ = p0.shape
    tm = _pick(n, (768, 384, 256, 128))
    kern = functools.partial(_qk_prep_kernel, scale=ATT_HEADDIM ** -0.5 * LOG2E)
    return pl.pallas_call(
        kern,
        grid=(b, n // tm),
        in_specs=[
            pl.BlockSpec((1, tm, ATT_Q), lambda i, t: (i, t, q_col // ATT_Q)),
            pl.BlockSpec((1, tm, ATT_KV), lambda i, t: (i, t, k_col // ATT_KV)),
            pl.BlockSpec((1, tm, ATT_KV), lambda i, t: (i, t, v_col // ATT_KV)),
            pl.BlockSpec((tm, LANES), lambda i, t: (t, 0)),
            pl.BlockSpec((tm, LANES), lambda i, t: (t, 0)),
            pl.BlockSpec((1, LANES), lambda i, t: (0, 0)),
            pl.BlockSpec((1, LANES), lambda i, t: (0, 0)),
        ],
        out_specs=[
            pl.BlockSpec((1, ATT_HEADS, ATT_HEADDIM, tm), lambda i, t: (i, 0, 0, t)),
            pl.BlockSpec((1, ATT_KV_HEADS, tm, ATT_HEADDIM), lambda i, t: (i, 0, t, 0)),
            pl.BlockSpec((1, ATT_KV_HEADS, ATT_HEADDIM, tm), lambda i, t: (i, 0, 0, t)),
        ],
        out_shape=[
            jax.ShapeDtypeStruct((b, ATT_HEADS, ATT_HEADDIM, n), BF16),
            jax.ShapeDtypeStruct((b, ATT_KV_HEADS, n, ATT_HEADDIM), BF16),
            jax.ShapeDtypeStruct((b, ATT_KV_HEADS, ATT_HEADDIM, n), BF16),
        ],
        compiler_params=_params(("arbitrary", "arbitrary")),
        name="gqa_qk_prep",
    )(p0, p0, p0, cos_t, sin_t, q_g, k_g)


def _attn_kernel(qt_ref, k_ref, vt_ref, o_ref, s_scr, m_scr, ot_scr, *, group, hpb, n_lat, n_all, lat_tiles, ctx_self):
    t = pl.program_id(2)
    dv = vt_ref.shape[2]
    tq = qt_ref.shape[3]
    n_blocks = qt_ref.shape[1] // hpb

    def run(k_lo, k_len):
        def scores(i):
            kk = k_ref[0, i * hpb // group, k_lo:k_lo + k_len, :]
            qt = jnp.concatenate([qt_ref[0, i * hpb + j] for j in range(hpb)], axis=1)
            s = jnp.dot(kk, qt, preferred_element_type=F32)
            s_scr[i % 2, :k_len, :] = s
            m_scr[i % 2] = jnp.max(s, axis=0, keepdims=True)

        def combine(i):
            row = lax.broadcasted_iota(jnp.int32, (BF16_ROWS, k_len), 0)
            ones_row = jnp.where(row == 0, 1.0, 0.0).astype(BF16)
            e = jnp.exp2(s_scr[i % 2, :k_len, :] - m_scr[i % 2]).astype(BF16)
            vt = vt_ref[0, i * hpb // group, :, k_lo:k_lo + k_len]
            ot = jnp.dot(jnp.concatenate([vt, ones_row], axis=0), e, preferred_element_type=F32)
            ot = ot[:dv] * (1.0 / ot[dv:dv + 1])
            for j in range(hpb):
                ot_scr[(i * hpb + j) * dv:(i * hpb + j + 1) * dv, :] = ot[:, j * tq:(j + 1) * tq]

        scores(0)
        for i in range(n_blocks):
            if i + 1 < n_blocks:
                scores(i + 1)
            combine(i)
        o_ref[0] = ot_scr[...].T.astype(o_ref.dtype)

    if ctx_self:
        @pl.when(t < lat_tiles)
        def _():
            run(0, n_all)

        @pl.when(t >= lat_tiles)
        def _():
            run(n_lat, n_all - n_lat)
    else:
        run(0, n_all)


def _attention(qt, k, vt, n_lat, ctx_self):
    b, h, dqk, nq = qt.shape
    hkv, dv, n_all = vt.shape[1], vt.shape[2], vt.shape[3]
    group = h // hkv
    if ctx_self:
        tq = _pick(math.gcd(n_lat, n_all - n_lat), (256, 128))
    else:
        tq = _pick(n_lat, (ATTN_BLOCK_LANES, 256, 128))
    hpb = max(1, min(group, ATTN_BLOCK_LANES // tq))
    heads = min(h, ATTN_BLOCKS_PER_STEP * hpb)
    kvb = max(heads // group, 1)
    assert nq == (n_all if ctx_self else n_lat)
    kern = functools.partial(_attn_kernel, group=group, hpb=hpb, n_lat=n_lat, n_all=n_all,
                             lat_tiles=n_lat // tq, ctx_self=ctx_self)
    return pl.pallas_call(
        kern,
        grid=(b, h // heads, nq // tq),
        in_specs=[
            pl.BlockSpec((1, heads, dqk, tq), lambda i, hb, t: (i, hb, 0, t)),
            pl.BlockSpec((1, kvb, n_all, dqk), lambda i, hb, t: (i, hb, 0, 0)),
            pl.BlockSpec((1, kvb, dv, n_all), lambda i, hb, t: (i, hb, 0, 0)),
        ],
        out_specs=pl.BlockSpec((1, tq, heads * dv), lambda i, hb, t: (i, t, hb)),
        out_shape=jax.ShapeDtypeStruct((b, nq, h * dv), BF16),
        scratch_shapes=[pltpu.VMEM((2, n_all, tq * hpb), F32), pltpu.VMEM((2, 1, tq * hpb), F32),
                        pltpu.VMEM((heads * dv, tq), F32)],
        compiler_params=_params(("arbitrary", "arbitrary", "arbitrary")),
        name="attention",
    )(qt, k, vt)


def _hidden_chunks(hidden):
    chunks, lo = [], 0
    while lo < hidden:
        w = min(1024, hidden - lo)
        chunks.append((lo, w))
        lo += w
    return chunks


def _mix_ffn_kernel(*refs, n_in, tm, n_lat, final):
    x_ref, ml_ref, mc_ref, g_ref = refs[:4]
    a_refs = refs[4:4 + n_in]
    w_refs = refs[4 + n_in:4 + 2 * n_in]
    wg_ref, wu_ref, wd_ref, fg_ref, o_ref = refs[4 + 2 * n_in:]
    t = pl.program_id(1)
    is_lat = _is_lat(t, tm, n_lat)
    mod = functools.partial(_mod_row, ml_ref, mc_ref, is_lat=is_lat)
    mix = jnp.dot(a_refs[0][0], w_refs[0][...], preferred_element_type=F32)
    for a_ref, w_ref in zip(a_refs[1:], w_refs[1:]):
        mix = mix + jnp.dot(a_ref[0], w_ref[...], preferred_element_type=F32)
    x = x_ref[0] + mod(2) * mix
    h = (_rms(x, g_ref[...]) * (1.0 + mod(4)) + mod(3)).astype(BF16)
    acc = None
    for lo, w in _hidden_chunks(wd_ref.shape[0]):
        gate = jnp.dot(h, wg_ref[:, lo:lo + w], preferred_element_type=F32)
        up = jnp.dot(h, wu_ref[:, lo:lo + w], preferred_element_type=F32)
        a = (_silu(gate) * up).astype(BF16)
        part = jnp.dot(a, wd_ref[lo:lo + w, :], preferred_element_type=F32)
        acc = part if acc is None else acc + part
    y = x + mod(5) * acc
    if final:
        y = _rms(y, fg_ref[...])
    o_ref[0] = y


def _mix_ffn(xa, mods_l, mods_c, g, acts, weights, w_gate, w_up, w_down, final_g, n_rows, n_lat, final):
    b, _, d = xa.shape
    hid = w_down.shape[0]
    tm = _pick(n_rows, (512, 384, 256, 128))
    n_in = len(acts)
    kern = functools.partial(_mix_ffn_kernel, n_in=n_in, tm=tm, n_lat=n_lat, final=final)
    resident = dict(pipeline_mode=pl.Buffered(1))
    in_specs = [
        pl.BlockSpec((1, tm, d), lambda i, t: (i, t, 0)),
        pl.BlockSpec((1, ADA_CHUNKS, d), lambda i, t: (i, 0, 0)),
        pl.BlockSpec((ADA_CHUNKS, d), lambda i, t: (0, 0)),
        pl.BlockSpec((1, d), lambda i, t: (0, 0)),
    ]
    in_specs += [pl.BlockSpec((1, tm, a.shape[2]), lambda i, t: (i, t, 0)) for a in acts]
    in_specs += [pl.BlockSpec(w.shape, lambda i, t: (0, 0), **resident) for w in weights]
    in_specs += [
        pl.BlockSpec((d, hid), lambda i, t: (0, 0), **resident),
        pl.BlockSpec((d, hid), lambda i, t: (0, 0), **resident),
        pl.BlockSpec((hid, d), lambda i, t: (0, 0), **resident),
        pl.BlockSpec((1, d), lambda i, t: (0, 0)),
    ]
    return pl.pallas_call(
        kern,
        grid=(b, n_rows // tm),
        in_specs=in_specs,
        out_specs=pl.BlockSpec((1, tm, d), lambda i, t: (i, t, 0)),
        out_shape=jax.ShapeDtypeStruct((b, n_rows, d), F32),
        compiler_params=_params(("arbitrary", "arbitrary")),
        name="mix_ffn_final" if final else "mix_ffn",
    )(xa, mods_l, mods_c, g, *acts, *weights, w_gate, w_up, w_down, final_g)


def _mla_q_kernel(p_ref, g_ref, w_ref, cos_ref, sin_ref, o_ref, *, q_rank, scale):
    cq = p_ref[0, :, :q_rank].astype(F32)
    qn = _rms(cq, g_ref[...]).astype(BF16)
    qq = jnp.dot(qn, w_ref[...], preferred_element_type=F32)
    qq = qq * scale
    n_nope = MLA_HEADS * MLA_NOPE
    ppb = LANES // MLA_ROPE
    npb = LANES // MLA_NOPE
    pes_t = [_rope_block(qq[:, n_nope + c * LANES:n_nope + (c + 1) * LANES], cos_ref[...], sin_ref[...], MLA_ROPE).T
             for c in range(MLA_HEADS // ppb)]
    for c in range(MLA_HEADS // npb):
        nope_t = qq[:, c * LANES:(c + 1) * LANES].T
        for k in range(npb):
            h = c * npb + k
            pe_t = pes_t[h // ppb][(h % ppb) * MLA_ROPE:(h % ppb + 1) * MLA_ROPE, :]
            qh = jnp.concatenate([nope_t[k * MLA_NOPE:(k + 1) * MLA_NOPE, :], pe_t], axis=0)
            o_ref[0, h] = qh.astype(o_ref.dtype)


def _mla_q(p1, g, w_uq, cos_t, sin_t, n_lat, q_rank):
    b = p1.shape[0]
    cols = p1.shape[2]
    tm = _pick(n_lat, (512, 256, 128))
    kern = functools.partial(_mla_q_kernel, q_rank=q_rank, scale=MLA_QK ** -0.5 * LOG2E)
    return pl.pallas_call(
        kern,
        grid=(b, n_lat // tm),
        in_specs=[
            pl.BlockSpec((1, tm, cols), lambda i, t: (i, t, 0)),
            pl.BlockSpec((1, q_rank), lambda i, t: (0, 0)),
            pl.BlockSpec(w_uq.shape, lambda i, t: (0, 0)),
            pl.BlockSpec((tm, LANES), lambda i, t: (t, 0)),
            pl.BlockSpec((tm, LANES), lambda i, t: (t, 0)),
        ],
        out_specs=pl.BlockSpec((1, MLA_HEADS, MLA_QK, tm), lambda i, t: (i, 0, 0, t)),
        out_shape=jax.ShapeDtypeStruct((b, MLA_HEADS, MLA_QK, n_lat), BF16),
        compiler_params=_params(("arbitrary", "arbitrary")),
        name="mla_q",
    )(p1, g, w_uq, cos_t, sin_t)


def _mla_kv_kernel(p_ref, pe_ref, g_ref, w_ref, cos_ref, sin_ref, ko_ref, vo_ref, *, q_rank):
    ckv = p_ref[0, :, q_rank:].astype(F32)
    kvn = _rms(ckv, g_ref[...]).astype(BF16)
    kv = jnp.dot(kvn, w_ref[...], preferred_element_type=F32)
    pe = _rope_block(pe_ref[0], cos_ref[...], sin_ref[...], MLA_ROPE)[:, :MLA_ROPE]
    n_nope = MLA_HEADS * MLA_NOPE
    for h in range(MLA_HEADS):
        kh = jnp.concatenate([kv[:, h * MLA_NOPE:(h + 1) * MLA_NOPE], pe], axis=1)
        ko_ref[0, h] = kh.astype(ko_ref.dtype)
    hpb = LANES // MLA_V
    for c in range(MLA_HEADS // hpb):
        v_t = kv[:, n_nope + c * LANES:n_nope + (c + 1) * LANES].T.astype(vo_ref.dtype)
        for k in range(hpb):
            vo_ref[0, c * hpb + k] = v_t[k * MLA_V:(k + 1) * MLA_V, :]


def _mla_kv(p1, s1, g, w_ukv, cos_t, sin_t, q_rank):
    b, n, cols = p1.shape
    kv_rank = cols - q_rank
    tm = _pick(n, (768, 384, 256, 128))
    kern = functools.partial(_mla_kv_kernel, q_rank=q_rank)
    return pl.pallas_call(
        kern,
        grid=(b, n // tm),
        in_specs=[
            pl.BlockSpec((1, tm, cols), lambda i, t: (i, t, 0)),
            pl.BlockSpec((1, tm, LANES), lambda i, t: (i, t, 0)),
            pl.BlockSpec((1, kv_rank), lambda i, t: (0, 0)),
            pl.BlockSpec(w_ukv.shape, lambda i, t: (0, 0)),
            pl.BlockSpec((tm, LANES), lambda i, t: (t, 0)),
            pl.BlockSpec((tm, LANES), lambda i, t: (t, 0)),
        ],
        out_specs=[
            pl.BlockSpec((1, MLA_HEADS, tm, MLA_QK), lambda i, t: (i, 0, t, 0)),
            pl.BlockSpec((1, MLA_HEADS, MLA_V, tm), lambda i, t: (i, 0, 0, t)),
        ],
        out_shape=[
            jax.ShapeDtypeStruct((b, MLA_HEADS, n, MLA_QK), BF16),
            jax.ShapeDtypeStruct((b, MLA_HEADS, MLA_V, n), BF16),
        ],
        compiler_params=_params(("arbitrary", "arbitrary")),
        name="mla_kv",
    )(p1, s1, g, w_ukv, cos_t, sin_t)


def _pad_lanes(v, width=LANES):
    v = v.reshape(1, -1).astype(F32)
    return jnp.pad(v, ((0, 0), (0, width - v.shape[1])))


def kernel(x, c, ctx, c_ctx, ada_w, ada_b, norm1_g, norm2_g, ffn_w_up, ffn_w_down, ab_w_in, ab_w_out, ssd_conv_w, ssd_conv_b, ssd_a_log, ssd_dt_bias, ssd_d, ssd_norm_g, att_q_g, att_k_g, mla_w_in, mla_q_norm_g, mla_w_uq, mla_kv_norm_g, mla_w_ukv, mla_w_o, final_norm_g):
    b, t_lat, d = x.shape
    n_ctx = ctx.shape[1]
    rows = t_lat // GRID_W
    hid = ffn_w_down.shape[1]
    row2 = lambda v: v.reshape(1, -1).astype(F32)

    bp = -(-(b + 1) // 8) * 8
    cvec = jnp.concatenate([c, c_ctx[None, :], jnp.zeros((bp - b - 1, d), F32)], axis=0)
    mods = _ada_mods(cvec, ada_w, ada_b)
    mods_l = [mods[i, :b].reshape(b, ADA_CHUNKS, d) for i in range(2)]
    mods_c = [mods[i, b].reshape(ADA_CHUNKS, d) for i in range(2)]

    xa = jnp.concatenate([x, ctx], axis=1)

    w_in = ab_w_in[0]
    o_xbc, o_dt = SSD_INNER, SSD_INNER + SSD_CONV_DIM
    o_att = o_dt + 2 * SSD_HEADS
    w_main = jnp.concatenate([w_in[:, o_xbc:o_dt], w_in[:, :SSD_INNER], w_in[:, o_att:]], axis=1).astype(BF16)
    w_dt = jnp.zeros((d, 2 * LANES), F32)
    w_dt = w_dt.at[:, :SSD_HEADS].set(w_in[:, o_dt:o_dt + SSD_HEADS])
    w_dt = w_dt.at[:, LANES:LANES + SSD_HEADS].set(w_in[:, o_dt + SSD_HEADS:o_att]).astype(BF16)
    p0, s0 = _in_proj(xa, mods_l[0], mods_c[0], row2(norm1_g[0]), w_main, w_dt, t_lat)

    conv_w = ssd_conv_w[0].astype(F32)
    conv_b = row2(ssd_conv_b[0])
    yf = _ssd_direction(p0, s0, conv_w, conv_b, _pad_lanes(ssd_dt_bias[0, 0]), _pad_lanes(ssd_a_log[0, 0]),
                        t_lat, reverse=False)
    d_skip = jnp.repeat(ssd_d[0].astype(F32), SSD_HEADDIM).reshape(1, SSD_INNER)
    y_ssd = _ssd_direction(p0, s0, conv_w, conv_b, _pad_lanes(ssd_dt_bias[0, 1]), _pad_lanes(ssd_a_log[0, 1]),
                           t_lat, reverse=True, extra=(yf, d_skip, row2(ssd_norm_g[0])))

    q_col = SSD_CONV_DIM + SSD_INNER
    k_col = q_col + ATT_Q
    cos_a, sin_a = _rope_tables(rows, ATT_HEADDIM, n_ctx)
    hpb = LANES // ATT_HEADDIM
    qh, kh, vh = _qk_prep(p0, cos_a, sin_a, jnp.tile(row2(att_q_g[0]), (1, hpb)),
                          jnp.tile(row2(att_k_g[0]), (1, hpb)), q_col, k_col, k_col + ATT_KV)
    o_att_l0 = _attention(qh, kh, vh, t_lat, ctx_self=True)

    w_out = ab_w_out[0].astype(BF16)
    w_up = ffn_w_up[0].astype(BF16)
    xa = _mix_ffn(xa, mods_l[0], mods_c[0], row2(norm2_g[0]), [y_ssd, o_att_l0],
                  [w_out[:SSD_INNER], w_out[SSD_INNER:]], w_up[:, :hid], w_up[:, hid:],
                  ffn_w_down[0].astype(BF16), row2(final_norm_g), t_lat + n_ctx, t_lat, final=False)

    q_rank = mla_q_norm_g.shape[1]
    kv_rank = mla_kv_norm_g.shape[1]
    w_in1 = mla_w_in[0]
    w_pe = jnp.pad(w_in1[:, q_rank + kv_rank:], ((0, 0), (0, LANES - MLA_ROPE))).astype(BF16)
    p1, s1 = _in_proj(xa, mods_l[1], mods_c[1], row2(norm1_g[1]), w_in1[:, :q_rank + kv_rank].astype(BF16), w_pe,
                      t_lat)
    cos_m, sin_m = _rope_tables(rows, MLA_ROPE, n_ctx)
    w_uq = mla_w_uq[0].reshape(q_rank, MLA_HEADS, MLA_QK)
    w_uq = jnp.concatenate([w_uq[:, :, :MLA_NOPE].reshape(q_rank, -1), w_uq[:, :, MLA_NOPE:].reshape(q_rank, -1)],
                           axis=1).astype(BF16)
    w_ukv = mla_w_ukv[0].reshape(kv_rank, MLA_HEADS, MLA_NOPE + MLA_V)
    w_ukv = jnp.concatenate([w_ukv[:, :, :MLA_NOPE].reshape(kv_rank, -1), w_ukv[:, :, MLA_NOPE:].reshape(kv_rank, -1)],
                            axis=1).astype(BF16)
    qm = _mla_q(p1, row2(mla_q_norm_g[0]), w_uq, cos_m, sin_m, t_lat, q_rank)
    km, vm = _mla_kv(p1, s1, row2(mla_kv_norm_g[0]), w_ukv, cos_m, sin_m, q_rank)
    o_mla = _attention(qm, km, vm, t_lat, ctx_self=False)

    w_up = ffn_w_up[1].astype(BF16)
    return _mix_ffn(xa, mods_l[1], mods_c[1], row2(norm2_g[1]), [o_mla], [mla_w_o[0].astype(BF16)],
                    w_up[:, :hid], w_up[:, hid:], ffn_w_down[1].astype(BF16), row2(final_norm_g),
                    t_lat, t_lat, final=True)
```

```python
import functools
import math

import jax
import jax.numpy as jnp
from jax import lax
from jax.experimental import pallas as pl
from jax.experimental.pallas import tpu as pltpu

F32 = jnp.float32
BF16 = jnp.bfloat16

EPS = 1e-6
ROPE_THETA = 10000.0
GRID_W = 64
ADA_CHUNKS = 6

SSD_HEADS = 16
SSD_HEADDIM = 64
SSD_GROUPS = 4
SSD_STATE = 128
SSD_CHUNK = 128
SSD_INNER = SSD_HEADS * SSD_HEADDIM
SSD_GN = SSD_GROUPS * SSD_STATE
SSD_CONV_DIM = SSD_INNER + 2 * SSD_GN

ATT_HEADS = 16
ATT_KV_HEADS = 4
ATT_HEADDIM = 64
ATT_Q = ATT_HEADS * ATT_HEADDIM
ATT_KV = ATT_KV_HEADS * ATT_HEADDIM

MLA_HEADS = 16
MLA_NOPE = 64
MLA_ROPE = 32
MLA_V = 64
MLA_QK = MLA_NOPE + MLA_ROPE

LANES = 128
SUBLANES = 8
BF16_ROWS = 16
ATTN_BLOCKS_PER_STEP = 8
ATTN_BOUND_SLACK = 1.0 + 2.0 ** -8
ATTN_MAX_SHIFT = 64.0
ATTN_BLOCK_LANES = 512
LOG2E = math.log2(math.e)
VMEM_LIMIT = 56 * 1024 * 1024


def _pick(n, candidates):
    for c in candidates:
        if n % c == 0:
            return c
    raise ValueError(f"no tile for {n} in {candidates}")


def _params(sem):
    return pltpu.CompilerParams(dimension_semantics=sem, vmem_limit_bytes=VMEM_LIMIT)


def _silu(x):
    hx = 0.5 * x
    return hx + hx * jnp.tanh(hx)


def _rms(x, g):
    ms = jnp.mean(x * x, axis=-1, keepdims=True)
    return x * lax.rsqrt(ms + EPS) * g


def _mod_row(ml_ref, mc_ref, idx, is_lat):
    return jnp.where(is_lat, ml_ref[0, idx:idx + 1, :], mc_ref[idx:idx + 1, :])


def _is_lat(t, tm, n_lat):
    rows = t * tm + lax.broadcasted_iota(jnp.int32, (tm, 1), 0)
    return rows < n_lat


def _ada_kernel(c_ref, w_ref, b_ref, o_ref):
    s = _silu(c_ref[...]).astype(BF16)
    o_ref[0] = jnp.dot(s, w_ref[0].astype(BF16), preferred_element_type=F32) + b_ref[0]


def _ada_mods(cvec, ada_w, ada_b):
    depth, d, n = ada_w.shape
    bp = cvec.shape[0]
    tn = _pick(n, (1536, 1024, 512, 256, 128))
    return pl.pallas_call(
        _ada_kernel,
        grid=(depth, n // tn),
        in_specs=[
            pl.BlockSpec((bp, d), lambda i, j: (0, 0)),
            pl.BlockSpec((1, d, tn), lambda i, j: (i, 0, j)),
            pl.BlockSpec((1, 1, tn), lambda i, j: (i, 0, j)),
        ],
        out_specs=pl.BlockSpec((1, bp, tn), lambda i, j: (i, 0, j)),
        out_shape=jax.ShapeDtypeStruct((depth, bp, n), F32),
        compiler_params=_params(("arbitrary", "arbitrary")),
        name="ada_mods",
    )(cvec, ada_w, ada_b.reshape(depth, 1, n))


def _in_proj_kernel(x_ref, ml_ref, mc_ref, g_ref, w_ref, ws_ref, o_ref, os_ref, h_ref, *, tm, n_lat):
    t = pl.program_id(1)
    j = pl.program_id(2)

    @pl.when(j == 0)
    def _():
        rc = _pick(tm, (256, 128))
        for r in range(0, tm, rc):
            is_lat = (t * tm + r + lax.broadcasted_iota(jnp.int32, (rc, 1), 0)) < n_lat
            h = _rms(x_ref[0, r:r + rc, :], g_ref[...])
            h = h * (1.0 + _mod_row(ml_ref, mc_ref, 1, is_lat)) + _mod_row(ml_ref, mc_ref, 0, is_lat)
            hb = h.astype(BF16)
            h_ref[r:r + rc, :] = hb
            os_ref[0, r:r + rc, :] = jnp.dot(hb, ws_ref[...], preferred_element_type=F32)
            o_ref[0, r:r + rc, :] = jnp.dot(hb, w_ref[...], preferred_element_type=F32).astype(o_ref.dtype)

    @pl.when(j > 0)
    def _():
        o_ref[0] = jnp.dot(h_ref[...], w_ref[...], preferred_element_type=F32).astype(o_ref.dtype)


def _in_proj(xa, mods_l, mods_c, g, w_main, w_side, n_lat):
    b, n, d = xa.shape
    nm = w_main.shape[1]
    ns = w_side.shape[1]
    tm = _pick(n, (768, 384, 256, 128))
    tn = _pick(nm, (1536, 1024, 768, 640, 512, 256, 128))
    kern = functools.partial(_in_proj_kernel, tm=tm, n_lat=n_lat)
    return pl.pallas_call(
        kern,
        grid=(b, n // tm, nm // tn),
        in_specs=[
            pl.BlockSpec((1, tm, d), lambda i, t, j: (i, t, 0)),
            pl.BlockSpec((1, ADA_CHUNKS, d), lambda i, t, j: (i, 0, 0)),
            pl.BlockSpec((ADA_CHUNKS, d), lambda i, t, j: (0, 0)),
            pl.BlockSpec((1, d), lambda i, t, j: (0, 0)),
            pl.BlockSpec((d, tn), lambda i, t, j: (0, j)),
            pl.BlockSpec((d, ns), lambda i, t, j: (0, 0)),
        ],
        out_specs=[
            pl.BlockSpec((1, tm, tn), lambda i, t, j: (i, t, j)),
            pl.BlockSpec((1, tm, ns), lambda i, t, j: (i, t, 0)),
        ],
        out_shape=[
            jax.ShapeDtypeStruct((b, n, nm), BF16),
            jax.ShapeDtypeStruct((b, n, ns), F32),
        ],
        scratch_shapes=[pltpu.VMEM((tm, d), BF16)],
        compiler_params=_params(("arbitrary", "arbitrary", "arbitrary")),
        name="in_proj",
    )(xa, mods_l, mods_c, g, w_main, w_side)


def _cumsum_rows(v, reverse):
    n = v.shape[0]
    row = lax.broadcasted_iota(jnp.int32, v.shape, 0)
    k = 1
    while k < n:
        if reverse:
            v = v + jnp.where(row < n - k, pltpu.roll(v, n - k, axis=0), 0.0)
        else:
            v = v + jnp.where(row >= k, pltpu.roll(v, k, axis=0), 0.0)
        k *= 2
    return v


def _ssd_chunk_index(s, nl, nc, reverse):
    if reverse:
        return nc - 1 - s
    return lax.rem(s + nl, nc)


def _ssd_kernel(*refs, reverse, finish, nl, nc):
    if finish:
        (xm_ref, xp_ref, xn_ref, dt_ref, cw_ref, cb_ref, dtb_ref, alog_ref,
         z_ref, yf_ref, dsk_ref, ng_ref, o_ref, st_ref) = refs
    else:
        (xm_ref, xp_ref, xn_ref, dt_ref, cw_ref, cb_ref, dtb_ref, alog_ref, o_ref, st_ref) = refs
    q = SSD_CHUNK
    p = SSD_HEADDIM
    hpg = SSD_HEADS // SSD_GROUPS
    gw = hpg * p
    s = pl.program_id(1)
    chunk = _ssd_chunk_index(s, nl, nc, reverse)

    @pl.when(s == 0)
    def _():
        st_ref[...] = jnp.zeros_like(st_ref)

    xmb = xm_ref[0]
    xm = xmb.astype(F32)
    halo = xp_ref.shape[1]
    seq_first = jnp.logical_or(chunk == 0, chunk == nl)
    seq_last = jnp.logical_or(chunk == nl - 1, chunk == nc - 1)
    prev_row = xp_ref[0, halo - 1:halo, :].astype(F32) * jnp.where(seq_first, 0.0, 1.0)
    next_row = xn_ref[0, 0:1, :].astype(F32) * jnp.where(seq_last, 0.0, 1.0)
    ii = lax.broadcasted_iota(jnp.int32, (q, q), 0)
    jj = lax.broadcasted_iota(jnp.int32, (q, q), 1)
    shift_dn = jnp.where(ii == jj + 1, 1.0, 0.0).astype(BF16)
    shift_up = jnp.where(ii + 1 == jj, 1.0, 0.0).astype(BF16)
    x_prev = jnp.dot(shift_dn, xmb, preferred_element_type=F32)
    x_next = jnp.dot(shift_up, xmb, preferred_element_type=F32)
    row8 = lax.broadcasted_iota(jnp.int32, (SUBLANES, 1), 0)
    x_prev = jnp.concatenate(
        [x_prev[:SUBLANES] + jnp.where(row8 == 0, prev_row, 0.0), x_prev[SUBLANES:]], axis=0)
    x_next = jnp.concatenate(
        [x_next[:q - SUBLANES], x_next[q - SUBLANES:] + jnp.where(row8 == SUBLANES - 1, next_row, 0.0)], axis=0)
    xc = x_prev * cw_ref[0:1, :] + xm * cw_ref[1:2, :] + x_next * cw_ref[2:3, :] + cb_ref[...]
    act = _silu(xc)
    xs = act[:, :SSD_INNER]
    xsb = xs.astype(BF16)
    bm = act[:, SSD_INNER:SSD_INNER + SSD_GN]
    cm = act[:, SSD_INNER + SSD_GN:]

    dtr = dt_ref[0] + dtb_ref[...]
    dt = jnp.maximum(dtr, 0.0) + jnp.log(1.0 + jnp.exp(-jnp.abs(dtr)))
    a = -jnp.exp(alog_ref[...])
    acs = _cumsum_rows(dt * a, reverse)
    acs_t = acs.T
    dt_t = dt.T
    edge = acs[0:1, :] if reverse else acs[q - 1:q, :]
    dtd = dt * jnp.exp(edge - acs)
    cdec = jnp.exp(edge)
    mask = (ii <= jj) if reverse else (ii >= jj)
    lane_h = lax.broadcasted_iota(jnp.int32, (q, LANES), 1)
    lane_g = lax.broadcasted_iota(jnp.int32, (q, gw), 1)

    def expand(v, g):
        rows = v.shape[0]
        parts = []
        for k in range(0, hpg, LANES // p):
            h0 = g * hpg + k
            lo = jnp.broadcast_to(v[:, h0:h0 + 1], (rows, LANES))
            hi = jnp.broadcast_to(v[:, h0 + 1:h0 + 2], (rows, LANES))
            parts.append(jnp.where(lane_h[:rows] < p, lo, hi))
        return jnp.concatenate(parts, axis=1)

    ys = []
    for g in range(SSD_GROUPS):
        cg32 = cm[:, g * SSD_STATE:(g + 1) * SSD_STATE]
        bg32 = bm[:, g * SSD_STATE:(g + 1) * SSD_STATE]
        cb = lax.dot_general(cg32.astype(BF16), bg32.astype(BF16), (((1,), (1,)), ((), ())),
                             preferred_element_type=F32)
        st = st_ref[g]
        xg = xs[:, g * gw:(g + 1) * gw]
        rhs = jnp.concatenate([xsb[:, g * gw:(g + 1) * gw], st.astype(BF16)], axis=0)
        y_g = None
        for k in reversed(range(hpg)):
            h = g * hpg + k
            a_col = jnp.broadcast_to(acs[:, h:h + 1], (q, q))
            seg = a_col - jnp.broadcast_to(acs_t[h:h + 1, :], (q, q))
            lmat = jnp.exp(jnp.where(mask, seg, -jnp.inf))
            m_h = (cb * lmat * jnp.broadcast_to(dt_t[h:h + 1, :], (q, q))).astype(BF16)
            c_h = (cg32 * jnp.exp(a_col)).astype(BF16)
            res = jnp.dot(jnp.concatenate([m_h, c_h], axis=1), rhs, preferred_element_type=F32)
            y_g = res if y_g is None else jnp.where(lane_g < (k + 1) * p, res, y_g)
        ys.append(y_g)
        xdd = (xg * expand(dtd, g)).astype(BF16)
        new = jnp.dot(bg32.T.astype(BF16), xdd, preferred_element_type=F32)
        st_ref[g] = st * expand(cdec, g) + new
    y = jnp.concatenate(ys, axis=1)

    if finish:
        y = yf_ref[0] + y + xs * dsk_ref[...]
        y = y * _silu(z_ref[0].astype(F32))
        o_ref[0] = _rms(y, ng_ref[...]).astype(o_ref.dtype)
    else:
        o_ref[0] = y


def _ssd_direction(p0, s0, conv_w, conv_b, dt_bias, a_log, n_lat, reverse, extra=None):
    b, n, _ = p0.shape
    q = SSD_CHUNK
    nc = n // q
    nl = n_lat // q
    halo = 16
    hpc = q // halo
    nh = n // halo
    d = 1 if reverse else 0
    cidx = functools.partial(_ssd_chunk_index, nl=nl, nc=nc, reverse=reverse)
    finish = extra is not None
    in_specs = [
        pl.BlockSpec((1, q, SSD_CONV_DIM), lambda i, s: (i, cidx(s), 0)),
        pl.BlockSpec((1, halo, SSD_CONV_DIM), lambda i, s: (i, jnp.maximum(cidx(s) * hpc - 1, 0), 0)),
        pl.BlockSpec((1, halo, SSD_CONV_DIM), lambda i, s: (i, jnp.minimum(cidx(s) * hpc + hpc, nh - 1), 0)),
        pl.BlockSpec((1, q, LANES), lambda i, s: (i, cidx(s), d)),
        pl.BlockSpec((3, SSD_CONV_DIM), lambda i, s: (0, 0)),
        pl.BlockSpec((1, SSD_CONV_DIM), lambda i, s: (0, 0)),
        pl.BlockSpec((1, LANES), lambda i, s: (0, 0)),
        pl.BlockSpec((1, LANES), lambda i, s: (0, 0)),
    ]
    args = [p0, p0, p0, s0, conv_w, conv_b, dt_bias, a_log]
    if finish:
        yf, d_skip, norm_g = extra
        in_specs += [
            pl.BlockSpec((1, q, SSD_INNER), lambda i, s: (i, cidx(s), SSD_CONV_DIM // SSD_INNER)),
            pl.BlockSpec((1, q, SSD_INNER), lambda i, s: (i, cidx(s), 0)),
            pl.BlockSpec((1, SSD_INNER), lambda i, s: (0, 0)),
            pl.BlockSpec((1, SSD_INNER), lambda i, s: (0, 0)),
        ]
        args += [p0, yf, d_skip, norm_g]
    kern = functools.partial(_ssd_kernel, reverse=reverse, finish=finish, nl=nl, nc=nc)
    return pl.pallas_call(
        kern,
        grid=(b, nc),
        in_specs=in_specs,
        out_specs=pl.BlockSpec((1, q, SSD_INNER), lambda i, s: (i, cidx(s), 0)),
        out_shape=jax.ShapeDtypeStruct((b, n, SSD_INNER), BF16 if finish else F32),
        scratch_shapes=[pltpu.VMEM((SSD_GROUPS, SSD_STATE, (SSD_HEADS // SSD_GROUPS) * SSD_HEADDIM), F32)],
        compiler_params=_params(("arbitrary", "arbitrary")),
        name="ssd_bwd_finish" if finish else "ssd_fwd",
    )(*args)


def _rope_tables(rows, rot_dim, n_ctx):
    n_freq = rot_dim // 4
    row = jnp.repeat(jnp.arange(rows, dtype=F32), GRID_W)
    col = jnp.tile(jnp.arange(GRID_W, dtype=F32), rows)
    inv = ROPE_THETA ** (-jnp.arange(n_freq, dtype=F32) / n_freq)
    ang = jnp.concatenate([row[:, None] * inv, col[:, None] * inv], axis=-1)
    cos, sin = jnp.cos(ang), jnp.sin(ang)
    reps = LANES // rot_dim
    cos_t = jnp.tile(jnp.concatenate([cos, cos], axis=-1), (1, reps))
    sin_t = jnp.tile(jnp.concatenate([-sin, sin], axis=-1), (1, reps))
    cos_t = jnp.concatenate([cos_t, jnp.ones((n_ctx, LANES), F32)], axis=0)
    sin_t = jnp.concatenate([sin_t, jnp.zeros((n_ctx, LANES), F32)], axis=0)
    return cos_t, sin_t


def _rope_block(xb, cos, sin, rot_dim):
    half = rot_dim // 2
    lane = lax.broadcasted_iota(jnp.int32, xb.shape, 1)
    first = jnp.bitwise_and(lane, rot_dim - 1) < half
    partner = jnp.where(first, pltpu.roll(xb, LANES - half, axis=1), pltpu.roll(xb, half, axis=1))
    return xb * cos + partner * sin


def _segment_mean_matrix(seg):
    sh = seg.bit_length() - 1
    i = jnp.right_shift(lax.broadcasted_iota(jnp.int32, (LANES, LANES), 0), sh)
    j = jnp.right_shift(lax.broadcasted_iota(jnp.int32, (LANES, LANES), 1), sh)
    return jnp.where(i == j, 1.0 / seg, 0.0).astype(BF16)


def _segment_mean_sq(xb, smat):
    x2 = xb * xb
    hi = x2.astype(BF16)
    lo = (x2 - hi.astype(F32)).astype(BF16)
    return (jnp.dot(hi, smat, preferred_element_type=F32) + jnp.dot(lo, smat, preferred_element_type=F32))


def _qk_prep_kernel(q_ref, k_ref, v_ref, cos_ref, sin_ref, qg_ref, kg_ref, qo_ref, ko_ref, vo_ref, *, scale):
    smat = _segment_mean_matrix(ATT_HEADDIM)
    cos = cos_ref[...]
    sin = sin_ref[...]
    hpb = LANES // ATT_HEADDIM

    def prep(src_ref, g_ref, dst_ref, n_heads, mul, transposed):
        for c in range(n_heads // hpb):
            xb = src_ref[0, :, c * LANES:(c + 1) * LANES].astype(F32)
            xn = xb * lax.rsqrt(_segment_mean_sq(xb, smat) + EPS) * g_ref[...]
            r = _rope_block(xn, cos, sin, ATT_HEADDIM)
            if mul != 1.0:
                r = r * mul
            if transposed:
                r = r.T
            r = r.astype(dst_ref.dtype)
            for k in range(hpb):
                if transposed:
                    dst_ref[0, c * hpb + k] = r[k * ATT_HEADDIM:(k + 1) * ATT_HEADDIM, :]
                else:
                    dst_ref[0, c * hpb + k] = r[:, k * ATT_HEADDIM:(k + 1) * ATT_HEADDIM]

    prep(q_ref, qg_ref, qo_ref, ATT_HEADS, scale, True)
    prep(k_ref, kg_ref, ko_ref, ATT_KV_HEADS, 1.0, False)
    for c in range(ATT_KV_HEADS // hpb):
        vt = v_ref[0, :, c * LANES:(c + 1) * LANES].astype(F32).T.astype(vo_ref.dtype)
        for k in range(hpb):
            vo_ref[0, c * hpb + k] = vt[k * ATT_HEADDIM:(k + 1) * ATT_HEADDIM, :]


def _qk_prep(p0, cos_t, sin_t, q_g, k_g, q_col, k_col, v_col):
    b, n, _ = p0.shape
    tm = _pick(n, (768, 384, 256, 128))
    kern = functools.partial(_qk_prep_kernel, scale=ATT_HEADDIM ** -0.5 * LOG2E)
    return pl.pallas_call(
        kern,
        grid=(b, n // tm),
        in_specs=[
            pl.BlockSpec((1, tm, ATT_Q), lambda i, t: (i, t, q_col // ATT_Q)),
            pl.BlockSpec((1, tm, ATT_KV), lambda i, t: (i, t, k_col // ATT_KV)),
            pl.BlockSpec((1, tm, ATT_KV), lambda i, t: (i, t, v_col // ATT_KV)),
            pl.BlockSpec((tm, LANES), lambda i, t: (t, 0)),
            pl.BlockSpec((tm, LANES), lambda i, t: (t, 0)),
            pl.BlockSpec((1, LANES), lambda i, t: (0, 0)),
            pl.BlockSpec((1, LANES), lambda i, t: (0, 0)),
        ],
        out_specs=[
            pl.BlockSpec((1, ATT_HEADS, ATT_HEADDIM, tm), lambda i, t: (i, 0, 0, t)),
            pl.BlockSpec((1, ATT_KV_HEADS, tm, ATT_HEADDIM), lambda i, t: (i, 0, t, 0)),
            pl.BlockSpec((1, ATT_KV_HEADS, ATT_HEADDIM, tm), lambda i, t: (i, 0, 0, t)),
        ],
        out_shape=[
            jax.ShapeDtypeStruct((b, ATT_HEADS, ATT_HEADDIM, n), BF16),
            jax.ShapeDtypeStruct((b, ATT_KV_HEADS, n, ATT_HEADDIM), BF16),
            jax.ShapeDtypeStruct((b, ATT_KV_HEADS, ATT_HEADDIM, n), BF16),
        ],
        compiler_params=_params(("arbitrary", "arbitrary")),
        name="gqa_qk_prep",
    )(p0, p0, p0, cos_t, sin_t, q_g, k_g)


def _attn_kernel(qt_ref, k_ref, vt_ref, o_ref, s_scr, m_scr, e_scr, gap_scr, ksq_scr, ot_scr, *,
                 group, hpb, n_lat, n_all, lat_tiles, ctx_self):
    t = pl.program_id(2)
    dv = vt_ref.shape[2]
    tq = qt_ref.shape[3]
    n_blocks = qt_ref.shape[1] // hpb

    @pl.when(t == 0)
    def _():
        for kv in range(k_ref.shape[1]):
            kf = k_ref[0, kv].astype(F32)
            ksq = jnp.max(jnp.sum(kf * kf, axis=1, keepdims=True), axis=0, keepdims=True)
            ksq_scr[kv] = jnp.broadcast_to(ksq, (1, LANES))

    def queries(i):
        return jnp.concatenate([qt_ref[0, i * hpb + j] for j in range(hpb)], axis=1)

    def finish(i, ot):
        ot = ot[:dv] * (1.0 / ot[dv:dv + 1])
        for j in range(hpb):
            ot_scr[(i * hpb + j) * dv:(i * hpb + j + 1) * dv, :] = ot[:, j * tq:(j + 1) * tq]

    def run(k_lo, k_len, exact):
        row = lax.broadcasted_iota(jnp.int32, (BF16_ROWS, k_len), 0)
        ones_row = jnp.where(row == 0, 1.0, 0.0).astype(BF16)

        def scores(i):
            kv = i * hpb // group
            qt = queries(i)
            s = jnp.dot(k_ref[0, kv, k_lo:k_lo + k_len, :], qt, preferred_element_type=F32)
            m = jnp.max(s, axis=0, keepdims=True)
            if exact:
                s_scr[i % 2, :k_len, :] = s
                m_scr[i % 2] = m
            else:
                qf = qt.astype(F32)
                bound = jnp.sqrt(jnp.sum(qf * qf, axis=0, keepdims=True) * ksq_scr[kv][:, :1]) * ATTN_BOUND_SLACK
                e_scr[i % 2, :k_len, :] = jnp.exp2(s - bound).astype(BF16)
                gap_scr[i] = bound - m

        def combine(i):
            if exact:
                e = jnp.exp2(s_scr[i % 2, :k_len, :] - m_scr[i % 2]).astype(BF16)
            else:
                e = e_scr[i % 2, :k_len, :]
            vt = vt_ref[0, i * hpb // group, :, k_lo:k_lo + k_len]
            finish(i, jnp.dot(jnp.concatenate([vt, ones_row], axis=0), e, preferred_element_type=F32))

        scores(0)
        for i in range(n_blocks):
            if i + 1 < n_blocks:
                scores(i + 1)
            combine(i)

    def attend(k_lo, k_len):
        run(k_lo, k_len, exact=False)

        @pl.when(jnp.max(gap_scr[...]) > ATTN_MAX_SHIFT)
        def _():
            run(k_lo, k_len, exact=True)

        o_ref[0] = ot_scr[...].T.astype(o_ref.dtype)

    if ctx_self:
        @pl.when(t < lat_tiles)
        def _():
            attend(0, n_all)

        @pl.when(t >= lat_tiles)
        def _():
            attend(n_lat, n_all - n_lat)
    else:
        attend(0, n_all)


def _attention(qt, k, vt, n_lat, ctx_self):
    b, h, dqk, nq = qt.shape
    hkv, dv, n_all = vt.shape[1], vt.shape[2], vt.shape[3]
    group = h // hkv
    if ctx_self:
        tq = _pick(math.gcd(n_lat, n_all - n_lat), (256, 128))
    else:
        tq = _pick(n_lat, (ATTN_BLOCK_LANES, 256, 128))
    hpb = max(1, min(group, ATTN_BLOCK_LANES // tq))
    heads = min(h, ATTN_BLOCKS_PER_STEP * hpb)
    kvb = max(heads // group, 1)
    assert nq == (n_all if ctx_self else n_lat)
    kern = functools.partial(_attn_kernel, group=group, hpb=hpb, n_lat=n_lat, n_all=n_all,
                             lat_tiles=n_lat // tq, ctx_self=ctx_self)
    return pl.pallas_call(
        kern,
        grid=(b, h // heads, nq // tq),
        in_specs=[
            pl.BlockSpec((1, heads, dqk, tq), lambda i, hb, t: (i, hb, 0, t)),
            pl.BlockSpec((1, kvb, n_all, dqk), lambda i, hb, t: (i, hb, 0, 0)),
            pl.BlockSpec((1, kvb, dv, n_all), lambda i, hb, t: (i, hb, 0, 0)),
        ],
        out_specs=pl.BlockSpec((1, tq, heads * dv), lambda i, hb, t: (i, t, hb)),
        out_shape=jax.ShapeDtypeStruct((b, nq, h * dv), BF16),
        scratch_shapes=[pltpu.VMEM((2, n_all, tq * hpb), F32), pltpu.VMEM((2, 1, tq * hpb), F32),
                        pltpu.VMEM((2, n_all, tq * hpb), BF16), pltpu.VMEM((heads // hpb, 1, tq * hpb), F32),
                        pltpu.VMEM((kvb, 1, LANES), F32), pltpu.VMEM((heads * dv, tq), F32)],
        compiler_params=_params(("arbitrary", "arbitrary", "arbitrary")),
        name="attention",
    )(qt, k, vt)


def _hidden_chunks(hidden):
    chunks, lo = [], 0
    while lo < hidden:
        w = min(1024, hidden - lo)
        chunks.append((lo, w))
        lo += w
    return chunks


def _mix_ffn_kernel(*refs, n_in, tm, n_lat, final):
    x_ref, ml_ref, mc_ref, g_ref = refs[:4]
    a_refs = refs[4:4 + n_in]
    w_refs = refs[4 + n_in:4 + 2 * n_in]
    wg_ref, wu_ref, wd_ref, fg_ref, o_ref = refs[4 + 2 * n_in:]
    t = pl.program_id(1)
    is_lat = _is_lat(t, tm, n_lat)
    mod = functools.partial(_mod_row, ml_ref, mc_ref, is_lat=is_lat)
    mix = jnp.dot(a_refs[0][0], w_refs[0][...], preferred_element_type=F32)
    for a_ref, w_ref in zip(a_refs[1:], w_refs[1:]):
        mix = mix + jnp.dot(a_ref[0], w_ref[...], preferred_element_type=F32)
    x = x_ref[0] + mod(2) * mix
    h = (_rms(x, g_ref[...]) * (1.0 + mod(4)) + mod(3)).astype(BF16)
    acc = None
    for lo, w in _hidden_chunks(wd_ref.shape[0]):
        gate = jnp.dot(h, wg_ref[:, lo:lo + w], preferred_element_type=F32)
        up = jnp.dot(h, wu_ref[:, lo:lo + w], preferred_element_type=F32)
        a = (_silu(gate) * up).astype(BF16)
        part = jnp.dot(a, wd_ref[lo:lo + w, :], preferred_element_type=F32)
        acc = part if acc is None else acc + part
    y = x + mod(5) * acc
    if final:
        y = _rms(y, fg_ref[...])
    o_ref[0] = y


def _mix_ffn(xa, mods_l, mods_c, g, acts, weights, w_gate, w_up, w_down, final_g, n_rows, n_lat, final):
    b, _, d = xa.shape
    hid = w_down.shape[0]
    tm = _pick(n_rows, (512, 384, 256, 128))
    n_in = len(acts)
    kern = functools.partial(_mix_ffn_kernel, n_in=n_in, tm=tm, n_lat=n_lat, final=final)
    resident = dict(pipeline_mode=pl.Buffered(1))
    in_specs = [
        pl.BlockSpec((1, tm, d), lambda i, t: (i, t, 0)),
        pl.BlockSpec((1, ADA_CHUNKS, d), lambda i, t: (i, 0, 0)),
        pl.BlockSpec((ADA_CHUNKS, d), lambda i, t: (0, 0)),
        pl.BlockSpec((1, d), lambda i, t: (0, 0)),
    ]
    in_specs += [pl.BlockSpec((1, tm, a.shape[2]), lambda i, t: (i, t, 0)) for a in acts]
    in_specs += [pl.BlockSpec(w.shape, lambda i, t: (0, 0), **resident) for w in weights]
    in_specs += [
        pl.BlockSpec((d, hid), lambda i, t: (0, 0), **resident),
        pl.BlockSpec((d, hid), lambda i, t: (0, 0), **resident),
        pl.BlockSpec((hid, d), lambda i, t: (0, 0), **resident),
        pl.BlockSpec((1, d), lambda i, t: (0, 0)),
    ]
    return pl.pallas_call(
        kern,
        grid=(b, n_rows // tm),
        in_specs=in_specs,
        out_specs=pl.BlockSpec((1, tm, d), lambda i, t: (i, t, 0)),
        out_shape=jax.ShapeDtypeStruct((b, n_rows, d), F32),
        compiler_params=_params(("arbitrary", "arbitrary")),
        name="mix_ffn_final" if final else "mix_ffn",
    )(xa, mods_l, mods_c, g, *acts, *weights, w_gate, w_up, w_down, final_g)


def _mla_q_kernel(p_ref, g_ref, w_ref, cos_ref, sin_ref, o_ref, *, q_rank, scale):
    cq = p_ref[0, :, :q_rank].astype(F32)
    qn = _rms(cq, g_ref[...]).astype(BF16)
    qq = jnp.dot(qn, w_ref[...], preferred_element_type=F32)
    qq = qq * scale
    n_nope = MLA_HEADS * MLA_NOPE
    ppb = LANES // MLA_ROPE
    npb = LANES // MLA_NOPE
    pes_t = [_rope_block(qq[:, n_nope + c * LANES:n_nope + (c + 1) * LANES], cos_ref[...], sin_ref[...], MLA_ROPE).T
             for c in range(MLA_HEADS // ppb)]
    for c in range(MLA_HEADS // npb):
        nope_t = qq[:, c * LANES:(c + 1) * LANES].T
        for k in range(npb):
            h = c * npb + k
            pe_t = pes_t[h // ppb][(h % ppb) * MLA_ROPE:(h % ppb + 1) * MLA_ROPE, :]
            qh = jnp.concatenate([nope_t[k * MLA_NOPE:(k + 1) * MLA_NOPE, :], pe_t], axis=0)
            o_ref[0, h] = qh.astype(o_ref.dtype)


def _mla_q(p1, g, w_uq, cos_t, sin_t, n_lat, q_rank):
    b = p1.shape[0]
    cols = p1.shape[2]
    tm = _pick(n_lat, (512, 256, 128))
    kern = functools.partial(_mla_q_kernel, q_rank=q_rank, scale=MLA_QK ** -0.5 * LOG2E)
    return pl.pallas_call(
        kern,
        grid=(b, n_lat // tm),
        in_specs=[
            pl.BlockSpec((1, tm, cols), lambda i, t: (i, t, 0)),
            pl.BlockSpec((1, q_rank), lambda i, t: (0, 0)),
            pl.BlockSpec(w_uq.shape, lambda i, t: (0, 0)),
            pl.BlockSpec((tm, LANES), lambda i, t: (t, 0)),
            pl.BlockSpec((tm, LANES), lambda i, t: (t, 0)),
        ],
        out_specs=pl.BlockSpec((1, MLA_HEADS, MLA_QK, tm), lambda i, t: (i, 0, 0, t)),
        out_shape=jax.ShapeDtypeStruct((b, MLA_HEADS, MLA_QK, n_lat), BF16),
        compiler_params=_params(("arbitrary", "arbitrary")),
        name="mla_q",
    )(p1, g, w_uq, cos_t, sin_t)


def _mla_kv_kernel(p_ref, pe_ref, g_ref, w_ref, cos_ref, sin_ref, ko_ref, vo_ref, *, q_rank):
    ckv = p_ref[0, :, q_rank:].astype(F32)
    kvn = _rms(ckv, g_ref[...]).astype(BF16)
    kv = jnp.dot(kvn, w_ref[...], preferred_element_type=F32)
    pe = _rope_block(pe_ref[0], cos_ref[...], sin_ref[...], MLA_ROPE)[:, :MLA_ROPE]
    n_nope = MLA_HEADS * MLA_NOPE
    for h in range(MLA_HEADS):
        kh = jnp.concatenate([kv[:, h * MLA_NOPE:(h + 1) * MLA_NOPE], pe], axis=1)
        ko_ref[0, h] = kh.astype(ko_ref.dtype)
    hpb = LANES // MLA_V
    for c in range(MLA_HEADS // hpb):
        v_t = kv[:, n_nope + c * LANES:n_nope + (c + 1) * LANES].T.astype(vo_ref.dtype)
        for k in range(hpb):
            vo_ref[0, c * hpb + k] = v_t[k * MLA_V:(k + 1) * MLA_V, :]


def _mla_kv(p1, s1, g, w_ukv, cos_t, sin_t, q_rank):
    b, n, cols = p1.shape
    kv_rank = cols - q_rank
    tm = _pick(n, (768, 384, 256, 128))
    kern = functools.partial(_mla_kv_kernel, q_rank=q_rank)
    return pl.pallas_call(
        kern,
        grid=(b, n // tm),
        in_specs=[
            pl.BlockSpec((1, tm, cols), lambda i, t: (i, t, 0)),
            pl.BlockSpec((1, tm, LANES), lambda i, t: (i, t, 0)),
            pl.BlockSpec((1, kv_rank), lambda i, t: (0, 0)),
            pl.BlockSpec(w_ukv.shape, lambda i, t: (0, 0)),
            pl.BlockSpec((tm, LANES), lambda i, t: (t, 0)),
            pl.BlockSpec((tm, LANES), lambda i, t: (t, 0)),
        ],
        out_specs=[
            pl.BlockSpec((1, MLA_HEADS, tm, MLA_QK), lambda i, t: (i, 0, t, 0)),
            pl.BlockSpec((1, MLA_HEADS, MLA_V, tm), lambda i, t: (i, 0, 0, t)),
        ],
        out_shape=[
            jax.ShapeDtypeStruct((b, MLA_HEADS, n, MLA_QK), BF16),
            jax.ShapeDtypeStruct((b, MLA_HEADS, MLA_V, n), BF16),
        ],
        compiler_params=_params(("arbitrary", "arbitrary")),
        name="mla_kv",
    )(p1, s1, g, w_ukv, cos_t, sin_t)


def _pad_lanes(v, width=LANES):
    v = v.reshape(1, -1).astype(F32)
    return jnp.pad(v, ((0, 0), (0, width - v.shape[1])))


def kernel(x, c, ctx, c_ctx, ada_w, ada_b, norm1_g, norm2_g, ffn_w_up, ffn_w_down, ab_w_in, ab_w_out, ssd_conv_w, ssd_conv_b, ssd_a_log, ssd_dt_bias, ssd_d, ssd_norm_g, att_q_g, att_k_g, mla_w_in, mla_q_norm_g, mla_w_uq, mla_kv_norm_g, mla_w_ukv, mla_w_o, final_norm_g):
    b, t_lat, d = x.shape
    n_ctx = ctx.shape[1]
    rows = t_lat // GRID_W
    hid = ffn_w_down.shape[1]
    row2 = lambda v: v.reshape(1, -1).astype(F32)

    bp = -(-(b + 1) // 8) * 8
    cvec = jnp.concatenate([c, c_ctx[None, :], jnp.zeros((bp - b - 1, d), F32)], axis=0)
    mods = _ada_mods(cvec, ada_w, ada_b)
    mods_l = [mods[i, :b].reshape(b, ADA_CHUNKS, d) for i in range(2)]
    mods_c = [mods[i, b].reshape(ADA_CHUNKS, d) for i in range(2)]

    xa = jnp.concatenate([x, ctx], axis=1)

    w_in = ab_w_in[0]
    o_xbc, o_dt = SSD_INNER, SSD_INNER + SSD_CONV_DIM
    o_att = o_dt + 2 * SSD_HEADS
    w_main = jnp.concatenate([w_in[:, o_xbc:o_dt], w_in[:, :SSD_INNER], w_in[:, o_att:]], axis=1).astype(BF16)
    w_dt = jnp.zeros((d, 2 * LANES), F32)
    w_dt = w_dt.at[:, :SSD_HEADS].set(w_in[:, o_dt:o_dt + SSD_HEADS])
    w_dt = w_dt.at[:, LANES:LANES + SSD_HEADS].set(w_in[:, o_dt + SSD_HEADS:o_att]).astype(BF16)
    p0, s0 = _in_proj(xa, mods_l[0], mods_c[0], row2(norm1_g[0]), w_main, w_dt, t_lat)

    conv_w = ssd_conv_w[0].astype(F32)
    conv_b = row2(ssd_conv_b[0])
    yf = _ssd_direction(p0, s0, conv_w, conv_b, _pad_lanes(ssd_dt_bias[0, 0]), _pad_lanes(ssd_a_log[0, 0]),
                        t_lat, reverse=False)
    d_skip = jnp.repeat(ssd_d[0].astype(F32), SSD_HEADDIM).reshape(1, SSD_INNER)
    y_ssd = _ssd_direction(p0, s0, conv_w, conv_b, _pad_lanes(ssd_dt_bias[0, 1]), _pad_lanes(ssd_a_log[0, 1]),
                           t_lat, reverse=True, extra=(yf, d_skip, row2(ssd_norm_g[0])))

    q_col = SSD_CONV_DIM + SSD_INNER
    k_col = q_col + ATT_Q
    cos_a, sin_a = _rope_tables(rows, ATT_HEADDIM, n_ctx)
    hpb = LANES // ATT_HEADDIM
    qh, kh, vh = _qk_prep(p0, cos_a, sin_a, jnp.tile(row2(att_q_g[0]), (1, hpb)),
                          jnp.tile(row2(att_k_g[0]), (1, hpb)), q_col, k_col, k_col + ATT_KV)
    o_att_l0 = _attention(qh, kh, vh, t_lat, ctx_self=True)

    w_out = ab_w_out[0].astype(BF16)
    w_up = ffn_w_up[0].astype(BF16)
    xa = _mix_ffn(xa, mods_l[0], mods_c[0], row2(norm2_g[0]), [y_ssd, o_att_l0],
                  [w_out[:SSD_INNER], w_out[SSD_INNER:]], w_up[:, :hid], w_up[:, hid:],
                  ffn_w_down[0].astype(BF16), row2(final_norm_g), t_lat + n_ctx, t_lat, final=False)

    q_rank = mla_q_norm_g.shape[1]
    kv_rank = mla_kv_norm_g.shape[1]
    w_in1 = mla_w_in[0]
    w_pe = jnp.pad(w_in1[:, q_rank + kv_rank:], ((0, 0), (0, LANES - MLA_ROPE))).astype(BF16)
    p1, s1 = _in_proj(xa, mods_l[1], mods_c[1], row2(norm1_g[1]), w_in1[:, :q_rank + kv_rank].astype(BF16), w_pe,
                      t_lat)
    cos_m, sin_m = _rope_tables(rows, MLA_ROPE, n_ctx)
    w_uq = mla_w_uq[0].reshape(q_rank, MLA_HEADS, MLA_QK)
    w_uq = jnp.concatenate([w_uq[:, :, :MLA_NOPE].reshape(q_rank, -1), w_uq[:, :, MLA_NOPE:].reshape(q_rank, -1)],
                           axis=1).astype(BF16)
    w_ukv = mla_w_ukv[0].reshape(kv_rank, MLA_HEADS, MLA_NOPE + MLA_V)
    w_ukv = jnp.concatenate([w_ukv[:, :, :MLA_NOPE].reshape(kv_rank, -1), w_ukv[:, :, MLA_NOPE:].reshape(kv_rank, -1)],
                            axis=1).astype(BF16)
    qm = _mla_q(p1, row2(mla_q_norm_g[0]), w_uq, cos_m, sin_m, t_lat, q_rank)
    km, vm = _mla_kv(p1, s1, row2(mla_kv_norm_g[0]), w_ukv, cos_m, sin_m, q_rank)
    o_mla = _attention(qm, km, vm, t_lat, ctx_self=False)

    w_up = ffn_w_up[1].astype(BF16)
    return _mix_ffn(xa, mods_l[1], mods_c[1], row2(norm2_g[1]), [o_mla], [mla_w_o[0].astype(BF16)],
                    w_up[:, :hid], w_up[:, hid:], ffn_w_down[1].astype(BF16), row2(final_norm_g),
                    t_lat, t_lat, final=True)
```

```python
import functools
import math

import jax
import jax.numpy as jnp
from jax import lax
from jax.experimental import pallas as pl
from jax.experimental.pallas import tpu as pltpu

F32 = jnp.float32
BF16 = jnp.bfloat16

EPS = 1e-6
ROPE_THETA = 10000.0
GRID_W = 64
ADA_CHUNKS = 6

SSD_HEADS = 16
SSD_HEADDIM = 64
SSD_GROUPS = 4
SSD_STATE = 128
SSD_CHUNK = 128
SSD_INNER = SSD_HEADS * SSD_HEADDIM
SSD_GN = SSD_GROUPS * SSD_STATE
SSD_CONV_DIM = SSD_INNER + 2 * SSD_GN

ATT_HEADS = 16
ATT_KV_HEADS = 4
ATT_HEADDIM = 64
ATT_Q = ATT_HEADS * ATT_HEADDIM
ATT_KV = ATT_KV_HEADS * ATT_HEADDIM

MLA_HEADS = 16
MLA_NOPE = 64
MLA_ROPE = 32
MLA_V = 64
MLA_QK = MLA_NOPE + MLA_ROPE

LANES = 128
SUBLANES = 8
BF16_ROWS = 16
ATTN_BLOCKS_PER_STEP = 8
ATTN_BOUND_SLACK = 1.0 + 2.0 ** -6
ATTN_MAX_SHIFT = 64.0
ATTN_BLOCK_LANES = 512
LOG2E = math.log2(math.e)
VMEM_LIMIT = 56 * 1024 * 1024


def _pick(n, candidates):
    for c in candidates:
        if n % c == 0:
            return c
    raise ValueError(f"no tile for {n} in {candidates}")


def _params(sem):
    return pltpu.CompilerParams(dimension_semantics=sem, vmem_limit_bytes=VMEM_LIMIT)


def _silu(x):
    hx = 0.5 * x
    return hx + hx * jnp.tanh(hx)


def _rms(x, g):
    ms = jnp.mean(x * x, axis=-1, keepdims=True)
    return x * lax.rsqrt(ms + EPS) * g


def _mod_row(ml_ref, mc_ref, idx, is_lat):
    return jnp.where(is_lat, ml_ref[0, idx:idx + 1, :], mc_ref[idx:idx + 1, :])


def _is_lat(t, tm, n_lat):
    rows = t * tm + lax.broadcasted_iota(jnp.int32, (tm, 1), 0)
    return rows < n_lat


def _ada_kernel(c_ref, w_ref, b_ref, o_ref):
    s = _silu(c_ref[...]).astype(BF16)
    o_ref[0] = jnp.dot(s, w_ref[0].astype(BF16), preferred_element_type=F32) + b_ref[0]


def _ada_mods(cvec, ada_w, ada_b):
    depth, d, n = ada_w.shape
    bp = cvec.shape[0]
    tn = _pick(n, (1536, 1024, 512, 256, 128))
    return pl.pallas_call(
        _ada_kernel,
        grid=(depth, n // tn),
        in_specs=[
            pl.BlockSpec((bp, d), lambda i, j: (0, 0)),
            pl.BlockSpec((1, d, tn), lambda i, j: (i, 0, j)),
            pl.BlockSpec((1, 1, tn), lambda i, j: (i, 0, j)),
        ],
        out_specs=pl.BlockSpec((1, bp, tn), lambda i, j: (i, 0, j)),
        out_shape=jax.ShapeDtypeStruct((depth, bp, n), F32),
        compiler_params=_params(("arbitrary", "arbitrary")),
        name="ada_mods",
    )(cvec, ada_w, ada_b.reshape(depth, 1, n))


def _in_proj_kernel(x_ref, ml_ref, mc_ref, g_ref, w_ref, ws_ref, o_ref, os_ref, h_ref, *, tm, n_lat):
    t = pl.program_id(1)
    j = pl.program_id(2)

    @pl.when(j == 0)
    def _():
        rc = _pick(tm, (256, 128))
        for r in range(0, tm, rc):
            is_lat = (t * tm + r + lax.broadcasted_iota(jnp.int32, (rc, 1), 0)) < n_lat
            h = _rms(x_ref[0, r:r + rc, :], g_ref[...])
            h = h * (1.0 + _mod_row(ml_ref, mc_ref, 1, is_lat)) + _mod_row(ml_ref, mc_ref, 0, is_lat)
            hb = h.astype(BF16)
            h_ref[r:r + rc, :] = hb
            os_ref[0, r:r + rc, :] = jnp.dot(hb, ws_ref[...], preferred_element_type=F32)
            o_ref[0, r:r + rc, :] = jnp.dot(hb, w_ref[...], preferred_element_type=F32).astype(o_ref.dtype)

    @pl.when(j > 0)
    def _():
        o_ref[0] = jnp.dot(h_ref[...], w_ref[...], preferred_element_type=F32).astype(o_ref.dtype)


def _in_proj(xa, mods_l, mods_c, g, w_main, w_side, n_lat):
    b, n, d = xa.shape
    nm = w_main.shape[1]
    ns = w_side.shape[1]
    tm = _pick(n, (768, 384, 256, 128))
    tn = _pick(nm, (1536, 1024, 768, 640, 512, 256, 128))
    kern = functools.partial(_in_proj_kernel, tm=tm, n_lat=n_lat)
    return pl.pallas_call(
        kern,
        grid=(b, n // tm, nm // tn),
        in_specs=[
            pl.BlockSpec((1, tm, d), lambda i, t, j: (i, t, 0)),
            pl.BlockSpec((1, ADA_CHUNKS, d), lambda i, t, j: (i, 0, 0)),
            pl.BlockSpec((ADA_CHUNKS, d), lambda i, t, j: (0, 0)),
            pl.BlockSpec((1, d), lambda i, t, j: (0, 0)),
            pl.BlockSpec((d, tn), lambda i, t, j: (0, j)),
            pl.BlockSpec((d, ns), lambda i, t, j: (0, 0)),
        ],
        out_specs=[
            pl.BlockSpec((1, tm, tn), lambda i, t, j: (i, t, j)),
            pl.BlockSpec((1, tm, ns), lambda i, t, j: (i, t, 0)),
        ],
        out_shape=[
            jax.ShapeDtypeStruct((b, n, nm), BF16),
            jax.ShapeDtypeStruct((b, n, ns), F32),
        ],
        scratch_shapes=[pltpu.VMEM((tm, d), BF16)],
        compiler_params=_params(("arbitrary", "arbitrary", "arbitrary")),
        name="in_proj",
    )(xa, mods_l, mods_c, g, w_main, w_side)


def _cumsum_rows(v, reverse):
    n = v.shape[0]
    row = lax.broadcasted_iota(jnp.int32, v.shape, 0)
    k = 1
    while k < n:
        if reverse:
            v = v + jnp.where(row < n - k, pltpu.roll(v, n - k, axis=0), 0.0)
        else:
            v = v + jnp.where(row >= k, pltpu.roll(v, k, axis=0), 0.0)
        k *= 2
    return v


def _ssd_chunk_index(s, nl, nc, reverse):
    if reverse:
        return nc - 1 - s
    return lax.rem(s + nl, nc)


def _ssd_kernel(*refs, reverse, finish, nl, nc):
    if finish:
        (xm_ref, xp_ref, xn_ref, dt_ref, cw_ref, cb_ref, dtb_ref, alog_ref,
         z_ref, yf_ref, dsk_ref, ng_ref, o_ref, st_ref) = refs
    else:
        (xm_ref, xp_ref, xn_ref, dt_ref, cw_ref, cb_ref, dtb_ref, alog_ref, o_ref, st_ref) = refs
    q = SSD_CHUNK
    p = SSD_HEADDIM
    hpg = SSD_HEADS // SSD_GROUPS
    gw = hpg * p
    s = pl.program_id(1)
    chunk = _ssd_chunk_index(s, nl, nc, reverse)

    @pl.when(s == 0)
    def _():
        st_ref[...] = jnp.zeros_like(st_ref)

    xmb = xm_ref[0]
    xm = xmb.astype(F32)
    halo = xp_ref.shape[1]
    seq_first = jnp.logical_or(chunk == 0, chunk == nl)
    seq_last = jnp.logical_or(chunk == nl - 1, chunk == nc - 1)
    prev_row = xp_ref[0, halo - 1:halo, :].astype(F32) * jnp.where(seq_first, 0.0, 1.0)
    next_row = xn_ref[0, 0:1, :].astype(F32) * jnp.where(seq_last, 0.0, 1.0)
    ii = lax.broadcasted_iota(jnp.int32, (q, q), 0)
    jj = lax.broadcasted_iota(jnp.int32, (q, q), 1)
    shift_dn = jnp.where(ii == jj + 1, 1.0, 0.0).astype(BF16)
    shift_up = jnp.where(ii + 1 == jj, 1.0, 0.0).astype(BF16)
    x_prev = jnp.dot(shift_dn, xmb, preferred_element_type=F32)
    x_next = jnp.dot(shift_up, xmb, preferred_element_type=F32)
    row8 = lax.broadcasted_iota(jnp.int32, (SUBLANES, 1), 0)
    x_prev = jnp.concatenate(
        [x_prev[:SUBLANES] + jnp.where(row8 == 0, prev_row, 0.0), x_prev[SUBLANES:]], axis=0)
    x_next = jnp.concatenate(
        [x_next[:q - SUBLANES], x_next[q - SUBLANES:] + jnp.where(row8 == SUBLANES - 1, next_row, 0.0)], axis=0)
    xc = x_prev * cw_ref[0:1, :] + xm * cw_ref[1:2, :] + x_next * cw_ref[2:3, :] + cb_ref[...]
    act = _silu(xc)
    xs = act[:, :SSD_INNER]
    xsb = xs.astype(BF16)
    bm = act[:, SSD_INNER:SSD_INNER + SSD_GN]
    cm = act[:, SSD_INNER + SSD_GN:]

    dtr = dt_ref[0] + dtb_ref[...]
    dt = jnp.maximum(dtr, 0.0) + jnp.log(1.0 + jnp.exp(-jnp.abs(dtr)))
    a = -jnp.exp(alog_ref[...])
    acs = _cumsum_rows(dt * a, reverse)
    acs_t = acs.T
    dt_t = dt.T
    edge = acs[0:1, :] if reverse else acs[q - 1:q, :]
    dtd = dt * jnp.exp(edge - acs)
    cdec = jnp.exp(edge)
    mask = (ii <= jj) if reverse else (ii >= jj)
    lane_h = lax.broadcasted_iota(jnp.int32, (q, LANES), 1)
    lane_g = lax.broadcasted_iota(jnp.int32, (q, gw), 1)

    def expand(v, g):
        rows = v.shape[0]
        parts = []
        for k in range(0, hpg, LANES // p):
            h0 = g * hpg + k
            lo = jnp.broadcast_to(v[:, h0:h0 + 1], (rows, LANES))
            hi = jnp.broadcast_to(v[:, h0 + 1:h0 + 2], (rows, LANES))
            parts.append(jnp.where(lane_h[:rows] < p, lo, hi))
        return jnp.concatenate(parts, axis=1)

    ys = []
    for g in range(SSD_GROUPS):
        cg32 = cm[:, g * SSD_STATE:(g + 1) * SSD_STATE]
        bg32 = bm[:, g * SSD_STATE:(g + 1) * SSD_STATE]
        cb = lax.dot_general(cg32.astype(BF16), bg32.astype(BF16), (((1,), (1,)), ((), ())),
                             preferred_element_type=F32)
        st = st_ref[g]
        xg = xs[:, g * gw:(g + 1) * gw]
        rhs = jnp.concatenate([xsb[:, g * gw:(g + 1) * gw], st.astype(BF16)], axis=0)
        y_g = None
        for k in reversed(range(hpg)):
            h = g * hpg + k
            a_col = jnp.broadcast_to(acs[:, h:h + 1], (q, q))
            seg = a_col - jnp.broadcast_to(acs_t[h:h + 1, :], (q, q))
            lmat = jnp.exp(jnp.where(mask, seg, -jnp.inf))
            m_h = (cb * lmat * jnp.broadcast_to(dt_t[h:h + 1, :], (q, q))).astype(BF16)
            c_h = (cg32 * jnp.exp(a_col)).astype(BF16)
            res = jnp.dot(jnp.concatenate([m_h, c_h], axis=1), rhs, preferred_element_type=F32)
            y_g = res if y_g is None else jnp.where(lane_g < (k + 1) * p, res, y_g)
        ys.append(y_g)
        xdd = (xg * expand(dtd, g)).astype(BF16)
        new = jnp.dot(bg32.T.astype(BF16), xdd, preferred_element_type=F32)
        st_ref[g] = st * expand(cdec, g) + new
    y = jnp.concatenate(ys, axis=1)

    if finish:
        y = yf_ref[0] + y + xs * dsk_ref[...]
        y = y * _silu(z_ref[0].astype(F32))
        o_ref[0] = _rms(y, ng_ref[...]).astype(o_ref.dtype)
    else:
        o_ref[0] = y


def _ssd_direction(p0, s0, conv_w, conv_b, dt_bias, a_log, n_lat, reverse, extra=None):
    b, n, _ = p0.shape
    q = SSD_CHUNK
    nc = n // q
    nl = n_lat // q
    halo = 16
    hpc = q // halo
    nh = n // halo
    d = 1 if reverse else 0
    cidx = functools.partial(_ssd_chunk_index, nl=nl, nc=nc, reverse=reverse)
    finish = extra is not None
    in_specs = [
        pl.BlockSpec((1, q, SSD_CONV_DIM), lambda i, s: (i, cidx(s), 0)),
        pl.BlockSpec((1, halo, SSD_CONV_DIM), lambda i, s: (i, jnp.maximum(cidx(s) * hpc - 1, 0), 0)),
        pl.BlockSpec((1, halo, SSD_CONV_DIM), lambda i, s: (i, jnp.minimum(cidx(s) * hpc + hpc, nh - 1), 0)),
        pl.BlockSpec((1, q, LANES), lambda i, s: (i, cidx(s), d)),
        pl.BlockSpec((3, SSD_CONV_DIM), lambda i, s: (0, 0)),
        pl.BlockSpec((1, SSD_CONV_DIM), lambda i, s: (0, 0)),
        pl.BlockSpec((1, LANES), lambda i, s: (0, 0)),
        pl.BlockSpec((1, LANES), lambda i, s: (0, 0)),
    ]
    args = [p0, p0, p0, s0, conv_w, conv_b, dt_bias, a_log]
    if finish:
        yf, d_skip, norm_g = extra
        in_specs += [
            pl.BlockSpec((1, q, SSD_INNER), lambda i, s: (i, cidx(s), SSD_CONV_DIM // SSD_INNER)),
            pl.BlockSpec((1, q, SSD_INNER), lambda i, s: (i, cidx(s), 0)),
            pl.BlockSpec((1, SSD_INNER), lambda i, s: (0, 0)),
            pl.BlockSpec((1, SSD_INNER), lambda i, s: (0, 0)),
        ]
        args += [p0, yf, d_skip, norm_g]
    kern = functools.partial(_ssd_kernel, reverse=reverse, finish=finish, nl=nl, nc=nc)
    return pl.pallas_call(
        kern,
        grid=(b, nc),
        in_specs=in_specs,
        out_specs=pl.BlockSpec((1, q, SSD_INNER), lambda i, s: (i, cidx(s), 0)),
        out_shape=jax.ShapeDtypeStruct((b, n, SSD_INNER), BF16 if finish else F32),
        scratch_shapes=[pltpu.VMEM((SSD_GROUPS, SSD_STATE, (SSD_HEADS // SSD_GROUPS) * SSD_HEADDIM), F32)],
        compiler_params=_params(("arbitrary", "arbitrary")),
        name="ssd_bwd_finish" if finish else "ssd_fwd",
    )(*args)


def _rope_tables(rows, rot_dim, n_ctx):
    n_freq = rot_dim // 4
    row = jnp.repeat(jnp.arange(rows, dtype=F32), GRID_W)
    col = jnp.tile(jnp.arange(GRID_W, dtype=F32), rows)
    inv = ROPE_THETA ** (-jnp.arange(n_freq, dtype=F32) / n_freq)
    ang = jnp.concatenate([row[:, None] * inv, col[:, None] * inv], axis=-1)
    cos, sin = jnp.cos(ang), jnp.sin(ang)
    reps = LANES // rot_dim
    cos_t = jnp.tile(jnp.concatenate([cos, cos], axis=-1), (1, reps))
    sin_t = jnp.tile(jnp.concatenate([-sin, sin], axis=-1), (1, reps))
    cos_t = jnp.concatenate([cos_t, jnp.ones((n_ctx, LANES), F32)], axis=0)
    sin_t = jnp.concatenate([sin_t, jnp.zeros((n_ctx, LANES), F32)], axis=0)
    return cos_t, sin_t


def _rope_block(xb, cos, sin, rot_dim):
    half = rot_dim // 2
    lane = lax.broadcasted_iota(jnp.int32, xb.shape, 1)
    first = jnp.bitwise_and(lane, rot_dim - 1) < half
    partner = jnp.where(first, pltpu.roll(xb, LANES - half, axis=1), pltpu.roll(xb, half, axis=1))
    return xb * cos + partner * sin


def _segment_mean_matrix(seg):
    sh = seg.bit_length() - 1
    i = jnp.right_shift(lax.broadcasted_iota(jnp.int32, (LANES, LANES), 0), sh)
    j = jnp.right_shift(lax.broadcasted_iota(jnp.int32, (LANES, LANES), 1), sh)
    return jnp.where(i == j, 1.0 / seg, 0.0).astype(BF16)


def _segment_mean_sq(xb, smat):
    x2 = xb * xb
    hi = x2.astype(BF16)
    lo = (x2 - hi.astype(F32)).astype(BF16)
    return (jnp.dot(hi, smat, preferred_element_type=F32) + jnp.dot(lo, smat, preferred_element_type=F32))


def _store_key_norms(ksq_ref, c, sumsq):
    mx = jnp.max(sumsq, axis=0, keepdims=True)
    ksq_ref[0, 0, :, (2 * c) * LANES:(2 * c + 1) * LANES] = mx
    ksq_ref[0, 0, :, (2 * c + 1) * LANES:(2 * c + 2) * LANES] = pltpu.roll(mx, LANES // 2, axis=1)


def _qk_prep_kernel(q_ref, k_ref, v_ref, cos_ref, sin_ref, qg_ref, kg_ref, qo_ref, ko_ref, vo_ref, ksq_ref, *, scale):
    smat = _segment_mean_matrix(ATT_HEADDIM)
    cos = cos_ref[...]
    sin = sin_ref[...]
    hpb = LANES // ATT_HEADDIM

    def prep(src_ref, g_ref, dst_ref, n_heads, mul, transposed):
        for c in range(n_heads // hpb):
            xb = src_ref[0, :, c * LANES:(c + 1) * LANES].astype(F32)
            xn = xb * lax.rsqrt(_segment_mean_sq(xb, smat) + EPS) * g_ref[...]
            r = _rope_block(xn, cos, sin, ATT_HEADDIM)
            if mul != 1.0:
                r = r * mul
            if transposed:
                r = r.T
            else:
                _store_key_norms(ksq_ref, c, _segment_mean_sq(r, smat) * float(ATT_HEADDIM))
            r = r.astype(dst_ref.dtype)
            for k in range(hpb):
                if transposed:
                    dst_ref[0, c * hpb + k] = r[k * ATT_HEADDIM:(k + 1) * ATT_HEADDIM, :]
                else:
                    dst_ref[0, c * hpb + k] = r[:, k * ATT_HEADDIM:(k + 1) * ATT_HEADDIM]

    prep(q_ref, qg_ref, qo_ref, ATT_HEADS, scale, True)
    prep(k_ref, kg_ref, ko_ref, ATT_KV_HEADS, 1.0, False)
    for c in range(ATT_KV_HEADS // hpb):
        vt = v_ref[0, :, c * LANES:(c + 1) * LANES].astype(F32).T.astype(vo_ref.dtype)
        for k in range(hpb):
            vo_ref[0, c * hpb + k] = vt[k * ATT_HEADDIM:(k + 1) * ATT_HEADDIM, :]


def _qk_prep(p0, cos_t, sin_t, q_g, k_g, q_col, k_col, v_col):
    b, n, _ = p0.shape
    tm = _pick(n, (768, 384, 256, 128))
    kern = functools.partial(_qk_prep_kernel, scale=ATT_HEADDIM ** -0.5 * LOG2E)
    return pl.pallas_call(
        kern,
        grid=(b, n // tm),
        in_specs=[
            pl.BlockSpec((1, tm, ATT_Q), lambda i, t: (i, t, q_col // ATT_Q)),
            pl.BlockSpec((1, tm, ATT_KV), lambda i, t: (i, t, k_col // ATT_KV)),
            pl.BlockSpec((1, tm, ATT_KV), lambda i, t: (i, t, v_col // ATT_KV)),
            pl.BlockSpec((tm, LANES), lambda i, t: (t, 0)),
            pl.BlockSpec((tm, LANES), lambda i, t: (t, 0)),
            pl.BlockSpec((1, LANES), lambda i, t: (0, 0)),
            pl.BlockSpec((1, LANES), lambda i, t: (0, 0)),
        ],
        out_specs=[
            pl.BlockSpec((1, ATT_HEADS, ATT_HEADDIM, tm), lambda i, t: (i, 0, 0, t)),
            pl.BlockSpec((1, ATT_KV_HEADS, tm, ATT_HEADDIM), lambda i, t: (i, 0, t, 0)),
            pl.BlockSpec((1, ATT_KV_HEADS, ATT_HEADDIM, tm), lambda i, t: (i, 0, 0, t)),
            pl.BlockSpec((1, 1, 1, ATT_KV_HEADS * LANES), lambda i, t: (i, t, 0, 0)),
        ],
        out_shape=[
            jax.ShapeDtypeStruct((b, ATT_HEADS, ATT_HEADDIM, n), BF16),
            jax.ShapeDtypeStruct((b, ATT_KV_HEADS, n, ATT_HEADDIM), BF16),
            jax.ShapeDtypeStruct((b, ATT_KV_HEADS, ATT_HEADDIM, n), BF16),
            jax.ShapeDtypeStruct((b, n // tm, 1, ATT_KV_HEADS * LANES), F32),
        ],
        compiler_params=_params(("arbitrary", "arbitrary")),
        name="gqa_qk_prep",
    )(p0, p0, p0, cos_t, sin_t, q_g, k_g)


def _attn_kernel(qt_ref, k_ref, vt_ref, ksq_ref, o_ref, s_scr, m_scr, e_scr, gap_scr, ot_scr, *,
                 group, hpb, n_lat, n_all, lat_tiles, ctx_self):
    t = pl.program_id(2)
    dv = vt_ref.shape[2]
    tq = qt_ref.shape[3]
    n_blocks = qt_ref.shape[1] // hpb

    ksq_max = jnp.max(ksq_ref[0], axis=0)

    def queries(i):
        return jnp.concatenate([qt_ref[0, i * hpb + j] for j in range(hpb)], axis=1)

    def finish(i, ot):
        ot = ot[:dv] * (1.0 / ot[dv:dv + 1])
        for j in range(hpb):
            ot_scr[(i * hpb + j) * dv:(i * hpb + j + 1) * dv, :] = ot[:, j * tq:(j + 1) * tq]

    def run(k_lo, k_len, exact):
        row = lax.broadcasted_iota(jnp.int32, (BF16_ROWS, k_len), 0)
        ones_row = jnp.where(row == 0, 1.0, 0.0).astype(BF16)

        def scores(i):
            kv = i * hpb // group
            qt = queries(i)
            s = jnp.dot(k_ref[0, kv, k_lo:k_lo + k_len, :], qt, preferred_element_type=F32)
            m = jnp.max(s, axis=0, keepdims=True)
            if exact:
                s_scr[i % 2, :k_len, :] = s
                m_scr[i % 2] = m
            else:
                qf = qt.astype(F32)
                ksq = ksq_max[:, kv * LANES:kv * LANES + 1]
                bound = jnp.sqrt(jnp.sum(qf * qf, axis=0, keepdims=True) * ksq) * ATTN_BOUND_SLACK
                e_scr[i % 2, :k_len, :] = jnp.exp2(s - bound).astype(BF16)
                gap_scr[i] = bound - m

        def combine(i):
            if exact:
                e = jnp.exp2(s_scr[i % 2, :k_len, :] - m_scr[i % 2]).astype(BF16)
            else:
                e = e_scr[i % 2, :k_len, :]
            vt = vt_ref[0, i * hpb // group, :, k_lo:k_lo + k_len]
            finish(i, jnp.dot(jnp.concatenate([vt, ones_row], axis=0), e, preferred_element_type=F32))

        scores(0)
        for i in range(n_blocks):
            if i + 1 < n_blocks:
                scores(i + 1)
            combine(i)

    def attend(k_lo, k_len):
        run(k_lo, k_len, exact=False)

        @pl.when(jnp.max(gap_scr[...]) > ATTN_MAX_SHIFT)
        def _():
            run(k_lo, k_len, exact=True)

        o_ref[0] = ot_scr[...].T.astype(o_ref.dtype)

    if ctx_self:
        @pl.when(t < lat_tiles)
        def _():
            attend(0, n_all)

        @pl.when(t >= lat_tiles)
        def _():
            attend(n_lat, n_all - n_lat)
    else:
        attend(0, n_all)


def _attention(qt, k, vt, ksq, n_lat, ctx_self):
    b, h, dqk, nq = qt.shape
    hkv, dv, n_all = vt.shape[1], vt.shape[2], vt.shape[3]
    group = h // hkv
    if ctx_self:
        tq = _pick(math.gcd(n_lat, n_all - n_lat), (256, 128))
    else:
        tq = _pick(n_lat, (ATTN_BLOCK_LANES, 256, 128))
    hpb = max(1, min(group, ATTN_BLOCK_LANES // tq))
    heads = min(h, ATTN_BLOCKS_PER_STEP * hpb)
    kvb = max(heads // group, 1)
    assert nq == (n_all if ctx_self else n_lat)
    kern = functools.partial(_attn_kernel, group=group, hpb=hpb, n_lat=n_lat, n_all=n_all,
                             lat_tiles=n_lat // tq, ctx_self=ctx_self)
    return pl.pallas_call(
        kern,
        grid=(b, h // heads, nq // tq),
        in_specs=[
            pl.BlockSpec((1, heads, dqk, tq), lambda i, hb, t: (i, hb, 0, t)),
            pl.BlockSpec((1, kvb, n_all, dqk), lambda i, hb, t: (i, hb, 0, 0)),
            pl.BlockSpec((1, kvb, dv, n_all), lambda i, hb, t: (i, hb, 0, 0)),
            pl.BlockSpec((1, ksq.shape[1], 1, kvb * LANES), lambda i, hb, t: (i, 0, 0, hb)),
        ],
        out_specs=pl.BlockSpec((1, tq, heads * dv), lambda i, hb, t: (i, t, hb)),
        out_shape=jax.ShapeDtypeStruct((b, nq, h * dv), BF16),
        scratch_shapes=[pltpu.VMEM((2, n_all, tq * hpb), F32), pltpu.VMEM((2, 1, tq * hpb), F32),
                        pltpu.VMEM((2, n_all, tq * hpb), BF16), pltpu.VMEM((heads // hpb, 1, tq * hpb), F32),
                        pltpu.VMEM((heads * dv, tq), F32)],
        compiler_params=_params(("arbitrary", "arbitrary", "arbitrary")),
        name="attention",
    )(qt, k, vt, ksq)


def _hidden_chunks(hidden):
    chunks, lo = [], 0
    while lo < hidden:
        w = min(1024, hidden - lo)
        chunks.append((lo, w))
        lo += w
    return chunks


def _mix_ffn_kernel(*refs, n_in, tm, n_lat, final):
    x_ref, ml_ref, mc_ref, g_ref = refs[:4]
    a_refs = refs[4:4 + n_in]
    w_refs = refs[4 + n_in:4 + 2 * n_in]
    wg_ref, wu_ref, wd_ref, fg_ref, o_ref = refs[4 + 2 * n_in:]
    t = pl.program_id(1)
    is_lat = _is_lat(t, tm, n_lat)
    mod = functools.partial(_mod_row, ml_ref, mc_ref, is_lat=is_lat)
    mix = jnp.dot(a_refs[0][0], w_refs[0][...], preferred_element_type=F32)
    for a_ref, w_ref in zip(a_refs[1:], w_refs[1:]):
        mix = mix + jnp.dot(a_ref[0], w_ref[...], preferred_element_type=F32)
    x = x_ref[0] + mod(2) * mix
    h = (_rms(x, g_ref[...]) * (1.0 + mod(4)) + mod(3)).astype(BF16)
    acc = None
    for lo, w in _hidden_chunks(wd_ref.shape[0]):
        gate = jnp.dot(h, wg_ref[:, lo:lo + w], preferred_element_type=F32)
        up = jnp.dot(h, wu_ref[:, lo:lo + w], preferred_element_type=F32)
        a = (_silu(gate) * up).astype(BF16)
        part = jnp.dot(a, wd_ref[lo:lo + w, :], preferred_element_type=F32)
        acc = part if acc is None else acc + part
    y = x + mod(5) * acc
    if final:
        y = _rms(y, fg_ref[...])
    o_ref[0] = y


def _mix_ffn(xa, mods_l, mods_c, g, acts, weights, w_gate, w_up, w_down, final_g, n_rows, n_lat, final):
    b, _, d = xa.shape
    hid = w_down.shape[0]
    tm = _pick(n_rows, (512, 384, 256, 128))
    n_in = len(acts)
    kern = functools.partial(_mix_ffn_kernel, n_in=n_in, tm=tm, n_lat=n_lat, final=final)
    resident = dict(pipeline_mode=pl.Buffered(1))
    in_specs = [
        pl.BlockSpec((1, tm, d), lambda i, t: (i, t, 0)),
        pl.BlockSpec((1, ADA_CHUNKS, d), lambda i, t: (i, 0, 0)),
        pl.BlockSpec((ADA_CHUNKS, d), lambda i, t: (0, 0)),
        pl.BlockSpec((1, d), lambda i, t: (0, 0)),
    ]
    in_specs += [pl.BlockSpec((1, tm, a.shape[2]), lambda i, t: (i, t, 0)) for a in acts]
    in_specs += [pl.BlockSpec(w.shape, lambda i, t: (0, 0), **resident) for w in weights]
    in_specs += [
        pl.BlockSpec((d, hid), lambda i, t: (0, 0), **resident),
        pl.BlockSpec((d, hid), lambda i, t: (0, 0), **resident),
        pl.BlockSpec((hid, d), lambda i, t: (0, 0), **resident),
        pl.BlockSpec((1, d), lambda i, t: (0, 0)),
    ]
    return pl.pallas_call(
        kern,
        grid=(b, n_rows // tm),
        in_specs=in_specs,
        out_specs=pl.BlockSpec((1, tm, d), lambda i, t: (i, t, 0)),
        out_shape=jax.ShapeDtypeStruct((b, n_rows, d), F32),
        compiler_params=_params(("arbitrary", "arbitrary")),
        name="mix_ffn_final" if final else "mix_ffn",
    )(xa, mods_l, mods_c, g, *acts, *weights, w_gate, w_up, w_down, final_g)


def _mla_q_kernel(p_ref, g_ref, w_ref, cos_ref, sin_ref, o_ref, *, q_rank, scale):
    cq = p_ref[0, :, :q_rank].astype(F32)
    qn = _rms(cq, g_ref[...]).astype(BF16)
    qq = jnp.dot(qn, w_ref[...], preferred_element_type=F32)
    qq = qq * scale
    n_nope = MLA_HEADS * MLA_NOPE
    ppb = LANES // MLA_ROPE
    npb = LANES // MLA_NOPE
    pes_t = [_rope_block(qq[:, n_nope + c * LANES:n_nope + (c + 1) * LANES], cos_ref[...], sin_ref[...], MLA_ROPE).T
             for c in range(MLA_HEADS // ppb)]
    for c in range(MLA_HEADS // npb):
        nope_t = qq[:, c * LANES:(c + 1) * LANES].T
        for k in range(npb):
            h = c * npb + k
            pe_t = pes_t[h // ppb][(h % ppb) * MLA_ROPE:(h % ppb + 1) * MLA_ROPE, :]
            qh = jnp.concatenate([nope_t[k * MLA_NOPE:(k + 1) * MLA_NOPE, :], pe_t], axis=0)
            o_ref[0, h] = qh.astype(o_ref.dtype)


def _mla_q(p1, g, w_uq, cos_t, sin_t, n_lat, q_rank):
    b = p1.shape[0]
    cols = p1.shape[2]
    tm = _pick(n_lat, (512, 256, 128))
    kern = functools.partial(_mla_q_kernel, q_rank=q_rank, scale=MLA_QK ** -0.5 * LOG2E)
    return pl.pallas_call(
        kern,
        grid=(b, n_lat // tm),
        in_specs=[
            pl.BlockSpec((1, tm, cols), lambda i, t: (i, t, 0)),
            pl.BlockSpec((1, q_rank), lambda i, t: (0, 0)),
            pl.BlockSpec(w_uq.shape, lambda i, t: (0, 0)),
            pl.BlockSpec((tm, LANES), lambda i, t: (t, 0)),
            pl.BlockSpec((tm, LANES), lambda i, t: (t, 0)),
        ],
        out_specs=pl.BlockSpec((1, MLA_HEADS, MLA_QK, tm), lambda i, t: (i, 0, 0, t)),
        out_shape=jax.ShapeDtypeStruct((b, MLA_HEADS, MLA_QK, n_lat), BF16),
        compiler_params=_params(("arbitrary", "arbitrary")),
        name="mla_q",
    )(p1, g, w_uq, cos_t, sin_t)


def _mla_kv_kernel(p_ref, pe_ref, g_ref, w_ref, cos_ref, sin_ref, ko_ref, vo_ref, ksq_ref, *, q_rank):
    ckv = p_ref[0, :, q_rank:].astype(F32)
    kvn = _rms(ckv, g_ref[...]).astype(BF16)
    kv = jnp.dot(kvn, w_ref[...], preferred_element_type=F32)
    pe_block = _rope_block(pe_ref[0], cos_ref[...], sin_ref[...], MLA_ROPE)
    pe = pe_block[:, :MLA_ROPE]
    n_nope = MLA_HEADS * MLA_NOPE
    for h in range(MLA_HEADS):
        kh = jnp.concatenate([kv[:, h * MLA_NOPE:(h + 1) * MLA_NOPE], pe], axis=1)
        ko_ref[0, h] = kh.astype(ko_ref.dtype)
    pe_sq = jnp.sum(pe_block * pe_block, axis=1, keepdims=True)
    smat = _segment_mean_matrix(MLA_NOPE)
    for c in range(MLA_HEADS * MLA_NOPE // LANES):
        nope_sq = _segment_mean_sq(kv[:, c * LANES:(c + 1) * LANES], smat) * float(MLA_NOPE)
        _store_key_norms(ksq_ref, c, nope_sq + pe_sq)
    hpb = LANES // MLA_V
    for c in range(MLA_HEADS // hpb):
        v_t = kv[:, n_nope + c * LANES:n_nope + (c + 1) * LANES].T.astype(vo_ref.dtype)
        for k in range(hpb):
            vo_ref[0, c * hpb + k] = v_t[k * MLA_V:(k + 1) * MLA_V, :]


def _mla_kv(p1, s1, g, w_ukv, cos_t, sin_t, q_rank):
    b, n, cols = p1.shape
    kv_rank = cols - q_rank
    tm = _pick(n, (768, 384, 256, 128))
    kern = functools.partial(_mla_kv_kernel, q_rank=q_rank)
    return pl.pallas_call(
        kern,
        grid=(b, n // tm),
        in_specs=[
            pl.BlockSpec((1, tm, cols), lambda i, t: (i, t, 0)),
            pl.BlockSpec((1, tm, LANES), lambda i, t: (i, t, 0)),
            pl.BlockSpec((1, kv_rank), lambda i, t: (0, 0)),
            pl.BlockSpec(w_ukv.shape, lambda i, t: (0, 0)),
            pl.BlockSpec((tm, LANES), lambda i, t: (t, 0)),
            pl.BlockSpec((tm, LANES), lambda i, t: (t, 0)),
        ],
        out_specs=[
            pl.BlockSpec((1, MLA_HEADS, tm, MLA_QK), lambda i, t: (i, 0, t, 0)),
            pl.BlockSpec((1, MLA_HEADS, MLA_V, tm), lambda i, t: (i, 0, 0, t)),
            pl.BlockSpec((1, 1, 1, MLA_HEADS * LANES), lambda i, t: (i, t, 0, 0)),
        ],
        out_shape=[
            jax.ShapeDtypeStruct((b, MLA_HEADS, n, MLA_QK), BF16),
            jax.ShapeDtypeStruct((b, MLA_HEADS, MLA_V, n), BF16),
            jax.ShapeDtypeStruct((b, n // tm, 1, MLA_HEADS * LANES), F32),
        ],
        compiler_params=_params(("arbitrary", "arbitrary")),
        name="mla_kv",
    )(p1, s1, g, w_ukv, cos_t, sin_t)


def _pad_lanes(v, width=LANES):
    v = v.reshape(1, -1).astype(F32)
    return jnp.pad(v, ((0, 0), (0, width - v.shape[1])))


def kernel(x, c, ctx, c_ctx, ada_w, ada_b, norm1_g, norm2_g, ffn_w_up, ffn_w_down, ab_w_in, ab_w_out, ssd_conv_w, ssd_conv_b, ssd_a_log, ssd_dt_bias, ssd_d, ssd_norm_g, att_q_g, att_k_g, mla_w_in, mla_q_norm_g, mla_w_uq, mla_kv_norm_g, mla_w_ukv, mla_w_o, final_norm_g):
    b, t_lat, d = x.shape
    n_ctx = ctx.shape[1]
    rows = t_lat // GRID_W
    hid = ffn_w_down.shape[1]
    row2 = lambda v: v.reshape(1, -1).astype(F32)

    bp = -(-(b + 1) // 8) * 8
    cvec = jnp.concatenate([c, c_ctx[None, :], jnp.zeros((bp - b - 1, d), F32)], axis=0)
    mods = _ada_mods(cvec, ada_w, ada_b)
    mods_l = [mods[i, :b].reshape(b, ADA_CHUNKS, d) for i in range(2)]
    mods_c = [mods[i, b].reshape(ADA_CHUNKS, d) for i in range(2)]

    xa = jnp.concatenate([x, ctx], axis=1)

    w_in = ab_w_in[0]
    o_xbc, o_dt = SSD_INNER, SSD_INNER + SSD_CONV_DIM
    o_att = o_dt + 2 * SSD_HEADS
    w_main = jnp.concatenate([w_in[:, o_xbc:o_dt], w_in[:, :SSD_INNER], w_in[:, o_att:]], axis=1).astype(BF16)
    w_dt = jnp.zeros((d, 2 * LANES), F32)
    w_dt = w_dt.at[:, :SSD_HEADS].set(w_in[:, o_dt:o_dt + SSD_HEADS])
    w_dt = w_dt.at[:, LANES:LANES + SSD_HEADS].set(w_in[:, o_dt + SSD_HEADS:o_att]).astype(BF16)
    p0, s0 = _in_proj(xa, mods_l[0], mods_c[0], row2(norm1_g[0]), w_main, w_dt, t_lat)

    conv_w = ssd_conv_w[0].astype(F32)
    conv_b = row2(ssd_conv_b[0])
    yf = _ssd_direction(p0, s0, conv_w, conv_b, _pad_lanes(ssd_dt_bias[0, 0]), _pad_lanes(ssd_a_log[0, 0]),
                        t_lat, reverse=False)
    d_skip = jnp.repeat(ssd_d[0].astype(F32), SSD_HEADDIM).reshape(1, SSD_INNER)
    y_ssd = _ssd_direction(p0, s0, conv_w, conv_b, _pad_lanes(ssd_dt_bias[0, 1]), _pad_lanes(ssd_a_log[0, 1]),
                           t_lat, reverse=True, extra=(yf, d_skip, row2(ssd_norm_g[0])))

    q_col = SSD_CONV_DIM + SSD_INNER
    k_col = q_col + ATT_Q
    cos_a, sin_a = _rope_tables(rows, ATT_HEADDIM, n_ctx)
    hpb = LANES // ATT_HEADDIM
    qh, kh, vh, ksq0 = _qk_prep(p0, cos_a, sin_a, jnp.tile(row2(att_q_g[0]), (1, hpb)),
                                jnp.tile(row2(att_k_g[0]), (1, hpb)), q_col, k_col, k_col + ATT_KV)
    o_att_l0 = _attention(qh, kh, vh, ksq0, t_lat, ctx_self=True)

    w_out = ab_w_out[0].astype(BF16)
    w_up = ffn_w_up[0].astype(BF16)
    xa = _mix_ffn(xa, mods_l[0], mods_c[0], row2(norm2_g[0]), [y_ssd, o_att_l0],
                  [w_out[:SSD_INNER], w_out[SSD_INNER:]], w_up[:, :hid], w_up[:, hid:],
                  ffn_w_down[0].astype(BF16), row2(final_norm_g), t_lat + n_ctx, t_lat, final=False)

    q_rank = mla_q_norm_g.shape[1]
    kv_rank = mla_kv_norm_g.shape[1]
    w_in1 = mla_w_in[0]
    w_pe = jnp.pad(w_in1[:, q_rank + kv_rank:], ((0, 0), (0, LANES - MLA_ROPE))).astype(BF16)
    p1, s1 = _in_proj(xa, mods_l[1], mods_c[1], row2(norm1_g[1]), w_in1[:, :q_rank + kv_rank].astype(BF16), w_pe,
                      t_lat)
    cos_m, sin_m = _rope_tables(rows, MLA_ROPE, n_ctx)
    w_uq = mla_w_uq[0].reshape(q_rank, MLA_HEADS, MLA_QK)
    w_uq = jnp.concatenate([w_uq[:, :, :MLA_NOPE].reshape(q_rank, -1), w_uq[:, :, MLA_NOPE:].reshape(q_rank, -1)],
                           axis=1).astype(BF16)
    w_ukv = mla_w_ukv[0].reshape(kv_rank, MLA_HEADS, MLA_NOPE + MLA_V)
    w_ukv = jnp.concatenate([w_ukv[:, :, :MLA_NOPE].reshape(kv_rank, -1), w_ukv[:, :, MLA_NOPE:].reshape(kv_rank, -1)],
                            axis=1).astype(BF16)
    qm = _mla_q(p1, row2(mla_q_norm_g[0]), w_uq, cos_m, sin_m, t_lat, q_rank)
    km, vm, ksq1 = _mla_kv(p1, s1, row2(mla_kv_norm_g[0]), w_ukv, cos_m, sin_m, q_rank)
    o_mla = _attention(qm, km, vm, ksq1, t_lat, ctx_self=False)

    w_up = ffn_w_up[1].astype(BF16)
    return _mix_ffn(xa, mods_l[1], mods_c[1], row2(norm2_g[1]), [o_mla], [mla_w_o[0].astype(BF16)],
                    w_up[:, :hid], w_up[:, hid:], ffn_w_down[1].astype(BF16), row2(final_norm_g),
                    t_lat, t_lat, final=True)
```

```python
import functools
import math

import jax
import jax.numpy as jnp
from jax import lax
from jax.experimental import pallas as pl
from jax.experimental.pallas import tpu as pltpu

F32 = jnp.float32
BF16 = jnp.bfloat16

EPS = 1e-6
ROPE_THETA = 10000.0
GRID_W = 64
ADA_CHUNKS = 6

SSD_HEADS = 16
SSD_HEADDIM = 64
SSD_GROUPS = 4
SSD_STATE = 128
SSD_CHUNK = 128
SSD_BATCH_PER_STEP = 2
SSD_INNER = SSD_HEADS * SSD_HEADDIM
SSD_GN = SSD_GROUPS * SSD_STATE
SSD_CONV_DIM = SSD_INNER + 2 * SSD_GN

ATT_HEADS = 16
ATT_KV_HEADS = 4
ATT_HEADDIM = 64
ATT_Q = ATT_HEADS * ATT_HEADDIM
ATT_KV = ATT_KV_HEADS * ATT_HEADDIM

MLA_HEADS = 16
MLA_NOPE = 64
MLA_ROPE = 32
MLA_V = 64
MLA_QK = MLA_NOPE + MLA_ROPE

LANES = 128
SUBLANES = 8
BF16_ROWS = 16
ATTN_BLOCKS_PER_STEP = 8
ATTN_BOUND_SLACK = 1.0 + 2.0 ** -6
ATTN_MAX_SHIFT = 64.0
ATTN_BLOCK_LANES = 512
LOG2E = math.log2(math.e)
VMEM_LIMIT = 56 * 1024 * 1024


def _pick(n, candidates):
    for c in candidates:
        if n % c == 0:
            return c
    raise ValueError(f"no tile for {n} in {candidates}")


def _params(sem):
    return pltpu.CompilerParams(dimension_semantics=sem, vmem_limit_bytes=VMEM_LIMIT)


def _silu(x):
    hx = 0.5 * x
    return hx + hx * jnp.tanh(hx)


def _rms(x, g):
    ms = jnp.mean(x * x, axis=-1, keepdims=True)
    return x * lax.rsqrt(ms + EPS) * g


def _mod_row(ml_ref, mc_ref, idx, is_lat):
    return jnp.where(is_lat, ml_ref[0, idx:idx + 1, :], mc_ref[idx:idx + 1, :])


def _is_lat(t, tm, n_lat):
    rows = t * tm + lax.broadcasted_iota(jnp.int32, (tm, 1), 0)
    return rows < n_lat


def _ada_kernel(c_ref, w_ref, b_ref, o_ref):
    s = _silu(c_ref[...]).astype(BF16)
    o_ref[0] = jnp.dot(s, w_ref[0].astype(BF16), preferred_element_type=F32) + b_ref[0]


def _ada_mods(cvec, ada_w, ada_b):
    depth, d, n = ada_w.shape
    bp = cvec.shape[0]
    tn = _pick(n, (1536, 1024, 512, 256, 128))
    return pl.pallas_call(
        _ada_kernel,
        grid=(depth, n // tn),
        in_specs=[
            pl.BlockSpec((bp, d), lambda i, j: (0, 0)),
            pl.BlockSpec((1, d, tn), lambda i, j: (i, 0, j)),
            pl.BlockSpec((1, 1, tn), lambda i, j: (i, 0, j)),
        ],
        out_specs=pl.BlockSpec((1, bp, tn), lambda i, j: (i, 0, j)),
        out_shape=jax.ShapeDtypeStruct((depth, bp, n), F32),
        compiler_params=_params(("arbitrary", "arbitrary")),
        name="ada_mods",
    )(cvec, ada_w, ada_b.reshape(depth, 1, n))


def _in_proj_kernel(x_ref, ml_ref, mc_ref, g_ref, w_ref, ws_ref, o_ref, os_ref, h_ref, *, tm, n_lat):
    t = pl.program_id(1)
    j = pl.program_id(2)

    @pl.when(j == 0)
    def _():
        rc = _pick(tm, (256, 128))
        for r in range(0, tm, rc):
            is_lat = (t * tm + r + lax.broadcasted_iota(jnp.int32, (rc, 1), 0)) < n_lat
            h = _rms(x_ref[0, r:r + rc, :], g_ref[...])
            h = h * (1.0 + _mod_row(ml_ref, mc_ref, 1, is_lat)) + _mod_row(ml_ref, mc_ref, 0, is_lat)
            hb = h.astype(BF16)
            h_ref[r:r + rc, :] = hb
            os_ref[0, r:r + rc, :] = jnp.dot(hb, ws_ref[...], preferred_element_type=F32)
            o_ref[0, r:r + rc, :] = jnp.dot(hb, w_ref[...], preferred_element_type=F32).astype(o_ref.dtype)

    @pl.when(j > 0)
    def _():
        o_ref[0] = jnp.dot(h_ref[...], w_ref[...], preferred_element_type=F32).astype(o_ref.dtype)


def _in_proj(xa, mods_l, mods_c, g, w_main, w_side, n_lat):
    b, n, d = xa.shape
    nm = w_main.shape[1]
    ns = w_side.shape[1]
    tm = _pick(n, (768, 384, 256, 128))
    tn = _pick(nm, (1536, 1024, 768, 640, 512, 256, 128))
    kern = functools.partial(_in_proj_kernel, tm=tm, n_lat=n_lat)
    return pl.pallas_call(
        kern,
        grid=(b, n // tm, nm // tn),
        in_specs=[
            pl.BlockSpec((1, tm, d), lambda i, t, j: (i, t, 0)),
            pl.BlockSpec((1, ADA_CHUNKS, d), lambda i, t, j: (i, 0, 0)),
            pl.BlockSpec((ADA_CHUNKS, d), lambda i, t, j: (0, 0)),
            pl.BlockSpec((1, d), lambda i, t, j: (0, 0)),
            pl.BlockSpec((d, tn), lambda i, t, j: (0, j)),
            pl.BlockSpec((d, ns), lambda i, t, j: (0, 0)),
        ],
        out_specs=[
            pl.BlockSpec((1, tm, tn), lambda i, t, j: (i, t, j)),
            pl.BlockSpec((1, tm, ns), lambda i, t, j: (i, t, 0)),
        ],
        out_shape=[
            jax.ShapeDtypeStruct((b, n, nm), BF16),
            jax.ShapeDtypeStruct((b, n, ns), F32),
        ],
        scratch_shapes=[pltpu.VMEM((tm, d), BF16)],
        compiler_params=_params(("arbitrary", "arbitrary", "arbitrary")),
        name="in_proj",
    )(xa, mods_l, mods_c, g, w_main, w_side)


def _cumsum_rows(v, reverse):
    n = v.shape[0]
    row = lax.broadcasted_iota(jnp.int32, v.shape, 0)
    k = 1
    while k < n:
        if reverse:
            v = v + jnp.where(row < n - k, pltpu.roll(v, n - k, axis=0), 0.0)
        else:
            v = v + jnp.where(row >= k, pltpu.roll(v, k, axis=0), 0.0)
        k *= 2
    return v


def _ssd_chunk_index(s, nl, nc, reverse):
    if reverse:
        return nc - 1 - s
    return lax.rem(s + nl, nc)


def _ssd_kernel(*refs, reverse, finish, nl, nc):
    st_ref = refs[-1]

    @pl.when(pl.program_id(1) == 0)
    def _():
        st_ref[...] = jnp.zeros_like(st_ref)

    for bb in range(refs[0].shape[0]):
        _ssd_chunk(bb, *refs, reverse=reverse, finish=finish, nl=nl, nc=nc)


def _ssd_chunk(bb, *refs, reverse, finish, nl, nc):
    if finish:
        (xm_ref, xp_ref, xn_ref, dt_ref, cw_ref, cb_ref, dtb_ref, alog_ref,
         z_ref, yf_ref, dsk_ref, ng_ref, o_ref, st_ref) = refs
    else:
        (xm_ref, xp_ref, xn_ref, dt_ref, cw_ref, cb_ref, dtb_ref, alog_ref, o_ref, st_ref) = refs
    q = SSD_CHUNK
    p = SSD_HEADDIM
    hpg = SSD_HEADS // SSD_GROUPS
    gw = hpg * p
    chunk = _ssd_chunk_index(pl.program_id(1), nl, nc, reverse)

    xmb = xm_ref[bb]
    xm = xmb.astype(F32)
    halo = xp_ref.shape[1]
    seq_first = jnp.logical_or(chunk == 0, chunk == nl)
    seq_last = jnp.logical_or(chunk == nl - 1, chunk == nc - 1)
    prev_row = xp_ref[bb, halo - 1:halo, :].astype(F32) * jnp.where(seq_first, 0.0, 1.0)
    next_row = xn_ref[bb, 0:1, :].astype(F32) * jnp.where(seq_last, 0.0, 1.0)
    ii = lax.broadcasted_iota(jnp.int32, (q, q), 0)
    jj = lax.broadcasted_iota(jnp.int32, (q, q), 1)
    shift_dn = jnp.where(ii == jj + 1, 1.0, 0.0).astype(BF16)
    shift_up = jnp.where(ii + 1 == jj, 1.0, 0.0).astype(BF16)
    x_prev = jnp.dot(shift_dn, xmb, preferred_element_type=F32)
    x_next = jnp.dot(shift_up, xmb, preferred_element_type=F32)
    row8 = lax.broadcasted_iota(jnp.int32, (SUBLANES, 1), 0)
    x_prev = jnp.concatenate(
        [x_prev[:SUBLANES] + jnp.where(row8 == 0, prev_row, 0.0), x_prev[SUBLANES:]], axis=0)
    x_next = jnp.concatenate(
        [x_next[:q - SUBLANES], x_next[q - SUBLANES:] + jnp.where(row8 == SUBLANES - 1, next_row, 0.0)], axis=0)
    hw = 0.5 * cw_ref[...]
    hc = x_prev * hw[0:1, :] + xm * hw[1:2, :] + x_next * hw[2:3, :] + 0.5 * cb_ref[...]
    act = hc + hc * jnp.tanh(hc)
    xs = act[:, :SSD_INNER]
    xsb = xs.astype(BF16)
    bm = act[:, SSD_INNER:SSD_INNER + SSD_GN]
    cm = act[:, SSD_INNER + SSD_GN:]

    dtr = dt_ref[bb] + dtb_ref[...]
    dt = jnp.maximum(dtr, 0.0) + jnp.log(1.0 + jnp.exp(-jnp.abs(dtr)))
    a = -jnp.exp(alog_ref[...])
    acs = _cumsum_rows(dt * a, reverse) * LOG2E
    col_t = (acs - jnp.log2(dt)).T
    edge = acs[0:1, :] if reverse else acs[q - 1:q, :]
    eacs = jnp.exp2(acs)
    dtd = dt * jnp.exp2(edge - acs)
    cdec = jnp.exp2(edge)
    mask = (ii <= jj) if reverse else (ii >= jj)
    lane_h = lax.broadcasted_iota(jnp.int32, (q, LANES), 1)
    lane_g = lax.broadcasted_iota(jnp.int32, (q, gw), 1)

    def expand(v, g):
        rows = v.shape[0]
        parts = []
        for k in range(0, hpg, LANES // p):
            h0 = g * hpg + k
            lo = jnp.broadcast_to(v[:, h0:h0 + 1], (rows, LANES))
            hi = jnp.broadcast_to(v[:, h0 + 1:h0 + 2], (rows, LANES))
            parts.append(jnp.where(lane_h[:rows] < p, lo, hi))
        return jnp.concatenate(parts, axis=1)

    ys = []
    for g in range(SSD_GROUPS):
        cg32 = cm[:, g * SSD_STATE:(g + 1) * SSD_STATE]
        bg32 = bm[:, g * SSD_STATE:(g + 1) * SSD_STATE]
        cb = lax.dot_general(cg32.astype(BF16), bg32.astype(BF16), (((1,), (1,)), ((), ())),
                             preferred_element_type=F32)
        st = st_ref[bb * SSD_GROUPS + g]
        xg = xs[:, g * gw:(g + 1) * gw]
        rhs = jnp.concatenate([xsb[:, g * gw:(g + 1) * gw], st.astype(BF16)], axis=0)
        y_g = None
        for k in reversed(range(hpg)):
            h = g * hpg + k
            seg = jnp.broadcast_to(acs[:, h:h + 1], (q, q)) - jnp.broadcast_to(col_t[h:h + 1, :], (q, q))
            m_h = (cb * jnp.exp2(jnp.where(mask, seg, -jnp.inf))).astype(BF16)
            c_h = (cg32 * jnp.broadcast_to(eacs[:, h:h + 1], (q, q))).astype(BF16)
            res = jnp.dot(jnp.concatenate([m_h, c_h], axis=1), rhs, preferred_element_type=F32)
            y_g = res if y_g is None else jnp.where(lane_g < (k + 1) * p, res, y_g)
        ys.append(y_g)
        xdd = (xg * expand(dtd, g)).astype(BF16)
        new = jnp.dot(bg32.T.astype(BF16), xdd, preferred_element_type=F32)
        st_ref[bb * SSD_GROUPS + g] = st * expand(cdec, g) + new
    y = jnp.concatenate(ys, axis=1)

    if finish:
        y = yf_ref[bb] + y + xs * dsk_ref[...]
        y = y * _silu(z_ref[bb].astype(F32))
        o_ref[bb] = _rms(y, ng_ref[...]).astype(o_ref.dtype)
    else:
        o_ref[bb] = y


def _ssd_direction(p0, s0, conv_w, conv_b, dt_bias, a_log, n_lat, reverse, extra=None):
    b, n, _ = p0.shape
    q = SSD_CHUNK
    nc = n // q
    nl = n_lat // q
    halo = 16
    hpc = q // halo
    nh = n // halo
    d = 1 if reverse else 0
    nb = _pick(b, (SSD_BATCH_PER_STEP, 1))
    cidx = functools.partial(_ssd_chunk_index, nl=nl, nc=nc, reverse=reverse)
    finish = extra is not None
    in_specs = [
        pl.BlockSpec((nb, q, SSD_CONV_DIM), lambda i, s: (i, cidx(s), 0)),
        pl.BlockSpec((nb, halo, SSD_CONV_DIM), lambda i, s: (i, jnp.maximum(cidx(s) * hpc - 1, 0), 0)),
        pl.BlockSpec((nb, halo, SSD_CONV_DIM), lambda i, s: (i, jnp.minimum(cidx(s) * hpc + hpc, nh - 1), 0)),
        pl.BlockSpec((nb, q, LANES), lambda i, s: (i, cidx(s), d)),
        pl.BlockSpec((3, SSD_CONV_DIM), lambda i, s: (0, 0)),
        pl.BlockSpec((1, SSD_CONV_DIM), lambda i, s: (0, 0)),
        pl.BlockSpec((1, LANES), lambda i, s: (0, 0)),
        pl.BlockSpec((1, LANES), lambda i, s: (0, 0)),
    ]
    args = [p0, p0, p0, s0, conv_w, conv_b, dt_bias, a_log]
    if finish:
        yf, d_skip, norm_g = extra
        in_specs += [
            pl.BlockSpec((nb, q, SSD_INNER), lambda i, s: (i, cidx(s), SSD_CONV_DIM // SSD_INNER)),
            pl.BlockSpec((nb, q, SSD_INNER), lambda i, s: (i, cidx(s), 0)),
            pl.BlockSpec((1, SSD_INNER), lambda i, s: (0, 0)),
            pl.BlockSpec((1, SSD_INNER), lambda i, s: (0, 0)),
        ]
        args += [p0, yf, d_skip, norm_g]
    kern = functools.partial(_ssd_kernel, reverse=reverse, finish=finish, nl=nl, nc=nc)
    return pl.pallas_call(
        kern,
        grid=(b // nb, nc),
        in_specs=in_specs,
        out_specs=pl.BlockSpec((nb, q, SSD_INNER), lambda i, s: (i, cidx(s), 0)),
        out_shape=jax.ShapeDtypeStruct((b, n, SSD_INNER), BF16 if finish else F32),
        scratch_shapes=[pltpu.VMEM((nb * SSD_GROUPS, SSD_STATE, (SSD_HEADS // SSD_GROUPS) * SSD_HEADDIM), F32)],
        compiler_params=_params(("arbitrary", "arbitrary")),
        name="ssd_bwd_finish" if finish else "ssd_fwd",
    )(*args)


def _rope_tables(rows, rot_dim, n_ctx):
    n_freq = rot_dim // 4
    row = jnp.repeat(jnp.arange(rows, dtype=F32), GRID_W)
    col = jnp.tile(jnp.arange(GRID_W, dtype=F32), rows)
    inv = ROPE_THETA ** (-jnp.arange(n_freq, dtype=F32) / n_freq)
    ang = jnp.concatenate([row[:, None] * inv, col[:, None] * inv], axis=-1)
    cos, sin = jnp.cos(ang), jnp.sin(ang)
    reps = LANES // rot_dim
    cos_t = jnp.tile(jnp.concatenate([cos, cos], axis=-1), (1, reps))
    sin_t = jnp.tile(jnp.concatenate([-sin, sin], axis=-1), (1, reps))
    cos_t = jnp.concatenate([cos_t, jnp.ones((n_ctx, LANES), F32)], axis=0)
    sin_t = jnp.concatenate([sin_t, jnp.zeros((n_ctx, LANES), F32)], axis=0)
    return cos_t, sin_t


def _rope_block(xb, cos, sin, rot_dim):
    half = rot_dim // 2
    lane = lax.broadcasted_iota(jnp.int32, xb.shape, 1)
    first = jnp.bitwise_and(lane, rot_dim - 1) < half
    partner = jnp.where(first, pltpu.roll(xb, LANES - half, axis=1), pltpu.roll(xb, half, axis=1))
    return xb * cos + partner * sin


def _segment_mean_matrix(seg):
    sh = seg.bit_length() - 1
    i = jnp.right_shift(lax.broadcasted_iota(jnp.int32, (LANES, LANES), 0), sh)
    j = jnp.right_shift(lax.broadcasted_iota(jnp.int32, (LANES, LANES), 1), sh)
    return jnp.where(i == j, 1.0 / seg, 0.0).astype(BF16)


def _segment_mean_sq(xb, smat):
    x2 = xb * xb
    hi = x2.astype(BF16)
    lo = (x2 - hi.astype(F32)).astype(BF16)
    return (jnp.dot(hi, smat, preferred_element_type=F32) + jnp.dot(lo, smat, preferred_element_type=F32))


def _store_key_norms(ksq_ref, c, sumsq):
    mx = jnp.max(sumsq, axis=0, keepdims=True)
    ksq_ref[0, 0, :, (2 * c) * LANES:(2 * c + 1) * LANES] = mx
    ksq_ref[0, 0, :, (2 * c + 1) * LANES:(2 * c + 2) * LANES] = pltpu.roll(mx, LANES // 2, axis=1)


def _qk_prep_kernel(q_ref, k_ref, v_ref, cos_ref, sin_ref, qg_ref, kg_ref, qo_ref, ko_ref, vo_ref, ksq_ref, *, scale):
    smat = _segment_mean_matrix(ATT_HEADDIM)
    cos = cos_ref[...]
    sin = sin_ref[...]
    hpb = LANES // ATT_HEADDIM

    def prep(src_ref, g_ref, dst_ref, n_heads, mul, transposed):
        for c in range(n_heads // hpb):
            xb = src_ref[0, :, c * LANES:(c + 1) * LANES].astype(F32)
            xn = xb * lax.rsqrt(_segment_mean_sq(xb, smat) + EPS) * g_ref[...]
            r = _rope_block(xn, cos, sin, ATT_HEADDIM)
            if mul != 1.0:
                r = r * mul
            if transposed:
                r = r.T
            else:
                _store_key_norms(ksq_ref, c, _segment_mean_sq(r, smat) * float(ATT_HEADDIM))
            r = r.astype(dst_ref.dtype)
            for k in range(hpb):
                if transposed:
                    dst_ref[0, c * hpb + k] = r[k * ATT_HEADDIM:(k + 1) * ATT_HEADDIM, :]
                else:
                    dst_ref[0, c * hpb + k] = r[:, k * ATT_HEADDIM:(k + 1) * ATT_HEADDIM]

    prep(q_ref, qg_ref, qo_ref, ATT_HEADS, scale, True)
    prep(k_ref, kg_ref, ko_ref, ATT_KV_HEADS, 1.0, False)
    for c in range(ATT_KV_HEADS // hpb):
        vt = v_ref[0, :, c * LANES:(c + 1) * LANES].astype(F32).T.astype(vo_ref.dtype)
        for k in range(hpb):
            vo_ref[0, c * hpb + k] = vt[k * ATT_HEADDIM:(k + 1) * ATT_HEADDIM, :]


def _qk_prep(p0, cos_t, sin_t, q_g, k_g, q_col, k_col, v_col):
    b, n, _ = p0.shape
    tm = _pick(n, (768, 384, 256, 128))
    kern = functools.partial(_qk_prep_kernel, scale=ATT_HEADDIM ** -0.5 * LOG2E)
    return pl.pallas_call(
        kern,
        grid=(b, n // tm),
        in_specs=[
            pl.BlockSpec((1, tm, ATT_Q), lambda i, t: (i, t, q_col // ATT_Q)),
            pl.BlockSpec((1, tm, ATT_KV), lambda i, t: (i, t, k_col // ATT_KV)),
            pl.BlockSpec((1, tm, ATT_KV), lambda i, t: (i, t, v_col // ATT_KV)),
            pl.BlockSpec((tm, LANES), lambda i, t: (t, 0)),
            pl.BlockSpec((tm, LANES), lambda i, t: (t, 0)),
            pl.BlockSpec((1, LANES), lambda i, t: (0, 0)),
            pl.BlockSpec((1, LANES), lambda i, t: (0, 0)),
        ],
        out_specs=[
            pl.BlockSpec((1, ATT_HEADS, ATT_HEADDIM, tm), lambda i, t: (i, 0, 0, t)),
            pl.BlockSpec((1, ATT_KV_HEADS, tm, ATT_HEADDIM), lambda i, t: (i, 0, t, 0)),
            pl.BlockSpec((1, ATT_KV_HEADS, ATT_HEADDIM, tm), lambda i, t: (i, 0, 0, t)),
            pl.BlockSpec((1, 1, 1, ATT_KV_HEADS * LANES), lambda i, t: (i, t, 0, 0)),
        ],
        out_shape=[
            jax.ShapeDtypeStruct((b, ATT_HEADS, ATT_HEADDIM, n), BF16),
            jax.ShapeDtypeStruct((b, ATT_KV_HEADS, n, ATT_HEADDIM), BF16),
            jax.ShapeDtypeStruct((b, ATT_KV_HEADS, ATT_HEADDIM, n), BF16),
            jax.ShapeDtypeStruct((b, n // tm, 1, ATT_KV_HEADS * LANES), F32),
        ],
        compiler_params=_params(("arbitrary", "arbitrary")),
        name="gqa_qk_prep",
    )(p0, p0, p0, cos_t, sin_t, q_g, k_g)


def _attn_kernel(qt_ref, k_ref, vt_ref, ksq_ref, o_ref, s_scr, m_scr, e_scr, gap_scr, ot_scr, *,
                 group, hpb, n_lat, n_all, lat_tiles, ctx_self):
    t = pl.program_id(2)
    dv = vt_ref.shape[2]
    tq = qt_ref.shape[3]
    n_blocks = qt_ref.shape[1] // hpb

    ksq_max = jnp.max(ksq_ref[0], axis=0)

    def queries(i):
        return jnp.concatenate([qt_ref[0, i * hpb + j] for j in range(hpb)], axis=1)

    def finish(i, ot):
        ot = ot[:dv] * (1.0 / ot[dv:dv + 1])
        for j in range(hpb):
            ot_scr[(i * hpb + j) * dv:(i * hpb + j + 1) * dv, :] = ot[:, j * tq:(j + 1) * tq]

    def run(k_lo, k_len, exact):
        row = lax.broadcasted_iota(jnp.int32, (BF16_ROWS, k_len), 0)
        ones_row = jnp.where(row == 0, 1.0, 0.0).astype(BF16)

        def scores(i):
            kv = i * hpb // group
            qt = queries(i)
            s = jnp.dot(k_ref[0, kv, k_lo:k_lo + k_len, :], qt, preferred_element_type=F32)
            m = jnp.max(s, axis=0, keepdims=True)
            if exact:
                s_scr[i % 2, :k_len, :] = s
                m_scr[i % 2] = m
            else:
                qf = qt.astype(F32)
                ksq = ksq_max[:, kv * LANES:kv * LANES + 1]
                bound = jnp.sqrt(jnp.sum(qf * qf, axis=0, keepdims=True) * ksq) * ATTN_BOUND_SLACK
                e_scr[i % 2, :k_len, :] = jnp.exp2(s - bound).astype(BF16)
                gap_scr[i] = bound - m

        def combine(i):
            if exact:
                e = jnp.exp2(s_scr[i % 2, :k_len, :] - m_scr[i % 2]).astype(BF16)
            else:
                e = e_scr[i % 2, :k_len, :]
            vt = vt_ref[0, i * hpb // group, :, k_lo:k_lo + k_len]
            finish(i, jnp.dot(jnp.concatenate([vt, ones_row], axis=0), e, preferred_element_type=F32))

        scores(0)
        for i in range(n_blocks):
            if i + 1 < n_blocks:
                scores(i + 1)
            combine(i)

    def attend(k_lo, k_len):
        run(k_lo, k_len, exact=False)

        @pl.when(jnp.max(gap_scr[...]) > ATTN_MAX_SHIFT)
        def _():
            run(k_lo, k_len, exact=True)

        o_ref[0] = ot_scr[...].T.astype(o_ref.dtype)

    if ctx_self:
        @pl.when(t < lat_tiles)
        def _():
            attend(0, n_all)

        @pl.when(t >= lat_tiles)
        def _():
            attend(n_lat, n_all - n_lat)
    else:
        attend(0, n_all)


def _attention(qt, k, vt, ksq, n_lat, ctx_self):
    b, h, dqk, nq = qt.shape
    hkv, dv, n_all = vt.shape[1], vt.shape[2], vt.shape[3]
    group = h // hkv
    if ctx_self:
        tq = _pick(math.gcd(n_lat, n_all - n_lat), (256, 128))
    else:
        tq = _pick(n_lat, (ATTN_BLOCK_LANES, 256, 128))
    hpb = max(1, min(group, ATTN_BLOCK_LANES // tq))
    heads = min(h, ATTN_BLOCKS_PER_STEP * hpb)
    kvb = max(heads // group, 1)
    assert nq == (n_all if ctx_self else n_lat)
    kern = functools.partial(_attn_kernel, group=group, hpb=hpb, n_lat=n_lat, n_all=n_all,
                             lat_tiles=n_lat // tq, ctx_self=ctx_self)
    return pl.pallas_call(
        kern,
        grid=(b, h // heads, nq // tq),
        in_specs=[
            pl.BlockSpec((1, heads, dqk, tq), lambda i, hb, t: (i, hb, 0, t)),
            pl.BlockSpec((1, kvb, n_all, dqk), lambda i, hb, t: (i, hb, 0, 0)),
            pl.BlockSpec((1, kvb, dv, n_all), lambda i, hb, t: (i, hb, 0, 0)),
            pl.BlockSpec((1, ksq.shape[1], 1, kvb * LANES), lambda i, hb, t: (i, 0, 0, hb)),
        ],
        out_specs=pl.BlockSpec((1, tq, heads * dv), lambda i, hb, t: (i, t, hb)),
        out_shape=jax.ShapeDtypeStruct((b, nq, h * dv), BF16),
        scratch_shapes=[pltpu.VMEM((2, n_all, tq * hpb), F32), pltpu.VMEM((2, 1, tq * hpb), F32),
                        pltpu.VMEM((2, n_all, tq * hpb), BF16), pltpu.VMEM((heads // hpb, 1, tq * hpb), F32),
                        pltpu.VMEM((heads * dv, tq), F32)],
        compiler_params=_params(("arbitrary", "arbitrary", "arbitrary")),
        name="attention",
    )(qt, k, vt, ksq)


def _hidden_chunks(hidden):
    chunks, lo = [], 0
    while lo < hidden:
        w = min(1024, hidden - lo)
        chunks.append((lo, w))
        lo += w
    return chunks


def _mix_ffn_kernel(*refs, n_in, tm, n_lat, final):
    x_ref, ml_ref, mc_ref, g_ref = refs[:4]
    a_refs = refs[4:4 + n_in]
    w_refs = refs[4 + n_in:4 + 2 * n_in]
    wg_ref, wu_ref, wd_ref, fg_ref, o_ref = refs[4 + 2 * n_in:]
    t = pl.program_id(1)
    is_lat = _is_lat(t, tm, n_lat)
    mod = functools.partial(_mod_row, ml_ref, mc_ref, is_lat=is_lat)
    mix = jnp.dot(a_refs[0][0], w_refs[0][...], preferred_element_type=F32)
    for a_ref, w_ref in zip(a_refs[1:], w_refs[1:]):
        mix = mix + jnp.dot(a_ref[0], w_ref[...], preferred_element_type=F32)
    x = x_ref[0] + mod(2) * mix
    h = (_rms(x, g_ref[...]) * (1.0 + mod(4)) + mod(3)).astype(BF16)
    acc = None
    for lo, w in _hidden_chunks(wd_ref.shape[0]):
        gate = jnp.dot(h, wg_ref[:, lo:lo + w], preferred_element_type=F32)
        up = jnp.dot(h, wu_ref[:, lo:lo + w], preferred_element_type=F32)
        a = (_silu(gate) * up).astype(BF16)
        part = jnp.dot(a, wd_ref[lo:lo + w, :], preferred_element_type=F32)
        acc = part if acc is None else acc + part
    y = x + mod(5) * acc
    if final:
        y = _rms(y, fg_ref[...])
    o_ref[0] = y


def _mix_ffn(xa, mods_l, mods_c, g, acts, weights, w_gate, w_up, w_down, final_g, n_rows, n_lat, final):
    b, _, d = xa.shape
    hid = w_down.shape[0]
    tm = _pick(n_rows, (576, 512, 384, 256, 128))
    n_in = len(acts)
    kern = functools.partial(_mix_ffn_kernel, n_in=n_in, tm=tm, n_lat=n_lat, final=final)
    resident = dict(pipeline_mode=pl.Buffered(1))
    in_specs = [
        pl.BlockSpec((1, tm, d), lambda i, t: (i, t, 0)),
        pl.BlockSpec((1, ADA_CHUNKS, d), lambda i, t: (i, 0, 0)),
        pl.BlockSpec((ADA_CHUNKS, d), lambda i, t: (0, 0)),
        pl.BlockSpec((1, d), lambda i, t: (0, 0)),
    ]
    in_specs += [pl.BlockSpec((1, tm, a.shape[2]), lambda i, t: (i, t, 0)) for a in acts]
    in_specs += [pl.BlockSpec(w.shape, lambda i, t: (0, 0), **resident) for w in weights]
    in_specs += [
        pl.BlockSpec((d, hid), lambda i, t: (0, 0), **resident),
        pl.BlockSpec((d, hid), lambda i, t: (0, 0), **resident),
        pl.BlockSpec((hid, d), lambda i, t: (0, 0), **resident),
        pl.BlockSpec((1, d), lambda i, t: (0, 0)),
    ]
    return pl.pallas_call(
        kern,
        grid=(b, n_rows // tm),
        in_specs=in_specs,
        out_specs=pl.BlockSpec((1, tm, d), lambda i, t: (i, t, 0)),
        out_shape=jax.ShapeDtypeStruct((b, n_rows, d), F32),
        compiler_params=_params(("arbitrary", "arbitrary")),
        name="mix_ffn_final" if final else "mix_ffn",
    )(xa, mods_l, mods_c, g, *acts, *weights, w_gate, w_up, w_down, final_g)


def _mla_q_kernel(p_ref, g_ref, w_ref, cos_ref, sin_ref, o_ref, *, q_rank, scale):
    cq = p_ref[0, :, :q_rank].astype(F32)
    qn = _rms(cq, g_ref[...]).astype(BF16)
    qq = jnp.dot(qn, w_ref[...], preferred_element_type=F32)
    qq = qq * scale
    n_nope = MLA_HEADS * MLA_NOPE
    ppb = LANES // MLA_ROPE
    npb = LANES // MLA_NOPE
    pes_t = [_rope_block(qq[:, n_nope + c * LANES:n_nope + (c + 1) * LANES], cos_ref[...], sin_ref[...], MLA_ROPE).T
             for c in range(MLA_HEADS // ppb)]
    for c in range(MLA_HEADS // npb):
        nope_t = qq[:, c * LANES:(c + 1) * LANES].T
        for k in range(npb):
            h = c * npb + k
            pe_t = pes_t[h // ppb][(h % ppb) * MLA_ROPE:(h % ppb + 1) * MLA_ROPE, :]
            qh = jnp.concatenate([nope_t[k * MLA_NOPE:(k + 1) * MLA_NOPE, :], pe_t], axis=0)
            o_ref[0, h] = qh.astype(o_ref.dtype)


def _mla_q(p1, g, w_uq, cos_t, sin_t, n_lat, q_rank):
    b = p1.shape[0]
    cols = p1.shape[2]
    tm = _pick(n_lat, (512, 256, 128))
    kern = functools.partial(_mla_q_kernel, q_rank=q_rank, scale=MLA_QK ** -0.5 * LOG2E)
    return pl.pallas_call(
        kern,
        grid=(b, n_lat // tm),
        in_specs=[
            pl.BlockSpec((1, tm, cols), lambda i, t: (i, t, 0)),
            pl.BlockSpec((1, q_rank), lambda i, t: (0, 0)),
            pl.BlockSpec(w_uq.shape, lambda i, t: (0, 0)),
            pl.BlockSpec((tm, LANES), lambda i, t: (t, 0)),
            pl.BlockSpec((tm, LANES), lambda i, t: (t, 0)),
        ],
        out_specs=pl.BlockSpec((1, MLA_HEADS, MLA_QK, tm), lambda i, t: (i, 0, 0, t)),
        out_shape=jax.ShapeDtypeStruct((b, MLA_HEADS, MLA_QK, n_lat), BF16),
        compiler_params=_params(("arbitrary", "arbitrary")),
        name="mla_q",
    )(p1, g, w_uq, cos_t, sin_t)


def _mla_kv_kernel(p_ref, pe_ref, g_ref, w_ref, cos_ref, sin_ref, ko_ref, vo_ref, ksq_ref, *, q_rank):
    ckv = p_ref[0, :, q_rank:].astype(F32)
    kvn = _rms(ckv, g_ref[...]).astype(BF16)
    kv = jnp.dot(kvn, w_ref[...], preferred_element_type=F32)
    pe_block = _rope_block(pe_ref[0], cos_ref[...], sin_ref[...], MLA_ROPE)
    pe = pe_block[:, :MLA_ROPE]
    n_nope = MLA_HEADS * MLA_NOPE
    for h in range(MLA_HEADS):
        kh = jnp.concatenate([kv[:, h * MLA_NOPE:(h + 1) * MLA_NOPE], pe], axis=1)
        ko_ref[0, h] = kh.astype(ko_ref.dtype)
    pe_sq = jnp.sum(pe_block * pe_block, axis=1, keepdims=True)
    smat = _segment_mean_matrix(MLA_NOPE)
    for c in range(MLA_HEADS * MLA_NOPE // LANES):
        nope_sq = _segment_mean_sq(kv[:, c * LANES:(c + 1) * LANES], smat) * float(MLA_NOPE)
        _store_key_norms(ksq_ref, c, nope_sq + pe_sq)
    hpb = LANES // MLA_V
    for c in range(MLA_HEADS // hpb):
        v_t = kv[:, n_nope + c * LANES:n_nope + (c + 1) * LANES].T.astype(vo_ref.dtype)
        for k in range(hpb):
            vo_ref[0, c * hpb + k] = v_t[k * MLA_V:(k + 1) * MLA_V, :]


def _mla_kv(p1, s1, g, w_ukv, cos_t, sin_t, q_rank):
    b, n, cols = p1.shape
    kv_rank = cols - q_rank
    tm = _pick(n, (768, 384, 256, 128))
    kern = functools.partial(_mla_kv_kernel, q_rank=q_rank)
    return pl.pallas_call(
        kern,
        grid=(b, n // tm),
        in_specs=[
            pl.BlockSpec((1, tm, cols), lambda i, t: (i, t, 0)),
            pl.BlockSpec((1, tm, LANES), lambda i, t: (i, t, 0)),
            pl.BlockSpec((1, kv_rank), lambda i, t: (0, 0)),
            pl.BlockSpec(w_ukv.shape, lambda i, t: (0, 0)),
            pl.BlockSpec((tm, LANES), lambda i, t: (t, 0)),
            pl.BlockSpec((tm, LANES), lambda i, t: (t, 0)),
        ],
        out_specs=[
            pl.BlockSpec((1, MLA_HEADS, tm, MLA_QK), lambda i, t: (i, 0, t, 0)),
            pl.BlockSpec((1, MLA_HEADS, MLA_V, tm), lambda i, t: (i, 0, 0, t)),
            pl.BlockSpec((1, 1, 1, MLA_HEADS * LANES), lambda i, t: (i, t, 0, 0)),
        ],
        out_shape=[
            jax.ShapeDtypeStruct((b, MLA_HEADS, n, MLA_QK), BF16),
            jax.ShapeDtypeStruct((b, MLA_HEADS, MLA_V, n), BF16),
            jax.ShapeDtypeStruct((b, n // tm, 1, MLA_HEADS * LANES), F32),
        ],
        compiler_params=_params(("arbitrary", "arbitrary")),
        name="mla_kv",
    )(p1, s1, g, w_ukv, cos_t, sin_t)


def _pad_lanes(v, width=LANES):
    v = v.reshape(1, -1).astype(F32)
    return jnp.pad(v, ((0, 0), (0, width - v.shape[1])))


def kernel(x, c, ctx, c_ctx, ada_w, ada_b, norm1_g, norm2_g, ffn_w_up, ffn_w_down, ab_w_in, ab_w_out, ssd_conv_w, ssd_conv_b, ssd_a_log, ssd_dt_bias, ssd_d, ssd_norm_g, att_q_g, att_k_g, mla_w_in, mla_q_norm_g, mla_w_uq, mla_kv_norm_g, mla_w_ukv, mla_w_o, final_norm_g):
    b, t_lat, d = x.shape
    n_ctx = ctx.shape[1]
    rows = t_lat // GRID_W
    hid = ffn_w_down.shape[1]
    row2 = lambda v: v.reshape(1, -1).astype(F32)

    bp = -(-(b + 1) // 8) * 8
    cvec = jnp.concatenate([c, c_ctx[None, :], jnp.zeros((bp - b - 1, d), F32)], axis=0)
    mods = _ada_mods(cvec, ada_w, ada_b)
    mods_l = [mods[i, :b].reshape(b, ADA_CHUNKS, d) for i in range(2)]
    mods_c = [mods[i, b].reshape(ADA_CHUNKS, d) for i in range(2)]

    xa = jnp.concatenate([x, ctx], axis=1)

    w_in = ab_w_in[0]
    o_xbc, o_dt = SSD_INNER, SSD_INNER + SSD_CONV_DIM
    o_att = o_dt + 2 * SSD_HEADS
    w_main = jnp.concatenate([w_in[:, o_xbc:o_dt], w_in[:, :SSD_INNER], w_in[:, o_att:]], axis=1).astype(BF16)
    w_dt = jnp.zeros((d, 2 * LANES), F32)
    w_dt = w_dt.at[:, :SSD_HEADS].set(w_in[:, o_dt:o_dt + SSD_HEADS])
    w_dt = w_dt.at[:, LANES:LANES + SSD_HEADS].set(w_in[:, o_dt + SSD_HEADS:o_att]).astype(BF16)
    p0, s0 = _in_proj(xa, mods_l[0], mods_c[0], row2(norm1_g[0]), w_main, w_dt, t_lat)

    conv_w = ssd_conv_w[0].astype(F32)
    conv_b = row2(ssd_conv_b[0])
    yf = _ssd_direction(p0, s0, conv_w, conv_b, _pad_lanes(ssd_dt_bias[0, 0]), _pad_lanes(ssd_a_log[0, 0]),
                        t_lat, reverse=False)
    d_skip = jnp.repeat(ssd_d[0].astype(F32), SSD_HEADDIM).reshape(1, SSD_INNER)
    y_ssd = _ssd_direction(p0, s0, conv_w, conv_b, _pad_lanes(ssd_dt_bias[0, 1]), _pad_lanes(ssd_a_log[0, 1]),
                           t_lat, reverse=True, extra=(yf, d_skip, row2(ssd_norm_g[0])))

    q_col = SSD_CONV_DIM + SSD_INNER
    k_col = q_col + ATT_Q
    cos_a, sin_a = _rope_tables(rows, ATT_HEADDIM, n_ctx)
    hpb = LANES // ATT_HEADDIM
    qh, kh, vh, ksq0 = _qk_prep(p0, cos_a, sin_a, jnp.tile(row2(att_q_g[0]), (1, hpb)),
                                jnp.tile(row2(att_k_g[0]), (1, hpb)), q_col, k_col, k_col + ATT_KV)
    o_att_l0 = _attention(qh, kh, vh, ksq0, t_lat, ctx_self=True)

    w_out = ab_w_out[0].astype(BF16)
    w_up = ffn_w_up[0].astype(BF16)
    xa = _mix_ffn(xa, mods_l[0], mods_c[0], row2(norm2_g[0]), [y_ssd, o_att_l0],
                  [w_out[:SSD_INNER], w_out[SSD_INNER:]], w_up[:, :hid], w_up[:, hid:],
                  ffn_w_down[0].astype(BF16), row2(final_norm_g), t_lat + n_ctx, t_lat, final=False)

    q_rank = mla_q_norm_g.shape[1]
    kv_rank = mla_kv_norm_g.shape[1]
    w_in1 = mla_w_in[0]
    w_pe = jnp.pad(w_in1[:, q_rank + kv_rank:], ((0, 0), (0, LANES - MLA_ROPE))).astype(BF16)
    p1, s1 = _in_proj(xa, mods_l[1], mods_c[1], row2(norm1_g[1]), w_in1[:, :q_rank + kv_rank].astype(BF16), w_pe,
                      t_lat)
    cos_m, sin_m = _rope_tables(rows, MLA_ROPE, n_ctx)
    w_uq = mla_w_uq[0].reshape(q_rank, MLA_HEADS, MLA_QK)
    w_uq = jnp.concatenate([w_uq[:, :, :MLA_NOPE].reshape(q_rank, -1), w_uq[:, :, MLA_NOPE:].reshape(q_rank, -1)],
                           axis=1).astype(BF16)
    w_ukv = mla_w_ukv[0].reshape(kv_rank, MLA_HEADS, MLA_NOPE + MLA_V)
    w_ukv = jnp.concatenate([w_ukv[:, :, :MLA_NOPE].reshape(kv_rank, -1), w_ukv[:, :, MLA_NOPE:].reshape(kv_rank, -1)],
                            axis=1).astype(BF16)
    qm = _mla_q(p1, row2(mla_q_norm_g[0]), w_uq, cos_m, sin_m, t_lat, q_rank)
    km, vm, ksq1 = _mla_kv(p1, s1, row2(mla_kv_norm_g[0]), w_ukv, cos_m, sin_m, q_rank)
    o_mla = _attention(qm, km, vm, ksq1, t_lat, ctx_self=False)

    w_up = ffn_w_up[1].astype(BF16)
    return _mix_ffn(xa, mods_l[1], mods_c[1], row2(norm2_g[1]), [o_mla], [mla_w_o[0].astype(BF16)],
                    w_up[:, :hid], w_up[:, hid:], ffn_w_down[1].astype(BF16), row2(final_norm_g),
                    t_lat, t_lat, final=True)
```

```python
import functools
import math

import jax
import jax.numpy as jnp
from jax import lax
from jax.experimental import pallas as pl
from jax.experimental.pallas import tpu as pltpu

F32 = jnp.float32
BF16 = jnp.bfloat16

EPS = 1e-6
ROPE_THETA = 10000.0
GRID_W = 64
ADA_CHUNKS = 6

SSD_HEADS = 16
SSD_HEADDIM = 64
SSD_GROUPS = 4
SSD_STATE = 128
SSD_CHUNK = 128
SSD_BATCH_PER_STEP = 4
SSD_INNER = SSD_HEADS * SSD_HEADDIM
SSD_GN = SSD_GROUPS * SSD_STATE
SSD_CONV_DIM = SSD_INNER + 2 * SSD_GN

ATT_HEADS = 16
ATT_KV_HEADS = 4
ATT_HEADDIM = 64
ATT_Q = ATT_HEADS * ATT_HEADDIM
ATT_KV = ATT_KV_HEADS * ATT_HEADDIM

MLA_HEADS = 16
MLA_NOPE = 64
MLA_ROPE = 32
MLA_V = 64
MLA_QK = MLA_NOPE + MLA_ROPE

LANES = 128
SUBLANES = 8
BF16_ROWS = 16
ATTN_BLOCKS_PER_STEP = 8
ATTN_BOUND_SLACK = 1.0 + 2.0 ** -6
ATTN_MAX_SHIFT = 64.0
ATTN_BLOCK_LANES = 512
LOG2E = math.log2(math.e)
VMEM_LIMIT = 56 * 1024 * 1024


def _pick(n, candidates):
    for c in candidates:
        if n % c == 0:
            return c
    raise ValueError(f"no tile for {n} in {candidates}")


def _params(sem):
    return pltpu.CompilerParams(dimension_semantics=sem, vmem_limit_bytes=VMEM_LIMIT)


def _silu(x):
    hx = 0.5 * x
    return hx + hx * jnp.tanh(hx)


def _rms(x, g):
    ms = jnp.mean(x * x, axis=-1, keepdims=True)
    return x * lax.rsqrt(ms + EPS) * g


def _mod_row(ml_ref, mc_ref, idx, is_lat):
    return jnp.where(is_lat, ml_ref[0, idx:idx + 1, :], mc_ref[idx:idx + 1, :])


def _is_lat(t, tm, n_lat):
    rows = t * tm + lax.broadcasted_iota(jnp.int32, (tm, 1), 0)
    return rows < n_lat


def _ada_kernel(c_ref, w_ref, b_ref, o_ref):
    s = _silu(c_ref[...]).astype(BF16)
    o_ref[0] = jnp.dot(s, w_ref[0].astype(BF16), preferred_element_type=F32) + b_ref[0]


def _ada_mods(cvec, ada_w, ada_b):
    depth, d, n = ada_w.shape
    bp = cvec.shape[0]
    tn = _pick(n, (1536, 1024, 512, 256, 128))
    return pl.pallas_call(
        _ada_kernel,
        grid=(depth, n // tn),
        in_specs=[
            pl.BlockSpec((bp, d), lambda i, j: (0, 0)),
            pl.BlockSpec((1, d, tn), lambda i, j: (i, 0, j)),
            pl.BlockSpec((1, 1, tn), lambda i, j: (i, 0, j)),
        ],
        out_specs=pl.BlockSpec((1, bp, tn), lambda i, j: (i, 0, j)),
        out_shape=jax.ShapeDtypeStruct((depth, bp, n), F32),
        compiler_params=_params(("arbitrary", "arbitrary")),
        name="ada_mods",
    )(cvec, ada_w, ada_b.reshape(depth, 1, n))


def _in_proj_kernel(*refs, n_src, lat_tiles, emit_rows):
    srcs = refs[:n_src]
    ml_ref, mc_ref, g_ref, w_ref, ws_ref = refs[n_src:n_src + 5]
    o_ref, os_ref = refs[n_src + 5:n_src + 7]
    t = pl.program_id(1)
    tm = o_ref.shape[1]
    rc = _pick(tm, (128,))
    lat = (t + lax.broadcasted_iota(jnp.int32, (rc, 1), 0) * 0) < lat_tiles
    shift = _mod_row(ml_ref, mc_ref, 0, lat)
    scale = _mod_row(ml_ref, mc_ref, 1, lat)
    for r in range(0, tm, rc):
        xin = srcs[0][0, r:r + rc, :]
        if n_src == 2:
            xin = jnp.where(lat, xin, srcs[1][0, r:r + rc, :])
        if emit_rows:
            refs[n_src + 7][0, r:r + rc, :] = xin
        hb = (_rms(xin, g_ref[...]) * (1.0 + scale) + shift).astype(BF16)
        os_ref[0, r:r + rc, :] = jnp.dot(hb, ws_ref[...], preferred_element_type=F32)
        o_ref[0, r:r + rc, :] = jnp.dot(hb, w_ref[...], preferred_element_type=F32).astype(o_ref.dtype)


def _in_proj(srcs, mods_l, mods_c, g, w_main, w_side, n_lat, n_ctx):
    b, _, d = srcs[0].shape
    n = n_lat + n_ctx
    nm = w_main.shape[1]
    ns = w_side.shape[1]
    tm = _pick(math.gcd(n_lat, n_ctx), (256, 128))
    lat_tiles = n_lat // tm
    split = len(srcs) == 2
    kern = functools.partial(_in_proj_kernel, n_src=len(srcs), lat_tiles=lat_tiles, emit_rows=split)
    if split:
        src_specs = [pl.BlockSpec((1, tm, d), lambda i, t: (i, jnp.minimum(t, lat_tiles - 1), 0)),
                     pl.BlockSpec((1, tm, d), lambda i, t: (i, jnp.maximum(t - lat_tiles, 0), 0))]
    else:
        src_specs = [pl.BlockSpec((1, tm, d), lambda i, t: (i, t, 0))]
    resident = dict(pipeline_mode=pl.Buffered(1))
    out_specs = [pl.BlockSpec((1, tm, nm), lambda i, t: (i, t, 0)),
                 pl.BlockSpec((1, tm, ns), lambda i, t: (i, t, 0))]
    out_shape = [jax.ShapeDtypeStruct((b, n, nm), BF16), jax.ShapeDtypeStruct((b, n, ns), F32)]
    if split:
        out_specs.append(pl.BlockSpec((1, tm, d), lambda i, t: (i, t, 0)))
        out_shape.append(jax.ShapeDtypeStruct((b, n, d), F32))
    return pl.pallas_call(
        kern,
        grid=(b, n // tm),
        in_specs=src_specs + [
            pl.BlockSpec((1, ADA_CHUNKS, d), lambda i, t: (i, 0, 0)),
            pl.BlockSpec((ADA_CHUNKS, d), lambda i, t: (0, 0)),
            pl.BlockSpec((1, d), lambda i, t: (0, 0)),
            pl.BlockSpec((d, nm), lambda i, t: (0, 0), **resident),
            pl.BlockSpec((d, ns), lambda i, t: (0, 0), **resident),
        ],
        out_specs=out_specs,
        out_shape=out_shape,
        compiler_params=_params(("arbitrary", "arbitrary")),
        name="in_proj",
    )(*srcs, mods_l, mods_c, g, w_main, w_side)


def _cumsum_rows(v, reverse):
    n = v.shape[0]
    row = lax.broadcasted_iota(jnp.int32, v.shape, 0)
    k = 1
    while k < n:
        if reverse:
            v = v + jnp.where(row < n - k, pltpu.roll(v, n - k, axis=0), 0.0)
        else:
            v = v + jnp.where(row >= k, pltpu.roll(v, k, axis=0), 0.0)
        k *= 2
    return v


def _ssd_chunk_index(s, nl, nc, reverse):
    if reverse:
        return nc - 1 - s
    return lax.rem(s + nl, nc)


def _ssd_kernel(*refs, reverse, finish, nl, nc):
    st_ref = refs[-1]

    @pl.when(pl.program_id(1) == 0)
    def _():
        st_ref[...] = jnp.zeros_like(st_ref)

    for bb in range(refs[0].shape[0]):
        _ssd_chunk(bb, *refs, reverse=reverse, finish=finish, nl=nl, nc=nc)


def _ssd_chunk(bb, *refs, reverse, finish, nl, nc):
    if finish:
        (xm_ref, xp_ref, xn_ref, dt_ref, cw_ref, cb_ref, dtb_ref, alog_ref,
         z_ref, yf_ref, dsk_ref, ng_ref, o_ref, st_ref) = refs
    else:
        (xm_ref, xp_ref, xn_ref, dt_ref, cw_ref, cb_ref, dtb_ref, alog_ref, o_ref, st_ref) = refs
    q = SSD_CHUNK
    p = SSD_HEADDIM
    hpg = SSD_HEADS // SSD_GROUPS
    gw = hpg * p
    chunk = _ssd_chunk_index(pl.program_id(1), nl, nc, reverse)

    xmb = xm_ref[bb]
    xm = xmb.astype(F32)
    halo = xp_ref.shape[1]
    seq_first = jnp.logical_or(chunk == 0, chunk == nl)
    seq_last = jnp.logical_or(chunk == nl - 1, chunk == nc - 1)
    prev_row = xp_ref[bb, halo - 1:halo, :].astype(F32) * jnp.where(seq_first, 0.0, 1.0)
    next_row = xn_ref[bb, 0:1, :].astype(F32) * jnp.where(seq_last, 0.0, 1.0)
    ii = lax.broadcasted_iota(jnp.int32, (q, q), 0)
    jj = lax.broadcasted_iota(jnp.int32, (q, q), 1)
    shift_dn = jnp.where(ii == jj + 1, 1.0, 0.0).astype(BF16)
    shift_up = jnp.where(ii + 1 == jj, 1.0, 0.0).astype(BF16)
    x_prev = jnp.dot(shift_dn, xmb, preferred_element_type=F32)
    x_next = jnp.dot(shift_up, xmb, preferred_element_type=F32)
    row8 = lax.broadcasted_iota(jnp.int32, (SUBLANES, 1), 0)
    x_prev = jnp.concatenate(
        [x_prev[:SUBLANES] + jnp.where(row8 == 0, prev_row, 0.0), x_prev[SUBLANES:]], axis=0)
    x_next = jnp.concatenate(
        [x_next[:q - SUBLANES], x_next[q - SUBLANES:] + jnp.where(row8 == SUBLANES - 1, next_row, 0.0)], axis=0)
    hw = 0.5 * cw_ref[...]
    hc = x_prev * hw[0:1, :] + xm * hw[1:2, :] + x_next * hw[2:3, :] + 0.5 * cb_ref[...]
    act = hc + hc * jnp.tanh(hc)
    xs = act[:, :SSD_INNER]
    xsb = xs.astype(BF16)
    bm = act[:, SSD_INNER:SSD_INNER + SSD_GN]
    cm = act[:, SSD_INNER + SSD_GN:]

    dtr = dt_ref[bb] + dtb_ref[...]
    dt = jnp.maximum(dtr, 0.0) + jnp.log(1.0 + jnp.exp(-jnp.abs(dtr)))
    a = -jnp.exp(alog_ref[...])
    acs = _cumsum_rows(dt * a, reverse) * LOG2E
    col_t = (acs - jnp.log2(dt)).T
    edge = acs[0:1, :] if reverse else acs[q - 1:q, :]
    eacs = jnp.exp2(acs)
    dtd = dt * jnp.exp2(edge - acs)
    cdec = jnp.exp2(edge)
    mask = (ii <= jj) if reverse else (ii >= jj)
    lane_h = lax.broadcasted_iota(jnp.int32, (q, LANES), 1)
    lane_g = lax.broadcasted_iota(jnp.int32, (q, gw), 1)

    def expand(v, g):
        rows = v.shape[0]
        parts = []
        for k in range(0, hpg, LANES // p):
            h0 = g * hpg + k
            lo = jnp.broadcast_to(v[:, h0:h0 + 1], (rows, LANES))
            hi = jnp.broadcast_to(v[:, h0 + 1:h0 + 2], (rows, LANES))
            parts.append(jnp.where(lane_h[:rows] < p, lo, hi))
        return jnp.concatenate(parts, axis=1)

    ys = []
    for g in range(SSD_GROUPS):
        cg32 = cm[:, g * SSD_STATE:(g + 1) * SSD_STATE]
        bg32 = bm[:, g * SSD_STATE:(g + 1) * SSD_STATE]
        cb = lax.dot_general(cg32.astype(BF16), bg32.astype(BF16), (((1,), (1,)), ((), ())),
                             preferred_element_type=F32)
        st = st_ref[bb * SSD_GROUPS + g]
        xg = xs[:, g * gw:(g + 1) * gw]
        rhs = jnp.concatenate([xsb[:, g * gw:(g + 1) * gw], st.astype(BF16)], axis=0)
        y_g = None
        for k in reversed(range(hpg)):
            h = g * hpg + k
            seg = jnp.broadcast_to(acs[:, h:h + 1], (q, q)) - jnp.broadcast_to(col_t[h:h + 1, :], (q, q))
            m_h = (cb * jnp.exp2(jnp.where(mask, seg, -jnp.inf))).astype(BF16)
            c_h = (cg32 * jnp.broadcast_to(eacs[:, h:h + 1], (q, q))).astype(BF16)
            res = jnp.dot(jnp.concatenate([m_h, c_h], axis=1), rhs, preferred_element_type=F32)
            y_g = res if y_g is None else jnp.where(lane_g < (k + 1) * p, res, y_g)
        ys.append(y_g)
        xdd = (xg * expand(dtd, g)).astype(BF16)
        new = jnp.dot(bg32.T.astype(BF16), xdd, preferred_element_type=F32)
        st_ref[bb * SSD_GROUPS + g] = st * expand(cdec, g) + new
    y = jnp.concatenate(ys, axis=1)

    if finish:
        y = yf_ref[bb] + y + xs * dsk_ref[...]
        y = y * _silu(z_ref[bb].astype(F32))
        o_ref[bb] = _rms(y, ng_ref[...]).astype(o_ref.dtype)
    else:
        o_ref[bb] = y


def _ssd_direction(p0, s0, conv_w, conv_b, dt_bias, a_log, n_lat, reverse, extra=None):
    b, n, _ = p0.shape
    q = SSD_CHUNK
    nc = n // q
    nl = n_lat // q
    halo = 16
    hpc = q // halo
    nh = n // halo
    d = 1 if reverse else 0
    nb = _pick(b, (SSD_BATCH_PER_STEP, 1))
    cidx = functools.partial(_ssd_chunk_index, nl=nl, nc=nc, reverse=reverse)
    finish = extra is not None
    in_specs = [
        pl.BlockSpec((nb, q, SSD_CONV_DIM), lambda i, s: (i, cidx(s), 0)),
        pl.BlockSpec((nb, halo, SSD_CONV_DIM), lambda i, s: (i, jnp.maximum(cidx(s) * hpc - 1, 0), 0)),
        pl.BlockSpec((nb, halo, SSD_CONV_DIM), lambda i, s: (i, jnp.minimum(cidx(s) * hpc + hpc, nh - 1), 0)),
        pl.BlockSpec((nb, q, LANES), lambda i, s: (i, cidx(s), d)),
        pl.BlockSpec((3, SSD_CONV_DIM), lambda i, s: (0, 0)),
        pl.BlockSpec((1, SSD_CONV_DIM), lambda i, s: (0, 0)),
        pl.BlockSpec((1, LANES), lambda i, s: (0, 0)),
        pl.BlockSpec((1, LANES), lambda i, s: (0, 0)),
    ]
    args = [p0, p0, p0, s0, conv_w, conv_b, dt_bias, a_log]
    if finish:
        yf, d_skip, norm_g = extra
        in_specs += [
            pl.BlockSpec((nb, q, SSD_INNER), lambda i, s: (i, cidx(s), SSD_CONV_DIM // SSD_INNER)),
            pl.BlockSpec((nb, q, SSD_INNER), lambda i, s: (i, cidx(s), 0)),
            pl.BlockSpec((1, SSD_INNER), lambda i, s: (0, 0)),
            pl.BlockSpec((1, SSD_INNER), lambda i, s: (0, 0)),
        ]
        args += [p0, yf, d_skip, norm_g]
    kern = functools.partial(_ssd_kernel, reverse=reverse, finish=finish, nl=nl, nc=nc)
    return pl.pallas_call(
        kern,
        grid=(b // nb, nc),
        in_specs=in_specs,
        out_specs=pl.BlockSpec((nb, q, SSD_INNER), lambda i, s: (i, cidx(s), 0)),
        out_shape=jax.ShapeDtypeStruct((b, n, SSD_INNER), BF16 if finish else F32),
        scratch_shapes=[pltpu.VMEM((nb * SSD_GROUPS, SSD_STATE, (SSD_HEADS // SSD_GROUPS) * SSD_HEADDIM), F32)],
        compiler_params=_params(("arbitrary", "arbitrary")),
        name="ssd_bwd_finish" if finish else "ssd_fwd",
    )(*args)


def _rope_tables(rows, rot_dim, n_ctx):
    n_freq = rot_dim // 4
    row = jnp.repeat(jnp.arange(rows, dtype=F32), GRID_W)
    col = jnp.tile(jnp.arange(GRID_W, dtype=F32), rows)
    inv = ROPE_THETA ** (-jnp.arange(n_freq, dtype=F32) / n_freq)
    ang = jnp.concatenate([row[:, None] * inv, col[:, None] * inv], axis=-1)
    cos, sin = jnp.cos(ang), jnp.sin(ang)
    reps = LANES // rot_dim
    cos_t = jnp.tile(jnp.concatenate([cos, cos], axis=-1), (1, reps))
    sin_t = jnp.tile(jnp.concatenate([-sin, sin], axis=-1), (1, reps))
    cos_t = jnp.concatenate([cos_t, jnp.ones((n_ctx, LANES), F32)], axis=0)
    sin_t = jnp.concatenate([sin_t, jnp.zeros((n_ctx, LANES), F32)], axis=0)
    return cos_t, sin_t


def _rope_block(xb, cos, sin, rot_dim):
    half = rot_dim // 2
    lane = lax.broadcasted_iota(jnp.int32, xb.shape, 1)
    first = jnp.bitwise_and(lane, rot_dim - 1) < half
    partner = jnp.where(first, pltpu.roll(xb, LANES - half, axis=1), pltpu.roll(xb, half, axis=1))
    return xb * cos + partner * sin


def _segment_mean_matrix(seg):
    sh = seg.bit_length() - 1
    i = jnp.right_shift(lax.broadcasted_iota(jnp.int32, (LANES, LANES), 0), sh)
    j = jnp.right_shift(lax.broadcasted_iota(jnp.int32, (LANES, LANES), 1), sh)
    return jnp.where(i == j, 1.0 / seg, 0.0).astype(BF16)


def _segment_mean_sq(xb, smat):
    x2 = xb * xb
    hi = x2.astype(BF16)
    lo = (x2 - hi.astype(F32)).astype(BF16)
    return (jnp.dot(hi, smat, preferred_element_type=F32) + jnp.dot(lo, smat, preferred_element_type=F32))


def _store_key_norms(ksq_ref, c, sumsq):
    mx = jnp.max(sumsq, axis=0, keepdims=True)
    ksq_ref[0, 0, :, (2 * c) * LANES:(2 * c + 1) * LANES] = mx
    ksq_ref[0, 0, :, (2 * c + 1) * LANES:(2 * c + 2) * LANES] = pltpu.roll(mx, LANES // 2, axis=1)


def _qk_prep_kernel(q_ref, k_ref, v_ref, cos_ref, sin_ref, qg_ref, kg_ref, qo_ref, ko_ref, vo_ref, ksq_ref, *, scale):
    smat = _segment_mean_matrix(ATT_HEADDIM)
    cos = cos_ref[...]
    sin = sin_ref[...]
    hpb = LANES // ATT_HEADDIM

    def prep(src_ref, g_ref, dst_ref, n_heads, mul, transposed):
        for c in range(n_heads // hpb):
            xb = src_ref[0, :, c * LANES:(c + 1) * LANES].astype(F32)
            xn = xb * lax.rsqrt(_segment_mean_sq(xb, smat) + EPS) * g_ref[...]
            r = _rope_block(xn, cos, sin, ATT_HEADDIM)
            if mul != 1.0:
                r = r * mul
            if transposed:
                r = r.T
            else:
                _store_key_norms(ksq_ref, c, _segment_mean_sq(r, smat) * float(ATT_HEADDIM))
            r = r.astype(dst_ref.dtype)
            for k in range(hpb):
                if transposed:
                    dst_ref[0, c * hpb + k] = r[k * ATT_HEADDIM:(k + 1) * ATT_HEADDIM, :]
                else:
                    dst_ref[0, c * hpb + k] = r[:, k * ATT_HEADDIM:(k + 1) * ATT_HEADDIM]

    prep(q_ref, qg_ref, qo_ref, ATT_HEADS, scale, True)
    prep(k_ref, kg_ref, ko_ref, ATT_KV_HEADS, 1.0, False)
    for c in range(ATT_KV_HEADS // hpb):
        vt = v_ref[0, :, c * LANES:(c + 1) * LANES].astype(F32).T.astype(vo_ref.dtype)
        for k in range(hpb):
            vo_ref[0, c * hpb + k] = vt[k * ATT_HEADDIM:(k + 1) * ATT_HEADDIM, :]


def _qk_prep(p0, cos_t, sin_t, q_g, k_g, q_col, k_col, v_col):
    b, n, _ = p0.shape
    tm = _pick(n, (768, 384, 256, 128))
    kern = functools.partial(_qk_prep_kernel, scale=ATT_HEADDIM ** -0.5 * LOG2E)
    return pl.pallas_call(
        kern,
        grid=(b, n // tm),
        in_specs=[
            pl.BlockSpec((1, tm, ATT_Q), lambda i, t: (i, t, q_col // ATT_Q)),
            pl.BlockSpec((1, tm, ATT_KV), lambda i, t: (i, t, k_col // ATT_KV)),
            pl.BlockSpec((1, tm, ATT_KV), lambda i, t: (i, t, v_col // ATT_KV)),
            pl.BlockSpec((tm, LANES), lambda i, t: (t, 0)),
            pl.BlockSpec((tm, LANES), lambda i, t: (t, 0)),
            pl.BlockSpec((1, LANES), lambda i, t: (0, 0)),
            pl.BlockSpec((1, LANES), lambda i, t: (0, 0)),
        ],
        out_specs=[
            pl.BlockSpec((1, ATT_HEADS, ATT_HEADDIM, tm), lambda i, t: (i, 0, 0, t)),
            pl.BlockSpec((1, ATT_KV_HEADS, tm, ATT_HEADDIM), lambda i, t: (i, 0, t, 0)),
            pl.BlockSpec((1, ATT_KV_HEADS, ATT_HEADDIM, tm), lambda i, t: (i, 0, 0, t)),
            pl.BlockSpec((1, 1, 1, ATT_KV_HEADS * LANES), lambda i, t: (i, t, 0, 0)),
        ],
        out_shape=[
            jax.ShapeDtypeStruct((b, ATT_HEADS, ATT_HEADDIM, n), BF16),
            jax.ShapeDtypeStruct((b, ATT_KV_HEADS, n, ATT_HEADDIM), BF16),
            jax.ShapeDtypeStruct((b, ATT_KV_HEADS, ATT_HEADDIM, n), BF16),
            jax.ShapeDtypeStruct((b, n // tm, 1, ATT_KV_HEADS * LANES), F32),
        ],
        compiler_params=_params(("arbitrary", "arbitrary")),
        name="gqa_qk_prep",
    )(p0, p0, p0, cos_t, sin_t, q_g, k_g)


def _attn_kernel(qt_ref, k_ref, vt_ref, ksq_ref, o_ref, s_scr, m_scr, e_scr, gap_scr, ot_scr, *,
                 group, hpb, n_lat, n_all, lat_tiles, ctx_self):
    t = pl.program_id(2)
    dv = vt_ref.shape[2]
    tq = qt_ref.shape[3]
    n_blocks = qt_ref.shape[1] // hpb

    ksq_max = jnp.max(ksq_ref[0], axis=0)

    def queries(i):
        return jnp.concatenate([qt_ref[0, i * hpb + j] for j in range(hpb)], axis=1)

    def finish(i, ot):
        ot = ot[:dv] * (1.0 / ot[dv:dv + 1])
        for j in range(hpb):
            ot_scr[(i * hpb + j) * dv:(i * hpb + j + 1) * dv, :] = ot[:, j * tq:(j + 1) * tq]

    def run(k_lo, k_len, exact):
        row = lax.broadcasted_iota(jnp.int32, (BF16_ROWS, k_len), 0)
        ones_row = jnp.where(row == 0, 1.0, 0.0).astype(BF16)

        def scores(i):
            kv = i * hpb // group
            qt = queries(i)
            s = jnp.dot(k_ref[0, kv, k_lo:k_lo + k_len, :], qt, preferred_element_type=F32)
            m = jnp.max(s, axis=0, keepdims=True)
            if exact:
                s_scr[i % 2, :k_len, :] = s
                m_scr[i % 2] = m
            else:
                qf = qt.astype(F32)
                ksq = ksq_max[:, kv * LANES:kv * LANES + 1]
                bound = jnp.sqrt(jnp.sum(qf * qf, axis=0, keepdims=True) * ksq) * ATTN_BOUND_SLACK
                e_scr[i % 2, :k_len, :] = jnp.exp2(s - bound).astype(BF16)
                gap_scr[i] = bound - m

        def combine(i):
            if exact:
                e = jnp.exp2(s_scr[i % 2, :k_len, :] - m_scr[i % 2]).astype(BF16)
            else:
                e = e_scr[i % 2, :k_len, :]
            vt = vt_ref[0, i * hpb // group, :, k_lo:k_lo + k_len]
            finish(i, jnp.dot(jnp.concatenate([vt, ones_row], axis=0), e, preferred_element_type=F32))

        scores(0)
        for i in range(n_blocks):
            if i + 1 < n_blocks:
                scores(i + 1)
            combine(i)

    def attend(k_lo, k_len):
        run(k_lo, k_len, exact=False)

        @pl.when(jnp.max(gap_scr[...]) > ATTN_MAX_SHIFT)
        def _():
            run(k_lo, k_len, exact=True)

        o_ref[0] = ot_scr[...].T.astype(o_ref.dtype)

    if ctx_self:
        @pl.when(t < lat_tiles)
        def _():
            attend(0, n_all)

        @pl.when(t >= lat_tiles)
        def _():
            attend(n_lat, n_all - n_lat)
    else:
        attend(0, n_all)


def _attention(qt, k, vt, ksq, n_lat, ctx_self):
    b, h, dqk, nq = qt.shape
    hkv, dv, n_all = vt.shape[1], vt.shape[2], vt.shape[3]
    group = h // hkv
    if ctx_self:
        tq = _pick(math.gcd(n_lat, n_all - n_lat), (256, 128))
    else:
        tq = _pick(n_lat, (ATTN_BLOCK_LANES, 256, 128))
    hpb = max(1, min(group, ATTN_BLOCK_LANES // tq))
    heads = min(h, ATTN_BLOCKS_PER_STEP * hpb)
    kvb = max(heads // group, 1)
    assert nq == (n_all if ctx_self else n_lat)
    kern = functools.partial(_attn_kernel, group=group, hpb=hpb, n_lat=n_lat, n_all=n_all,
                             lat_tiles=n_lat // tq, ctx_self=ctx_self)
    return pl.pallas_call(
        kern,
        grid=(b, h // heads, nq // tq),
        in_specs=[
            pl.BlockSpec((1, heads, dqk, tq), lambda i, hb, t: (i, hb, 0, t)),
            pl.BlockSpec((1, kvb, n_all, dqk), lambda i, hb, t: (i, hb, 0, 0)),
            pl.BlockSpec((1, kvb, dv, n_all), lambda i, hb, t: (i, hb, 0, 0)),
            pl.BlockSpec((1, ksq.shape[1], 1, kvb * LANES), lambda i, hb, t: (i, 0, 0, hb)),
        ],
        out_specs=pl.BlockSpec((1, tq, heads * dv), lambda i, hb, t: (i, t, hb)),
        out_shape=jax.ShapeDtypeStruct((b, nq, h * dv), BF16),
        scratch_shapes=[pltpu.VMEM((2, n_all, tq * hpb), F32), pltpu.VMEM((2, 1, tq * hpb), F32),
                        pltpu.VMEM((2, n_all, tq * hpb), BF16), pltpu.VMEM((heads // hpb, 1, tq * hpb), F32),
                        pltpu.VMEM((heads * dv, tq), F32)],
        compiler_params=_params(("arbitrary", "arbitrary", "arbitrary")),
        name="attention",
    )(qt, k, vt, ksq)


def _hidden_chunks(hidden):
    chunks, lo = [], 0
    while lo < hidden:
        w = min(1024, hidden - lo)
        chunks.append((lo, w))
        lo += w
    return chunks


def _mix_ffn_kernel(*refs, n_in, tm, n_lat, final):
    x_ref, ml_ref, mc_ref, g_ref = refs[:4]
    a_refs = refs[4:4 + n_in]
    w_refs = refs[4 + n_in:4 + 2 * n_in]
    wg_ref, wu_ref, wd_ref, fg_ref, o_ref = refs[4 + 2 * n_in:]
    t = pl.program_id(1)
    is_lat = _is_lat(t, tm, n_lat)
    mod = functools.partial(_mod_row, ml_ref, mc_ref, is_lat=is_lat)
    mix = jnp.dot(a_refs[0][0], w_refs[0][...], preferred_element_type=F32)
    for a_ref, w_ref in zip(a_refs[1:], w_refs[1:]):
        mix = mix + jnp.dot(a_ref[0], w_ref[...], preferred_element_type=F32)
    x = x_ref[0] + mod(2) * mix
    h = (_rms(x, g_ref[...]) * (1.0 + mod(4)) + mod(3)).astype(BF16)
    acc = None
    for lo, w in _hidden_chunks(wd_ref.shape[0]):
        gate = jnp.dot(h, wg_ref[:, lo:lo + w], preferred_element_type=F32)
        up = jnp.dot(h, wu_ref[:, lo:lo + w], preferred_element_type=F32)
        a = (_silu(gate) * up).astype(BF16)
        part = jnp.dot(a, wd_ref[lo:lo + w, :], preferred_element_type=F32)
        acc = part if acc is None else acc + part
    y = x + mod(5) * acc
    if final:
        y = _rms(y, fg_ref[...])
    o_ref[0] = y


def _mix_ffn(xa, mods_l, mods_c, g, acts, weights, w_gate, w_up, w_down, final_g, n_rows, n_lat, final):
    b, _, d = xa.shape
    hid = w_down.shape[0]
    tm = _pick(n_rows, (576, 512, 384, 256, 128))
    n_in = len(acts)
    kern = functools.partial(_mix_ffn_kernel, n_in=n_in, tm=tm, n_lat=n_lat, final=final)
    resident = dict(pipeline_mode=pl.Buffered(1))
    in_specs = [
        pl.BlockSpec((1, tm, d), lambda i, t: (i, t, 0)),
        pl.BlockSpec((1, ADA_CHUNKS, d), lambda i, t: (i, 0, 0)),
        pl.BlockSpec((ADA_CHUNKS, d), lambda i, t: (0, 0)),
        pl.BlockSpec((1, d), lambda i, t: (0, 0)),
    ]
    in_specs += [pl.BlockSpec((1, tm, a.shape[2]), lambda i, t: (i, t, 0)) for a in acts]
    in_specs += [pl.BlockSpec(w.shape, lambda i, t: (0, 0), **resident) for w in weights]
    in_specs += [
        pl.BlockSpec((d, hid), lambda i, t: (0, 0), **resident),
        pl.BlockSpec((d, hid), lambda i, t: (0, 0), **resident),
        pl.BlockSpec((hid, d), lambda i, t: (0, 0), **resident),
        pl.BlockSpec((1, d), lambda i, t: (0, 0)),
    ]
    return pl.pallas_call(
        kern,
        grid=(b, n_rows // tm),
        in_specs=in_specs,
        out_specs=pl.BlockSpec((1, tm, d), lambda i, t: (i, t, 0)),
        out_shape=jax.ShapeDtypeStruct((b, n_rows, d), F32),
        compiler_params=_params(("arbitrary", "arbitrary")),
        name="mix_ffn_final" if final else "mix_ffn",
    )(xa, mods_l, mods_c, g, *acts, *weights, w_gate, w_up, w_down, final_g)


def _mla_q_kernel(p_ref, g_ref, w_ref, cos_ref, sin_ref, o_ref, *, q_rank, scale):
    cq = p_ref[0, :, :q_rank].astype(F32)
    qn = _rms(cq, g_ref[...]).astype(BF16)
    qq = jnp.dot(qn, w_ref[...], preferred_element_type=F32)
    qq = qq * scale
    n_nope = MLA_HEADS * MLA_NOPE
    ppb = LANES // MLA_ROPE
    npb = LANES // MLA_NOPE
    pes_t = [_rope_block(qq[:, n_nope + c * LANES:n_nope + (c + 1) * LANES], cos_ref[...], sin_ref[...], MLA_ROPE).T
             for c in range(MLA_HEADS // ppb)]
    for c in range(MLA_HEADS // npb):
        nope_t = qq[:, c * LANES:(c + 1) * LANES].T
        for k in range(npb):
            h = c * npb + k
            pe_t = pes_t[h // ppb][(h % ppb) * MLA_ROPE:(h % ppb + 1) * MLA_ROPE, :]
            qh = jnp.concatenate([nope_t[k * MLA_NOPE:(k + 1) * MLA_NOPE, :], pe_t], axis=0)
            o_ref[0, h] = qh.astype(o_ref.dtype)


def _mla_q(p1, g, w_uq, cos_t, sin_t, n_lat, q_rank):
    b = p1.shape[0]
    cols = p1.shape[2]
    tm = _pick(n_lat, (512, 256, 128))
    kern = functools.partial(_mla_q_kernel, q_rank=q_rank, scale=MLA_QK ** -0.5 * LOG2E)
    return pl.pallas_call(
        kern,
        grid=(b, n_lat // tm),
        in_specs=[
            pl.BlockSpec((1, tm, cols), lambda i, t: (i, t, 0)),
            pl.BlockSpec((1, q_rank), lambda i, t: (0, 0)),
            pl.BlockSpec(w_uq.shape, lambda i, t: (0, 0)),
            pl.BlockSpec((tm, LANES), lambda i, t: (t, 0)),
            pl.BlockSpec((tm, LANES), lambda i, t: (t, 0)),
        ],
        out_specs=pl.BlockSpec((1, MLA_HEADS, MLA_QK, tm), lambda i, t: (i, 0, 0, t)),
        out_shape=jax.ShapeDtypeStruct((b, MLA_HEADS, MLA_QK, n_lat), BF16),
        compiler_params=_params(("arbitrary", "arbitrary")),
        name="mla_q",
    )(p1, g, w_uq, cos_t, sin_t)


def _mla_kv_kernel(p_ref, pe_ref, g_ref, w_ref, cos_ref, sin_ref, ko_ref, vo_ref, ksq_ref, *, q_rank):
    ckv = p_ref[0, :, q_rank:].astype(F32)
    kvn = _rms(ckv, g_ref[...]).astype(BF16)
    kv = jnp.dot(kvn, w_ref[...], preferred_element_type=F32)
    pe_block = _rope_block(pe_ref[0], cos_ref[...], sin_ref[...], MLA_ROPE)
    pe = pe_block[:, :MLA_ROPE]
    n_nope = MLA_HEADS * MLA_NOPE
    for h in range(MLA_HEADS):
        kh = jnp.concatenate([kv[:, h * MLA_NOPE:(h + 1) * MLA_NOPE], pe], axis=1)
        ko_ref[0, h] = kh.astype(ko_ref.dtype)
    pe_sq = jnp.sum(pe_block * pe_block, axis=1, keepdims=True)
    smat = _segment_mean_matrix(MLA_NOPE)
    for c in range(MLA_HEADS * MLA_NOPE // LANES):
        nope_sq = _segment_mean_sq(kv[:, c * LANES:(c + 1) * LANES], smat) * float(MLA_NOPE)
        _store_key_norms(ksq_ref, c, nope_sq + pe_sq)
    hpb = LANES // MLA_V
    for c in range(MLA_HEADS // hpb):
        v_t = kv[:, n_nope + c * LANES:n_nope + (c + 1) * LANES].T.astype(vo_ref.dtype)
        for k in range(hpb):
            vo_ref[0, c * hpb + k] = v_t[k * MLA_V:(k + 1) * MLA_V, :]


def _mla_kv(p1, s1, g, w_ukv, cos_t, sin_t, q_rank):
    b, n, cols = p1.shape
    kv_rank = cols - q_rank
    tm = _pick(n, (768, 384, 256, 128))
    kern = functools.partial(_mla_kv_kernel, q_rank=q_rank)
    return pl.pallas_call(
        kern,
        grid=(b, n // tm),
        in_specs=[
            pl.BlockSpec((1, tm, cols), lambda i, t: (i, t, 0)),
            pl.BlockSpec((1, tm, LANES), lambda i, t: (i, t, 0)),
            pl.BlockSpec((1, kv_rank), lambda i, t: (0, 0)),
            pl.BlockSpec(w_ukv.shape, lambda i, t: (0, 0)),
            pl.BlockSpec((tm, LANES), lambda i, t: (t, 0)),
            pl.BlockSpec((tm, LANES), lambda i, t: (t, 0)),
        ],
        out_specs=[
            pl.BlockSpec((1, MLA_HEADS, tm, MLA_QK), lambda i, t: (i, 0, t, 0)),
            pl.BlockSpec((1, MLA_HEADS, MLA_V, tm), lambda i, t: (i, 0, 0, t)),
            pl.BlockSpec((1, 1, 1, MLA_HEADS * LANES), lambda i, t: (i, t, 0, 0)),
        ],
        out_shape=[
            jax.ShapeDtypeStruct((b, MLA_HEADS, n, MLA_QK), BF16),
            jax.ShapeDtypeStruct((b, MLA_HEADS, MLA_V, n), BF16),
            jax.ShapeDtypeStruct((b, n // tm, 1, MLA_HEADS * LANES), F32),
        ],
        compiler_params=_params(("arbitrary", "arbitrary")),
        name="mla_kv",
    )(p1, s1, g, w_ukv, cos_t, sin_t)


def _pad_lanes(v, width=LANES):
    v = v.reshape(1, -1).astype(F32)
    return jnp.pad(v, ((0, 0), (0, width - v.shape[1])))


def kernel(x, c, ctx, c_ctx, ada_w, ada_b, norm1_g, norm2_g, ffn_w_up, ffn_w_down, ab_w_in, ab_w_out, ssd_conv_w, ssd_conv_b, ssd_a_log, ssd_dt_bias, ssd_d, ssd_norm_g, att_q_g, att_k_g, mla_w_in, mla_q_norm_g, mla_w_uq, mla_kv_norm_g, mla_w_ukv, mla_w_o, final_norm_g):
    b, t_lat, d = x.shape
    n_ctx = ctx.shape[1]
    rows = t_lat // GRID_W
    hid = ffn_w_down.shape[1]
    row2 = lambda v: v.reshape(1, -1).astype(F32)

    bp = -(-(b + 1) // 8) * 8
    cvec = jnp.concatenate([c, c_ctx[None, :], jnp.zeros((bp - b - 1, d), F32)], axis=0)
    mods = _ada_mods(cvec, ada_w, ada_b)
    mods_l = [mods[i, :b].reshape(b, ADA_CHUNKS, d) for i in range(2)]
    mods_c = [mods[i, b].reshape(ADA_CHUNKS, d) for i in range(2)]

    w_in = ab_w_in[0]
    o_xbc, o_dt = SSD_INNER, SSD_INNER + SSD_CONV_DIM
    o_att = o_dt + 2 * SSD_HEADS
    w_main = jnp.concatenate([w_in[:, o_xbc:o_dt], w_in[:, :SSD_INNER], w_in[:, o_att:]], axis=1).astype(BF16)
    w_dt = jnp.zeros((d, 2 * LANES), F32)
    w_dt = w_dt.at[:, :SSD_HEADS].set(w_in[:, o_dt:o_dt + SSD_HEADS])
    w_dt = w_dt.at[:, LANES:LANES + SSD_HEADS].set(w_in[:, o_dt + SSD_HEADS:o_att]).astype(BF16)
    p0, s0, xa = _in_proj([x, ctx], mods_l[0], mods_c[0], row2(norm1_g[0]), w_main, w_dt, t_lat, n_ctx)

    conv_w = ssd_conv_w[0].astype(F32)
    conv_b = row2(ssd_conv_b[0])
    yf = _ssd_direction(p0, s0, conv_w, conv_b, _pad_lanes(ssd_dt_bias[0, 0]), _pad_lanes(ssd_a_log[0, 0]),
                        t_lat, reverse=False)
    d_skip = jnp.repeat(ssd_d[0].astype(F32), SSD_HEADDIM).reshape(1, SSD_INNER)
    y_ssd = _ssd_direction(p0, s0, conv_w, conv_b, _pad_lanes(ssd_dt_bias[0, 1]), _pad_lanes(ssd_a_log[0, 1]),
                           t_lat, reverse=True, extra=(yf, d_skip, row2(ssd_norm_g[0])))

    q_col = SSD_CONV_DIM + SSD_INNER
    k_col = q_col + ATT_Q
    cos_a, sin_a = _rope_tables(rows, ATT_HEADDIM, n_ctx)
    hpb = LANES // ATT_HEADDIM
    qh, kh, vh, ksq0 = _qk_prep(p0, cos_a, sin_a, jnp.tile(row2(att_q_g[0]), (1, hpb)),
                                jnp.tile(row2(att_k_g[0]), (1, hpb)), q_col, k_col, k_col + ATT_KV)
    o_att_l0 = _attention(qh, kh, vh, ksq0, t_lat, ctx_self=True)

    w_out = ab_w_out[0].astype(BF16)
    w_up = ffn_w_up[0].astype(BF16)
    xa = _mix_ffn(xa, mods_l[0], mods_c[0], row2(norm2_g[0]), [y_ssd, o_att_l0],
                  [w_out[:SSD_INNER], w_out[SSD_INNER:]], w_up[:, :hid], w_up[:, hid:],
                  ffn_w_down[0].astype(BF16), row2(final_norm_g), t_lat + n_ctx, t_lat, final=False)

    q_rank = mla_q_norm_g.shape[1]
    kv_rank = mla_kv_norm_g.shape[1]
    w_in1 = mla_w_in[0]
    w_pe = jnp.pad(w_in1[:, q_rank + kv_rank:], ((0, 0), (0, LANES - MLA_ROPE))).astype(BF16)
    p1, s1 = _in_proj([xa], mods_l[1], mods_c[1], row2(norm1_g[1]), w_in1[:, :q_rank + kv_rank].astype(BF16), w_pe,
                      t_lat, n_ctx)
    cos_m, sin_m = _rope_tables(rows, MLA_ROPE, n_ctx)
    w_uq = mla_w_uq[0].reshape(q_rank, MLA_HEADS, MLA_QK)
    w_uq = jnp.concatenate([w_uq[:, :, :MLA_NOPE].reshape(q_rank, -1), w_uq[:, :, MLA_NOPE:].reshape(q_rank, -1)],
                           axis=1).astype(BF16)
    w_ukv = mla_w_ukv[0].reshape(kv_rank, MLA_HEADS, MLA_NOPE + MLA_V)
    w_ukv = jnp.concatenate([w_ukv[:, :, :MLA_NOPE].reshape(kv_rank, -1), w_ukv[:, :, MLA_NOPE:].reshape(kv_rank, -1)],
                            axis=1).astype(BF16)
    qm = _mla_q(p1, row2(mla_q_norm_g[0]), w_uq, cos_m, sin_m, t_lat, q_rank)
    km, vm, ksq1 = _mla_kv(p1, s1, row2(mla_kv_norm_g[0]), w_ukv, cos_m, sin_m, q_rank)
    o_mla = _attention(qm, km, vm, ksq1, t_lat, ctx_self=False)

    w_up = ffn_w_up[1].astype(BF16)
    return _mix_ffn(xa, mods_l[1], mods_c[1], row2(norm2_g[1]), [o_mla], [mla_w_o[0].astype(BF16)],
                    w_up[:, :hid], w_up[:, hid:], ffn_w_down[1].astype(BF16), row2(final_norm_g),
                    t_lat, t_lat, final=True)
```

```python
import functools
import math

import jax
import jax.numpy as jnp
from jax import lax
from jax.experimental import pallas as pl
from jax.experimental.pallas import tpu as pltpu

F32 = jnp.float32
BF16 = jnp.bfloat16

EPS = 1e-6
ROPE_THETA = 10000.0
GRID_W = 64
ADA_CHUNKS = 6

SSD_HEADS = 16
SSD_HEADDIM = 64
SSD_GROUPS = 4
SSD_STATE = 128
SSD_CHUNK = 128
SSD_BATCH_PER_STEP = 4
SSD_INNER = SSD_HEADS * SSD_HEADDIM
SSD_GN = SSD_GROUPS * SSD_STATE
SSD_CONV_DIM = SSD_INNER + 2 * SSD_GN

ATT_HEADS = 16
ATT_KV_HEADS = 4
ATT_HEADDIM = 64
ATT_Q = ATT_HEADS * ATT_HEADDIM
ATT_KV = ATT_KV_HEADS * ATT_HEADDIM

MLA_HEADS = 16
MLA_NOPE = 64
MLA_ROPE = 32
MLA_V = 64
MLA_QK = MLA_NOPE + MLA_ROPE

LANES = 128
SUBLANES = 8
BF16_ROWS = 16
ATTN_BLOCKS_PER_STEP = 8
ATTN_BOUND_SLACK = 1.0 + 2.0 ** -6
ATTN_MAX_SHIFT = 64.0
ATTN_BLOCK_LANES = 512
LOG2E = math.log2(math.e)
VMEM_LIMIT = 56 * 1024 * 1024


def _pick(n, candidates):
    for c in candidates:
        if n % c == 0:
            return c
    raise ValueError(f"no tile for {n} in {candidates}")


def _params(sem):
    return pltpu.CompilerParams(dimension_semantics=sem, vmem_limit_bytes=VMEM_LIMIT)


def _silu(x):
    hx = 0.5 * x
    return hx + hx * jnp.tanh(hx)


def _rms(x, g):
    ms = jnp.mean(x * x, axis=-1, keepdims=True)
    return x * lax.rsqrt(ms + EPS) * g


def _mod_row(ml_ref, mc_ref, idx, is_lat):
    return jnp.where(is_lat, ml_ref[0, idx:idx + 1, :], mc_ref[idx:idx + 1, :])


def _is_lat(t, tm, n_lat):
    rows = t * tm + lax.broadcasted_iota(jnp.int32, (tm, 1), 0)
    return rows < n_lat


def _ada_kernel(c_ref, w_ref, b_ref, o_ref):
    s = _silu(c_ref[...]).astype(BF16)
    o_ref[0] = jnp.dot(s, w_ref[0].astype(BF16), preferred_element_type=F32) + b_ref[0]


def _ada_mods(cvec, ada_w, ada_b):
    depth, d, n = ada_w.shape
    bp = cvec.shape[0]
    tn = _pick(n, (1536, 1024, 512, 256, 128))
    return pl.pallas_call(
        _ada_kernel,
        grid=(depth, n // tn),
        in_specs=[
            pl.BlockSpec((bp, d), lambda i, j: (0, 0)),
            pl.BlockSpec((1, d, tn), lambda i, j: (i, 0, j)),
            pl.BlockSpec((1, 1, tn), lambda i, j: (i, 0, j)),
        ],
        out_specs=pl.BlockSpec((1, bp, tn), lambda i, j: (i, 0, j)),
        out_shape=jax.ShapeDtypeStruct((depth, bp, n), F32),
        compiler_params=_params(("arbitrary", "arbitrary")),
        name="ada_mods",
    )(cvec, ada_w, ada_b.reshape(depth, 1, n))


def _in_proj_kernel(*refs, n_src, n_lat, emit_rows):
    srcs = refs[:n_src]
    ml_ref, mc_ref, g_ref, w_ref, ws_ref = refs[n_src:n_src + 5]
    o_ref, os_ref = refs[n_src + 5:n_src + 7]
    t = pl.program_id(1)
    tm = o_ref.shape[1]
    rc = _pick(tm, (128,))
    for r in range(0, tm, rc):
        lat = (t * tm + r + lax.broadcasted_iota(jnp.int32, (rc, 1), 0)) < n_lat
        shift = _mod_row(ml_ref, mc_ref, 0, lat)
        scale = _mod_row(ml_ref, mc_ref, 1, lat)
        xin = srcs[0][0, r:r + rc, :]
        if n_src == 2:
            xin = jnp.where(lat, xin, srcs[1][0, r:r + rc, :])
        if emit_rows:
            refs[n_src + 7][0, r:r + rc, :] = xin
        hb = (_rms(xin, g_ref[...]) * (1.0 + scale) + shift).astype(BF16)
        os_ref[0, r:r + rc, :] = jnp.dot(hb, ws_ref[...], preferred_element_type=F32)
        o_ref[0, r:r + rc, :] = jnp.dot(hb, w_ref[...], preferred_element_type=F32).astype(o_ref.dtype)


def _in_proj(srcs, mods_l, mods_c, g, w_main, w_side, n_lat, n_ctx):
    b, _, d = srcs[0].shape
    n = n_lat + n_ctx
    nm = w_main.shape[1]
    ns = w_side.shape[1]
    split = len(srcs) == 2
    tm = _pick(math.gcd(n_lat, n_ctx), (256, 128)) if split else _pick(n, (768, 384, 256, 128))
    lat_tiles = n_lat // tm
    kern = functools.partial(_in_proj_kernel, n_src=len(srcs), n_lat=n_lat, emit_rows=split)
    if split:
        src_specs = [pl.BlockSpec((1, tm, d), lambda i, t: (i, jnp.minimum(t, lat_tiles - 1), 0)),
                     pl.BlockSpec((1, tm, d), lambda i, t: (i, jnp.maximum(t - lat_tiles, 0), 0))]
    else:
        src_specs = [pl.BlockSpec((1, tm, d), lambda i, t: (i, t, 0))]
    resident = dict(pipeline_mode=pl.Buffered(1))
    out_specs = [pl.BlockSpec((1, tm, nm), lambda i, t: (i, t, 0)),
                 pl.BlockSpec((1, tm, ns), lambda i, t: (i, t, 0))]
    out_shape = [jax.ShapeDtypeStruct((b, n, nm), BF16), jax.ShapeDtypeStruct((b, n, ns), F32)]
    if split:
        out_specs.append(pl.BlockSpec((1, tm, d), lambda i, t: (i, t, 0)))
        out_shape.append(jax.ShapeDtypeStruct((b, n, d), F32))
    return pl.pallas_call(
        kern,
        grid=(b, n // tm),
        in_specs=src_specs + [
            pl.BlockSpec((1, ADA_CHUNKS, d), lambda i, t: (i, 0, 0)),
            pl.BlockSpec((ADA_CHUNKS, d), lambda i, t: (0, 0)),
            pl.BlockSpec((1, d), lambda i, t: (0, 0)),
            pl.BlockSpec((d, nm), lambda i, t: (0, 0), **resident),
            pl.BlockSpec((d, ns), lambda i, t: (0, 0), **resident),
        ],
        out_specs=out_specs,
        out_shape=out_shape,
        compiler_params=_params(("arbitrary", "arbitrary")),
        name="in_proj",
    )(*srcs, mods_l, mods_c, g, w_main, w_side)


def _cumsum_rows(v, reverse):
    n = v.shape[0]
    row = lax.broadcasted_iota(jnp.int32, v.shape, 0)
    k = 1
    while k < n:
        if reverse:
            v = v + jnp.where(row < n - k, pltpu.roll(v, n - k, axis=0), 0.0)
        else:
            v = v + jnp.where(row >= k, pltpu.roll(v, k, axis=0), 0.0)
        k *= 2
    return v


def _ssd_chunk_index(s, nl, nc, reverse):
    if reverse:
        return nc - 1 - s
    return lax.rem(s + nl, nc)


def _ssd_kernel(*refs, reverse, finish, nl, nc):
    st_ref = refs[-1]

    @pl.when(pl.program_id(1) == 0)
    def _():
        st_ref[...] = jnp.zeros_like(st_ref)

    for bb in range(refs[0].shape[0]):
        _ssd_chunk(bb, *refs, reverse=reverse, finish=finish, nl=nl, nc=nc)


def _ssd_conv_act(bb, xm_ref, xp_ref, xn_ref, cw_ref, cb_ref, chunk, nl, nc):
    q = SSD_CHUNK
    xmb = xm_ref[bb]
    xm = xmb.astype(F32)
    halo = xp_ref.shape[1]
    seq_first = jnp.logical_or(chunk == 0, chunk == nl)
    seq_last = jnp.logical_or(chunk == nl - 1, chunk == nc - 1)
    prev_row = xp_ref[bb, halo - 1:halo, :].astype(F32) * jnp.where(seq_first, 0.0, 1.0)
    next_row = xn_ref[bb, 0:1, :].astype(F32) * jnp.where(seq_last, 0.0, 1.0)
    ii = lax.broadcasted_iota(jnp.int32, (q, q), 0)
    jj = lax.broadcasted_iota(jnp.int32, (q, q), 1)
    shift_dn = jnp.where(ii == jj + 1, 1.0, 0.0).astype(BF16)
    shift_up = jnp.where(ii + 1 == jj, 1.0, 0.0).astype(BF16)
    x_prev = jnp.dot(shift_dn, xmb, preferred_element_type=F32)
    x_next = jnp.dot(shift_up, xmb, preferred_element_type=F32)
    row8 = lax.broadcasted_iota(jnp.int32, (SUBLANES, 1), 0)
    x_prev = jnp.concatenate(
        [x_prev[:SUBLANES] + jnp.where(row8 == 0, prev_row, 0.0), x_prev[SUBLANES:]], axis=0)
    x_next = jnp.concatenate(
        [x_next[:q - SUBLANES], x_next[q - SUBLANES:] + jnp.where(row8 == SUBLANES - 1, next_row, 0.0)], axis=0)
    hw = 0.5 * cw_ref[...]
    hc = x_prev * hw[0:1, :] + xm * hw[1:2, :] + x_next * hw[2:3, :] + 0.5 * cb_ref[...]
    return hc + hc * jnp.tanh(hc)


def _ssd_chunk(bb, *refs, reverse, finish, nl, nc):
    q = SSD_CHUNK
    p = SSD_HEADDIM
    hpg = SSD_HEADS // SSD_GROUPS
    gw = hpg * p
    chunk = _ssd_chunk_index(pl.program_id(1), nl, nc, reverse)
    if finish:
        act_ref, dt_ref, dtb_ref, alog_ref, z_ref, yf_ref, dsk_ref, ng_ref, o_ref, st_ref = refs
        actb = act_ref[bb]
        act = actb.astype(F32)
    else:
        xm_ref, xp_ref, xn_ref, dt_ref, cw_ref, cb_ref, dtb_ref, alog_ref, o_ref, act_ref, st_ref = refs
        act = _ssd_conv_act(bb, xm_ref, xp_ref, xn_ref, cw_ref, cb_ref, chunk, nl, nc)
        actb = act.astype(BF16)
        act_ref[bb] = actb
    ii = lax.broadcasted_iota(jnp.int32, (q, q), 0)
    jj = lax.broadcasted_iota(jnp.int32, (q, q), 1)
    xs = act[:, :SSD_INNER]
    xsb = actb[:, :SSD_INNER]
    bm = act[:, SSD_INNER:SSD_INNER + SSD_GN]
    cm = act[:, SSD_INNER + SSD_GN:]

    dtr = dt_ref[bb] + dtb_ref[...]
    dt = jnp.maximum(dtr, 0.0) + jnp.log(1.0 + jnp.exp(-jnp.abs(dtr)))
    a = -jnp.exp(alog_ref[...])
    acs = _cumsum_rows(dt * a, reverse) * LOG2E
    col_t = (acs - jnp.log2(dt)).T
    edge = acs[0:1, :] if reverse else acs[q - 1:q, :]
    dtd = dt * jnp.exp2(edge - acs)
    state_rows = finish
    dtd_t = dtd.T if state_rows else None
    cdec = jnp.exp2(edge)
    mask = (ii <= jj) if reverse else (ii >= jj)
    lane_h = lax.broadcasted_iota(jnp.int32, (q, LANES), 1)
    lane_res = lax.broadcasted_iota(jnp.int32, (q + SSD_STATE if state_rows else q, gw), 1)

    def expand(v, g):
        rows = v.shape[0]
        parts = []
        for k in range(0, hpg, LANES // p):
            h0 = g * hpg + k
            lo = jnp.broadcast_to(v[:, h0:h0 + 1], (rows, LANES))
            hi = jnp.broadcast_to(v[:, h0 + 1:h0 + 2], (rows, LANES))
            parts.append(jnp.where(lane_h[:rows] < p, lo, hi))
        return jnp.concatenate(parts, axis=1)

    ys = []
    for g in range(SSD_GROUPS):
        cg32 = cm[:, g * SSD_STATE:(g + 1) * SSD_STATE]
        bg32 = bm[:, g * SSD_STATE:(g + 1) * SSD_STATE]
        cb = lax.dot_general(cg32.astype(BF16), bg32.astype(BF16), (((1,), (1,)), ((), ())),
                             preferred_element_type=F32)
        st = st_ref[bb * SSD_GROUPS + g]
        bgt = bg32.T
        rhs = jnp.concatenate([xsb[:, g * gw:(g + 1) * gw], st.astype(BF16)], axis=0)
        res_g = None
        for k in reversed(range(hpg)):
            h = g * hpg + k
            a_col = jnp.broadcast_to(acs[:, h:h + 1], (q, q))
            seg = a_col - jnp.broadcast_to(col_t[h:h + 1, :], (q, q))
            m_h = (cb * jnp.exp2(jnp.where(mask, seg, -jnp.inf))).astype(BF16)
            c_h = (cg32 * jnp.exp2(a_col)).astype(BF16)
            lhs = jnp.concatenate([m_h, c_h], axis=1)
            if state_rows:
                w_h = (bgt * jnp.broadcast_to(dtd_t[h:h + 1, :], (SSD_STATE, q))).astype(BF16)
                lhs = jnp.concatenate(
                    [lhs, jnp.concatenate([w_h, jnp.zeros((SSD_STATE, SSD_STATE), BF16)], axis=1)], axis=0)
            res = jnp.dot(lhs, rhs, preferred_element_type=F32)
            res_g = res if res_g is None else jnp.where(lane_res < (k + 1) * p, res, res_g)
        ys.append(res_g[:q])
        if state_rows:
            new = res_g[q:]
        else:
            xdd = (xs[:, g * gw:(g + 1) * gw] * expand(dtd, g)).astype(BF16)
            new = jnp.dot(bgt.astype(BF16), xdd, preferred_element_type=F32)
        st_ref[bb * SSD_GROUPS + g] = st * expand(cdec, g) + new
    y = jnp.concatenate(ys, axis=1)

    if finish:
        y = yf_ref[bb] + y + xs * dsk_ref[...]
        y = y * _silu(z_ref[bb].astype(F32))
        o_ref[bb] = _rms(y, ng_ref[...]).astype(o_ref.dtype)
    else:
        o_ref[bb] = y


def _ssd_direction(p0, s0, dt_bias, a_log, n_lat, reverse, conv=None, extra=None):
    b, n, _ = p0.shape
    q = SSD_CHUNK
    nc = n // q
    nl = n_lat // q
    halo = BF16_ROWS
    hpc = q // halo
    nh = n // halo
    d = 1 if reverse else 0
    nb = _pick(b, (SSD_BATCH_PER_STEP, 1))
    cidx = functools.partial(_ssd_chunk_index, nl=nl, nc=nc, reverse=reverse)
    finish = extra is not None
    row_spec = lambda w, col=0: pl.BlockSpec((nb, q, w), lambda i, s: (i, cidx(s), col))
    vec_spec = lambda w: pl.BlockSpec((1, w), lambda i, s: (0, 0))
    if finish:
        act, yf, d_skip, norm_g = extra
        in_specs = [row_spec(SSD_CONV_DIM), row_spec(LANES, d), vec_spec(LANES), vec_spec(LANES),
                    row_spec(SSD_INNER, SSD_CONV_DIM // SSD_INNER), row_spec(SSD_INNER),
                    vec_spec(SSD_INNER), vec_spec(SSD_INNER)]
        args = [act, s0, dt_bias, a_log, p0, yf, d_skip, norm_g]
        out_specs = row_spec(SSD_INNER)
        out_shape = jax.ShapeDtypeStruct((b, n, SSD_INNER), BF16)
    else:
        conv_w, conv_b = conv
        in_specs = [
            row_spec(SSD_CONV_DIM),
            pl.BlockSpec((nb, halo, SSD_CONV_DIM), lambda i, s: (i, jnp.maximum(cidx(s) * hpc - 1, 0), 0)),
            pl.BlockSpec((nb, halo, SSD_CONV_DIM), lambda i, s: (i, jnp.minimum(cidx(s) * hpc + hpc, nh - 1), 0)),
            row_spec(LANES, d),
            pl.BlockSpec((3, SSD_CONV_DIM), lambda i, s: (0, 0)),
            vec_spec(SSD_CONV_DIM), vec_spec(LANES), vec_spec(LANES),
        ]
        args = [p0, p0, p0, s0, conv_w, conv_b, dt_bias, a_log]
        out_specs = [row_spec(SSD_INNER), row_spec(SSD_CONV_DIM)]
        out_shape = [jax.ShapeDtypeStruct((b, n, SSD_INNER), F32), jax.ShapeDtypeStruct((b, n, SSD_CONV_DIM), BF16)]
    kern = functools.partial(_ssd_kernel, reverse=reverse, finish=finish, nl=nl, nc=nc)
    return pl.pallas_call(
        kern,
        grid=(b // nb, nc),
        in_specs=in_specs,
        out_specs=out_specs,
        out_shape=out_shape,
        scratch_shapes=[pltpu.VMEM((nb * SSD_GROUPS, SSD_STATE, (SSD_HEADS // SSD_GROUPS) * SSD_HEADDIM), F32)],
        compiler_params=_params(("arbitrary", "arbitrary")),
        name="ssd_bwd_finish" if finish else "ssd_fwd",
    )(*args)


def _rope_tables(rows, rot_dim, n_ctx):
    n_freq = rot_dim // 4
    row = jnp.repeat(jnp.arange(rows, dtype=F32), GRID_W)
    col = jnp.tile(jnp.arange(GRID_W, dtype=F32), rows)
    inv = ROPE_THETA ** (-jnp.arange(n_freq, dtype=F32) / n_freq)
    ang = jnp.concatenate([row[:, None] * inv, col[:, None] * inv], axis=-1)
    cos, sin = jnp.cos(ang), jnp.sin(ang)
    reps = LANES // rot_dim
    cos_t = jnp.tile(jnp.concatenate([cos, cos], axis=-1), (1, reps))
    sin_t = jnp.tile(jnp.concatenate([-sin, sin], axis=-1), (1, reps))
    cos_t = jnp.concatenate([cos_t, jnp.ones((n_ctx, LANES), F32)], axis=0)
    sin_t = jnp.concatenate([sin_t, jnp.zeros((n_ctx, LANES), F32)], axis=0)
    return cos_t, sin_t


def _rope_block(xb, cos, sin, rot_dim):
    half = rot_dim // 2
    lane = lax.broadcasted_iota(jnp.int32, xb.shape, 1)
    first = jnp.bitwise_and(lane, rot_dim - 1) < half
    partner = jnp.where(first, pltpu.roll(xb, LANES - half, axis=1), pltpu.roll(xb, half, axis=1))
    return xb * cos + partner * sin


def _segment_mean_matrix(seg):
    sh = seg.bit_length() - 1
    i = jnp.right_shift(lax.broadcasted_iota(jnp.int32, (LANES, LANES), 0), sh)
    j = jnp.right_shift(lax.broadcasted_iota(jnp.int32, (LANES, LANES), 1), sh)
    return jnp.where(i == j, 1.0 / seg, 0.0).astype(BF16)


def _segment_mean_sq(xb, smat):
    x2 = xb * xb
    hi = x2.astype(BF16)
    lo = (x2 - hi.astype(F32)).astype(BF16)
    return (jnp.dot(hi, smat, preferred_element_type=F32) + jnp.dot(lo, smat, preferred_element_type=F32))


def _store_key_norms(ksq_ref, c, sumsq):
    mx = jnp.max(sumsq, axis=0, keepdims=True)
    ksq_ref[0, 0, :, (2 * c) * LANES:(2 * c + 1) * LANES] = mx
    ksq_ref[0, 0, :, (2 * c + 1) * LANES:(2 * c + 2) * LANES] = pltpu.roll(mx, LANES // 2, axis=1)


def _qk_prep_kernel(q_ref, k_ref, v_ref, cos_ref, sin_ref, qg_ref, kg_ref, qo_ref, ko_ref, vo_ref, ksq_ref, *, scale):
    smat = _segment_mean_matrix(ATT_HEADDIM)
    cos = cos_ref[...]
    sin = sin_ref[...]
    hpb = LANES // ATT_HEADDIM

    def prep(src_ref, g_ref, dst_ref, n_heads, mul, transposed):
        for c in range(n_heads // hpb):
            xb = src_ref[0, :, c * LANES:(c + 1) * LANES].astype(F32)
            xn = xb * lax.rsqrt(_segment_mean_sq(xb, smat) + EPS) * g_ref[...]
            r = _rope_block(xn, cos, sin, ATT_HEADDIM)
            if mul != 1.0:
                r = r * mul
            if transposed:
                r = r.T
            else:
                _store_key_norms(ksq_ref, c, _segment_mean_sq(r, smat) * float(ATT_HEADDIM))
            r = r.astype(dst_ref.dtype)
            for k in range(hpb):
                if transposed:
                    dst_ref[0, c * hpb + k] = r[k * ATT_HEADDIM:(k + 1) * ATT_HEADDIM, :]
                else:
                    dst_ref[0, c * hpb + k] = r[:, k * ATT_HEADDIM:(k + 1) * ATT_HEADDIM]

    prep(q_ref, qg_ref, qo_ref, ATT_HEADS, scale, True)
    prep(k_ref, kg_ref, ko_ref, ATT_KV_HEADS, 1.0, False)
    for c in range(ATT_KV_HEADS // hpb):
        vt = v_ref[0, :, c * LANES:(c + 1) * LANES].astype(F32).T.astype(vo_ref.dtype)
        for k in range(hpb):
            vo_ref[0, c * hpb + k] = vt[k * ATT_HEADDIM:(k + 1) * ATT_HEADDIM, :]


def _qk_prep(p0, cos_t, sin_t, q_g, k_g, q_col, k_col, v_col):
    b, n, _ = p0.shape
    tm = _pick(n, (768, 384, 256, 128))
    kern = functools.partial(_qk_prep_kernel, scale=ATT_HEADDIM ** -0.5 * LOG2E)
    return pl.pallas_call(
        kern,
        grid=(b, n // tm),
        in_specs=[
            pl.BlockSpec((1, tm, ATT_Q), lambda i, t: (i, t, q_col // ATT_Q)),
            pl.BlockSpec((1, tm, ATT_KV), lambda i, t: (i, t, k_col // ATT_KV)),
            pl.BlockSpec((1, tm, ATT_KV), lambda i, t: (i, t, v_col // ATT_KV)),
            pl.BlockSpec((tm, LANES), lambda i, t: (t, 0)),
            pl.BlockSpec((tm, LANES), lambda i, t: (t, 0)),
            pl.BlockSpec((1, LANES), lambda i, t: (0, 0)),
            pl.BlockSpec((1, LANES), lambda i, t: (0, 0)),
        ],
        out_specs=[
            pl.BlockSpec((1, ATT_HEADS, ATT_HEADDIM, tm), lambda i, t: (i, 0, 0, t)),
            pl.BlockSpec((1, ATT_KV_HEADS, tm, ATT_HEADDIM), lambda i, t: (i, 0, t, 0)),
            pl.BlockSpec((1, ATT_KV_HEADS, ATT_HEADDIM, tm), lambda i, t: (i, 0, 0, t)),
            pl.BlockSpec((1, 1, 1, ATT_KV_HEADS * LANES), lambda i, t: (i, t, 0, 0)),
        ],
        out_shape=[
            jax.ShapeDtypeStruct((b, ATT_HEADS, ATT_HEADDIM, n), BF16),
            jax.ShapeDtypeStruct((b, ATT_KV_HEADS, n, ATT_HEADDIM), BF16),
            jax.ShapeDtypeStruct((b, ATT_KV_HEADS, ATT_HEADDIM, n), BF16),
            jax.ShapeDtypeStruct((b, n // tm, 1, ATT_KV_HEADS * LANES), F32),
        ],
        compiler_params=_params(("arbitrary", "arbitrary")),
        name="gqa_qk_prep",
    )(p0, p0, p0, cos_t, sin_t, q_g, k_g)


def _attn_kernel(qt_ref, k_ref, vt_ref, ksq_ref, o_ref, s_scr, m_scr, e_scr, gap_scr, ot_scr, *,
                 group, hpb, n_lat, n_all, lat_tiles, ctx_self):
    t = pl.program_id(2)
    dv = vt_ref.shape[2]
    tq = qt_ref.shape[3]
    n_blocks = qt_ref.shape[1] // hpb

    ksq_max = jnp.max(ksq_ref[0], axis=0)

    def queries(i):
        return jnp.concatenate([qt_ref[0, i * hpb + j] for j in range(hpb)], axis=1)

    def finish(i, ot):
        ot = ot[:dv] * (1.0 / ot[dv:dv + 1])
        for j in range(hpb):
            ot_scr[(i * hpb + j) * dv:(i * hpb + j + 1) * dv, :] = ot[:, j * tq:(j + 1) * tq]

    def run(k_lo, k_len, exact):
        row = lax.broadcasted_iota(jnp.int32, (BF16_ROWS, k_len), 0)
        ones_row = jnp.where(row == 0, 1.0, 0.0).astype(BF16)

        def scores(i):
            kv = i * hpb // group
            qt = queries(i)
            s = jnp.dot(k_ref[0, kv, k_lo:k_lo + k_len, :], qt, preferred_element_type=F32)
            m = jnp.max(s, axis=0, keepdims=True)
            if exact:
                s_scr[i % 2, :k_len, :] = s
                m_scr[i % 2] = m
            else:
                qf = qt.astype(F32)
                ksq = ksq_max[:, kv * LANES:kv * LANES + 1]
                bound = jnp.sqrt(jnp.sum(qf * qf, axis=0, keepdims=True) * ksq) * ATTN_BOUND_SLACK
                e_scr[i % 2, :k_len, :] = jnp.exp2(s - bound).astype(BF16)
                gap_scr[i] = bound - m

        def combine(i):
            if exact:
                e = jnp.exp2(s_scr[i % 2, :k_len, :] - m_scr[i % 2]).astype(BF16)
            else:
                e = e_scr[i % 2, :k_len, :]
            vt = vt_ref[0, i * hpb // group, :, k_lo:k_lo + k_len]
            finish(i, jnp.dot(jnp.concatenate([vt, ones_row], axis=0), e, preferred_element_type=F32))

        scores(0)
        for i in range(n_blocks):
            if i + 1 < n_blocks:
                scores(i + 1)
            combine(i)

    def attend(k_lo, k_len):
        run(k_lo, k_len, exact=False)

        @pl.when(jnp.max(gap_scr[...]) > ATTN_MAX_SHIFT)
        def _():
            run(k_lo, k_len, exact=True)

        o_ref[0] = ot_scr[...].T.astype(o_ref.dtype)

    if ctx_self:
        @pl.when(t < lat_tiles)
        def _():
            attend(0, n_all)

        @pl.when(t >= lat_tiles)
        def _():
            attend(n_lat, n_all - n_lat)
    else:
        attend(0, n_all)


def _attention(qt, k, vt, ksq, n_lat, ctx_self):
    b, h, dqk, nq = qt.shape
    hkv, dv, n_all = vt.shape[1], vt.shape[2], vt.shape[3]
    group = h // hkv
    if ctx_self:
        tq = _pick(math.gcd(n_lat, n_all - n_lat), (256, 128))
    else:
        tq = _pick(n_lat, (ATTN_BLOCK_LANES, 256, 128))
    hpb = max(1, min(group, ATTN_BLOCK_LANES // tq))
    heads = min(h, ATTN_BLOCKS_PER_STEP * hpb)
    kvb = max(heads // group, 1)
    assert nq == (n_all if ctx_self else n_lat)
    kern = functools.partial(_attn_kernel, group=group, hpb=hpb, n_lat=n_lat, n_all=n_all,
                             lat_tiles=n_lat // tq, ctx_self=ctx_self)
    return pl.pallas_call(
        kern,
        grid=(b, h // heads, nq // tq),
        in_specs=[
            pl.BlockSpec((1, heads, dqk, tq), lambda i, hb, t: (i, hb, 0, t)),
            pl.BlockSpec((1, kvb, n_all, dqk), lambda i, hb, t: (i, hb, 0, 0)),
            pl.BlockSpec((1, kvb, dv, n_all), lambda i, hb, t: (i, hb, 0, 0)),
            pl.BlockSpec((1, ksq.shape[1], 1, kvb * LANES), lambda i, hb, t: (i, 0, 0, hb)),
        ],
        out_specs=pl.BlockSpec((1, tq, heads * dv), lambda i, hb, t: (i, t, hb)),
        out_shape=jax.ShapeDtypeStruct((b, nq, h * dv), BF16),
        scratch_shapes=[pltpu.VMEM((2, n_all, tq * hpb), F32), pltpu.VMEM((2, 1, tq * hpb), F32),
                        pltpu.VMEM((2, n_all, tq * hpb), BF16), pltpu.VMEM((heads // hpb, 1, tq * hpb), F32),
                        pltpu.VMEM((heads * dv, tq), F32)],
        compiler_params=_params(("arbitrary", "arbitrary", "arbitrary")),
        name="attention",
    )(qt, k, vt, ksq)


def _hidden_chunks(hidden):
    chunks, lo = [], 0
    while lo < hidden:
        w = min(1024, hidden - lo)
        chunks.append((lo, w))
        lo += w
    return chunks


def _mix_ffn_kernel(*refs, n_in, tm, n_lat, final):
    x_ref, ml_ref, mc_ref, g_ref = refs[:4]
    a_refs = refs[4:4 + n_in]
    w_refs = refs[4 + n_in:4 + 2 * n_in]
    wg_ref, wu_ref, wd_ref, fg_ref, o_ref = refs[4 + 2 * n_in:]
    t = pl.program_id(1)
    is_lat = _is_lat(t, tm, n_lat)
    mod = functools.partial(_mod_row, ml_ref, mc_ref, is_lat=is_lat)
    mix = jnp.dot(a_refs[0][0], w_refs[0][...], preferred_element_type=F32)
    for a_ref, w_ref in zip(a_refs[1:], w_refs[1:]):
        mix = mix + jnp.dot(a_ref[0], w_ref[...], preferred_element_type=F32)
    x = x_ref[0] + mod(2) * mix
    h = (_rms(x, g_ref[...]) * (1.0 + mod(4)) + mod(3)).astype(BF16)
    acc = None
    for lo, w in _hidden_chunks(wd_ref.shape[0]):
        gate = jnp.dot(h, wg_ref[:, lo:lo + w], preferred_element_type=F32)
        up = jnp.dot(h, wu_ref[:, lo:lo + w], preferred_element_type=F32)
        a = (_silu(gate) * up).astype(BF16)
        part = jnp.dot(a, wd_ref[lo:lo + w, :], preferred_element_type=F32)
        acc = part if acc is None else acc + part
    y = x + mod(5) * acc
    if final:
        y = _rms(y, fg_ref[...])
    o_ref[0] = y


def _mix_ffn(xa, mods_l, mods_c, g, acts, weights, w_gate, w_up, w_down, final_g, n_rows, n_lat, final):
    b, _, d = xa.shape
    hid = w_down.shape[0]
    tm = _pick(n_rows, (576, 512, 384, 256, 128))
    n_in = len(acts)
    kern = functools.partial(_mix_ffn_kernel, n_in=n_in, tm=tm, n_lat=n_lat, final=final)
    resident = dict(pipeline_mode=pl.Buffered(1))
    in_specs = [
        pl.BlockSpec((1, tm, d), lambda i, t: (i, t, 0)),
        pl.BlockSpec((1, ADA_CHUNKS, d), lambda i, t: (i, 0, 0)),
        pl.BlockSpec((ADA_CHUNKS, d), lambda i, t: (0, 0)),
        pl.BlockSpec((1, d), lambda i, t: (0, 0)),
    ]
    in_specs += [pl.BlockSpec((1, tm, a.shape[2]), lambda i, t: (i, t, 0)) for a in acts]
    in_specs += [pl.BlockSpec(w.shape, lambda i, t: (0, 0), **resident) for w in weights]
    in_specs += [
        pl.BlockSpec((d, hid), lambda i, t: (0, 0), **resident),
        pl.BlockSpec((d, hid), lambda i, t: (0, 0), **resident),
        pl.BlockSpec((hid, d), lambda i, t: (0, 0), **resident),
        pl.BlockSpec((1, d), lambda i, t: (0, 0)),
    ]
    return pl.pallas_call(
        kern,
        grid=(b, n_rows // tm),
        in_specs=in_specs,
        out_specs=pl.BlockSpec((1, tm, d), lambda i, t: (i, t, 0)),
        out_shape=jax.ShapeDtypeStruct((b, n_rows, d), F32),
        compiler_params=_params(("arbitrary", "arbitrary")),
        name="mix_ffn_final" if final else "mix_ffn",
    )(xa, mods_l, mods_c, g, *acts, *weights, w_gate, w_up, w_down, final_g)


def _mla_q_kernel(p_ref, g_ref, w_ref, cos_ref, sin_ref, o_ref, *, q_rank, scale):
    n_nope = MLA_HEADS * MLA_NOPE
    ppb = LANES // MLA_ROPE
    npb = LANES // MLA_NOPE
    gs = g_ref[...] * scale
    rc = _pick(p_ref.shape[1], (256, 128))
    for r in range(0, p_ref.shape[1], rc):
        cq = p_ref[0, r:r + rc, :q_rank].astype(F32)
        qn = _rms(cq, gs).astype(BF16)
        qq = jnp.dot(qn, w_ref[...], preferred_element_type=F32)
        cos, sin = cos_ref[r:r + rc, :], sin_ref[r:r + rc, :]
        pes_t = [_rope_block(qq[:, n_nope + c * LANES:n_nope + (c + 1) * LANES], cos, sin, MLA_ROPE).T
                 for c in range(MLA_HEADS // ppb)]
        for c in range(MLA_HEADS // npb):
            nope_t = qq[:, c * LANES:(c + 1) * LANES].T
            for k in range(npb):
                h = c * npb + k
                pe_t = pes_t[h // ppb][(h % ppb) * MLA_ROPE:(h % ppb + 1) * MLA_ROPE, :]
                qh = jnp.concatenate([nope_t[k * MLA_NOPE:(k + 1) * MLA_NOPE, :], pe_t], axis=0)
                o_ref[0, h, :, r:r + rc] = qh.astype(o_ref.dtype)


def _mla_q(p1, g, w_uq, cos_t, sin_t, n_lat, q_rank):
    b = p1.shape[0]
    cols = p1.shape[2]
    tm = _pick(n_lat, (512, 256, 128))
    kern = functools.partial(_mla_q_kernel, q_rank=q_rank, scale=MLA_QK ** -0.5 * LOG2E)
    return pl.pallas_call(
        kern,
        grid=(b, n_lat // tm),
        in_specs=[
            pl.BlockSpec((1, tm, cols), lambda i, t: (i, t, 0)),
            pl.BlockSpec((1, q_rank), lambda i, t: (0, 0)),
            pl.BlockSpec(w_uq.shape, lambda i, t: (0, 0)),
            pl.BlockSpec((tm, LANES), lambda i, t: (t, 0)),
            pl.BlockSpec((tm, LANES), lambda i, t: (t, 0)),
        ],
        out_specs=pl.BlockSpec((1, MLA_HEADS, MLA_QK, tm), lambda i, t: (i, 0, 0, t)),
        out_shape=jax.ShapeDtypeStruct((b, MLA_HEADS, MLA_QK, n_lat), BF16),
        compiler_params=_params(("arbitrary", "arbitrary")),
        name="mla_q",
    )(p1, g, w_uq, cos_t, sin_t)


def _mla_kv_kernel(p_ref, pe_ref, g_ref, w_ref, cos_ref, sin_ref, ko_ref, vo_ref, ksq_ref, *, q_rank):
    ckv = p_ref[0, :, q_rank:].astype(F32)
    kvn = _rms(ckv, g_ref[...]).astype(BF16)
    kv = jnp.dot(kvn, w_ref[...], preferred_element_type=F32)
    pe_block = _rope_block(pe_ref[0], cos_ref[...], sin_ref[...], MLA_ROPE)
    pe = pe_block[:, :MLA_ROPE]
    n_nope = MLA_HEADS * MLA_NOPE
    for h in range(MLA_HEADS):
        kh = jnp.concatenate([kv[:, h * MLA_NOPE:(h + 1) * MLA_NOPE], pe], axis=1)
        ko_ref[0, h] = kh.astype(ko_ref.dtype)
    pe_sq = jnp.sum(pe_block * pe_block, axis=1, keepdims=True)
    smat = _segment_mean_matrix(MLA_NOPE)
    for c in range(MLA_HEADS * MLA_NOPE // LANES):
        nope_sq = _segment_mean_sq(kv[:, c * LANES:(c + 1) * LANES], smat) * float(MLA_NOPE)
        _store_key_norms(ksq_ref, c, nope_sq + pe_sq)
    hpb = LANES // MLA_V
    for c in range(MLA_HEADS // hpb):
        v_t = kv[:, n_nope + c * LANES:n_nope + (c + 1) * LANES].T.astype(vo_ref.dtype)
        for k in range(hpb):
            vo_ref[0, c * hpb + k] = v_t[k * MLA_V:(k + 1) * MLA_V, :]


def _mla_kv(p1, s1, g, w_ukv, cos_t, sin_t, q_rank):
    b, n, cols = p1.shape
    kv_rank = cols - q_rank
    tm = _pick(n, (768, 384, 256, 128))
    kern = functools.partial(_mla_kv_kernel, q_rank=q_rank)
    return pl.pallas_call(
        kern,
        grid=(b, n // tm),
        in_specs=[
            pl.BlockSpec((1, tm, cols), lambda i, t: (i, t, 0)),
            pl.BlockSpec((1, tm, LANES), lambda i, t: (i, t, 0)),
            pl.BlockSpec((1, kv_rank), lambda i, t: (0, 0)),
            pl.BlockSpec(w_ukv.shape, lambda i, t: (0, 0)),
            pl.BlockSpec((tm, LANES), lambda i, t: (t, 0)),
            pl.BlockSpec((tm, LANES), lambda i, t: (t, 0)),
        ],
        out_specs=[
            pl.BlockSpec((1, MLA_HEADS, tm, MLA_QK), lambda i, t: (i, 0, t, 0)),
            pl.BlockSpec((1, MLA_HEADS, MLA_V, tm), lambda i, t: (i, 0, 0, t)),
            pl.BlockSpec((1, 1, 1, MLA_HEADS * LANES), lambda i, t: (i, t, 0, 0)),
        ],
        out_shape=[
            jax.ShapeDtypeStruct((b, MLA_HEADS, n, MLA_QK), BF16),
            jax.ShapeDtypeStruct((b, MLA_HEADS, MLA_V, n), BF16),
            jax.ShapeDtypeStruct((b, n // tm, 1, MLA_HEADS * LANES), F32),
        ],
        compiler_params=_params(("arbitrary", "arbitrary")),
        name="mla_kv",
    )(p1, s1, g, w_ukv, cos_t, sin_t)


def _pad_lanes(v, width=LANES):
    v = v.reshape(1, -1).astype(F32)
    return jnp.pad(v, ((0, 0), (0, width - v.shape[1])))


def kernel(x, c, ctx, c_ctx, ada_w, ada_b, norm1_g, norm2_g, ffn_w_up, ffn_w_down, ab_w_in, ab_w_out, ssd_conv_w, ssd_conv_b, ssd_a_log, ssd_dt_bias, ssd_d, ssd_norm_g, att_q_g, att_k_g, mla_w_in, mla_q_norm_g, mla_w_uq, mla_kv_norm_g, mla_w_ukv, mla_w_o, final_norm_g):
    b, t_lat, d = x.shape
    n_ctx = ctx.shape[1]
    rows = t_lat // GRID_W
    hid = ffn_w_down.shape[1]
    row2 = lambda v: v.reshape(1, -1).astype(F32)

    bp = -(-(b + 1) // 8) * 8
    cvec = jnp.concatenate([c, c_ctx[None, :], jnp.zeros((bp - b - 1, d), F32)], axis=0)
    mods = _ada_mods(cvec, ada_w, ada_b)
    mods_l = [mods[i, :b].reshape(b, ADA_CHUNKS, d) for i in range(2)]
    mods_c = [mods[i, b].reshape(ADA_CHUNKS, d) for i in range(2)]

    w_in = ab_w_in[0]
    o_xbc, o_dt = SSD_INNER, SSD_INNER + SSD_CONV_DIM
    o_att = o_dt + 2 * SSD_HEADS
    w_main = jnp.concatenate([w_in[:, o_xbc:o_dt], w_in[:, :SSD_INNER], w_in[:, o_att:]], axis=1).astype(BF16)
    w_dt = jnp.zeros((d, 2 * LANES), F32)
    w_dt = w_dt.at[:, :SSD_HEADS].set(w_in[:, o_dt:o_dt + SSD_HEADS])
    w_dt = w_dt.at[:, LANES:LANES + SSD_HEADS].set(w_in[:, o_dt + SSD_HEADS:o_att]).astype(BF16)
    p0, s0, xa = _in_proj([x, ctx], mods_l[0], mods_c[0], row2(norm1_g[0]), w_main, w_dt, t_lat, n_ctx)

    yf, act = _ssd_direction(p0, s0, _pad_lanes(ssd_dt_bias[0, 0]), _pad_lanes(ssd_a_log[0, 0]), t_lat,
                             reverse=False, conv=(ssd_conv_w[0].astype(F32), row2(ssd_conv_b[0])))
    d_skip = jnp.repeat(ssd_d[0].astype(F32), SSD_HEADDIM).reshape(1, SSD_INNER)
    y_ssd = _ssd_direction(p0, s0, _pad_lanes(ssd_dt_bias[0, 1]), _pad_lanes(ssd_a_log[0, 1]), t_lat,
                           reverse=True, extra=(act, yf, d_skip, row2(ssd_norm_g[0])))

    q_col = SSD_CONV_DIM + SSD_INNER
    k_col = q_col + ATT_Q
    cos_a, sin_a = _rope_tables(rows, ATT_HEADDIM, n_ctx)
    hpb = LANES // ATT_HEADDIM
    qh, kh, vh, ksq0 = _qk_prep(p0, cos_a, sin_a, jnp.tile(row2(att_q_g[0]), (1, hpb)),
                                jnp.tile(row2(att_k_g[0]), (1, hpb)), q_col, k_col, k_col + ATT_KV)
    o_att_l0 = _attention(qh, kh, vh, ksq0, t_lat, ctx_self=True)

    w_out = ab_w_out[0].astype(BF16)
    w_up = ffn_w_up[0].astype(BF16)
    xa = _mix_ffn(xa, mods_l[0], mods_c[0], row2(norm2_g[0]), [y_ssd, o_att_l0],
                  [w_out[:SSD_INNER], w_out[SSD_INNER:]], w_up[:, :hid], w_up[:, hid:],
                  ffn_w_down[0].astype(BF16), row2(final_norm_g), t_lat + n_ctx, t_lat, final=False)

    q_rank = mla_q_norm_g.shape[1]
    kv_rank = mla_kv_norm_g.shape[1]
    w_in1 = mla_w_in[0]
    w_pe = jnp.pad(w_in1[:, q_rank + kv_rank:], ((0, 0), (0, LANES - MLA_ROPE))).astype(BF16)
    p1, s1 = _in_proj([xa], mods_l[1], mods_c[1], row2(norm1_g[1]), w_in1[:, :q_rank + kv_rank].astype(BF16), w_pe,
                      t_lat, n_ctx)
    cos_m, sin_m = _rope_tables(rows, MLA_ROPE, n_ctx)
    w_uq = mla_w_uq[0].reshape(q_rank, MLA_HEADS, MLA_QK)
    w_uq = jnp.concatenate([w_uq[:, :, :MLA_NOPE].reshape(q_rank, -1), w_uq[:, :, MLA_NOPE:].reshape(q_rank, -1)],
                           axis=1).astype(BF16)
    w_ukv = mla_w_ukv[0].reshape(kv_rank, MLA_HEADS, MLA_NOPE + MLA_V)
    w_ukv = jnp.concatenate([w_ukv[:, :, :MLA_NOPE].reshape(kv_rank, -1), w_ukv[:, :, MLA_NOPE:].reshape(kv_rank, -1)],
                            axis=1).astype(BF16)
    qm = _mla_q(p1, row2(mla_q_norm_g[0]), w_uq, cos_m, sin_m, t_lat, q_rank)
    km, vm, ksq1 = _mla_kv(p1, s1, row2(mla_kv_norm_g[0]), w_ukv, cos_m, sin_m, q_rank)
    o_mla = _attention(qm, km, vm, ksq1, t_lat, ctx_self=False)

    w_up = ffn_w_up[1].astype(BF16)
    return _mix_ffn(xa, mods_l[1], mods_c[1], row2(norm2_g[1]), [o_mla], [mla_w_o[0].astype(BF16)],
                    w_up[:, :hid], w_up[:, hid:], ffn_w_down[1].astype(BF16), row2(final_norm_g),
                    t_lat, t_lat, final=True)
```

```python
import functools
import math

import jax
import jax.numpy as jnp
from jax import lax
from jax.experimental import pallas as pl
from jax.experimental.pallas import tpu as pltpu

F32 = jnp.float32
BF16 = jnp.bfloat16

EPS = 1e-6
ROPE_THETA = 10000.0
GRID_W = 64
ADA_CHUNKS = 6

SSD_HEADS = 16
SSD_HEADDIM = 64
SSD_GROUPS = 4
SSD_STATE = 128
SSD_CHUNK = 128
SSD_BATCH_PER_STEP = 4
SSD_INNER = SSD_HEADS * SSD_HEADDIM
SSD_GN = SSD_GROUPS * SSD_STATE
SSD_CONV_DIM = SSD_INNER + 2 * SSD_GN

ATT_HEADS = 16
ATT_KV_HEADS = 4
ATT_HEADDIM = 64
ATT_Q = ATT_HEADS * ATT_HEADDIM
ATT_KV = ATT_KV_HEADS * ATT_HEADDIM

MLA_HEADS = 16
MLA_NOPE = 64
MLA_ROPE = 32
MLA_V = 64
MLA_QK = MLA_NOPE + MLA_ROPE

LANES = 128
SUBLANES = 8
BF16_ROWS = 16
ATTN_BLOCKS_PER_STEP = 8
ATTN_BOUND_SLACK = 1.0 + 2.0 ** -6
ATTN_MAX_SHIFT = 64.0
ATTN_BLOCK_LANES = 512
LOG2E = math.log2(math.e)
VMEM_LIMIT = 56 * 1024 * 1024


def _pick(n, candidates):
    for c in candidates:
        if n % c == 0:
            return c
    raise ValueError(f"no tile for {n} in {candidates}")


def _params(sem):
    return pltpu.CompilerParams(dimension_semantics=sem, vmem_limit_bytes=VMEM_LIMIT)


def _silu(x):
    hx = 0.5 * x
    return hx + hx * jnp.tanh(hx)


def _rms(x, g):
    ms = jnp.mean(x * x, axis=-1, keepdims=True)
    return x * lax.rsqrt(ms + EPS) * g


def _mod_row(ml_ref, mc_ref, idx, is_lat):
    return jnp.where(is_lat, ml_ref[0, idx:idx + 1, :], mc_ref[idx:idx + 1, :])


def _is_lat(t, tm, n_lat):
    rows = t * tm + lax.broadcasted_iota(jnp.int32, (tm, 1), 0)
    return rows < n_lat


def _ada_kernel(c_ref, w_ref, b_ref, o_ref):
    s = _silu(c_ref[...]).astype(BF16)
    o_ref[0] = jnp.dot(s, w_ref[0].astype(BF16), preferred_element_type=F32) + b_ref[0]


def _ada_mods(cvec, ada_w, ada_b):
    depth, d, n = ada_w.shape
    bp = cvec.shape[0]
    tn = _pick(n, (1536, 1024, 512, 256, 128))
    return pl.pallas_call(
        _ada_kernel,
        grid=(depth, n // tn),
        in_specs=[
            pl.BlockSpec((bp, d), lambda i, j: (0, 0)),
            pl.BlockSpec((1, d, tn), lambda i, j: (i, 0, j)),
            pl.BlockSpec((1, 1, tn), lambda i, j: (i, 0, j)),
        ],
        out_specs=pl.BlockSpec((1, bp, tn), lambda i, j: (i, 0, j)),
        out_shape=jax.ShapeDtypeStruct((depth, bp, n), F32),
        compiler_params=_params(("arbitrary", "arbitrary")),
        name="ada_mods",
    )(cvec, ada_w, ada_b.reshape(depth, 1, n))


def _in_proj_kernel(*refs, n_src, n_lat, emit_rows):
    srcs = refs[:n_src]
    ml_ref, mc_ref, g_ref, w_ref, ws_ref = refs[n_src:n_src + 5]
    o_ref, os_ref = refs[n_src + 5:n_src + 7]
    t = pl.program_id(1)
    tm = o_ref.shape[1]
    rc = _pick(tm, (128,))
    for r in range(0, tm, rc):
        lat = (t * tm + r + lax.broadcasted_iota(jnp.int32, (rc, 1), 0)) < n_lat
        shift = _mod_row(ml_ref, mc_ref, 0, lat)
        scale = _mod_row(ml_ref, mc_ref, 1, lat)
        xin = srcs[0][0, r:r + rc, :]
        if n_src == 2:
            xin = jnp.where(lat, xin, srcs[1][0, r:r + rc, :])
        if emit_rows:
            refs[n_src + 7][0, r:r + rc, :] = xin
        hb = (_rms(xin, g_ref[...]) * (1.0 + scale) + shift).astype(BF16)
        os_ref[0, r:r + rc, :] = jnp.dot(hb, ws_ref[...], preferred_element_type=F32)
        o_ref[0, r:r + rc, :] = jnp.dot(hb, w_ref[...], preferred_element_type=F32).astype(o_ref.dtype)


def _in_proj(srcs, mods_l, mods_c, g, w_main, w_side, n_lat, n_ctx):
    b, _, d = srcs[0].shape
    n = n_lat + n_ctx
    nm = w_main.shape[1]
    ns = w_side.shape[1]
    split = len(srcs) == 2
    tm = _pick(math.gcd(n_lat, n_ctx), (256, 128)) if split else _pick(n, (768, 384, 256, 128))
    lat_tiles = n_lat // tm
    kern = functools.partial(_in_proj_kernel, n_src=len(srcs), n_lat=n_lat, emit_rows=split)
    if split:
        src_specs = [pl.BlockSpec((1, tm, d), lambda i, t: (i, jnp.minimum(t, lat_tiles - 1), 0)),
                     pl.BlockSpec((1, tm, d), lambda i, t: (i, jnp.maximum(t - lat_tiles, 0), 0))]
    else:
        src_specs = [pl.BlockSpec((1, tm, d), lambda i, t: (i, t, 0))]
    resident = dict(pipeline_mode=pl.Buffered(1))
    out_specs = [pl.BlockSpec((1, tm, nm), lambda i, t: (i, t, 0)),
                 pl.BlockSpec((1, tm, ns), lambda i, t: (i, t, 0))]
    out_shape = [jax.ShapeDtypeStruct((b, n, nm), BF16), jax.ShapeDtypeStruct((b, n, ns), F32)]
    if split:
        out_specs.append(pl.BlockSpec((1, tm, d), lambda i, t: (i, t, 0)))
        out_shape.append(jax.ShapeDtypeStruct((b, n, d), F32))
    return pl.pallas_call(
        kern,
        grid=(b, n // tm),
        in_specs=src_specs + [
            pl.BlockSpec((1, ADA_CHUNKS, d), lambda i, t: (i, 0, 0)),
            pl.BlockSpec((ADA_CHUNKS, d), lambda i, t: (0, 0)),
            pl.BlockSpec((1, d), lambda i, t: (0, 0)),
            pl.BlockSpec((d, nm), lambda i, t: (0, 0), **resident),
            pl.BlockSpec((d, ns), lambda i, t: (0, 0), **resident),
        ],
        out_specs=out_specs,
        out_shape=out_shape,
        compiler_params=_params(("arbitrary", "arbitrary")),
        name="in_proj",
    )(*srcs, mods_l, mods_c, g, w_main, w_side)


def _cumsum_rows(v, reverse):
    n = v.shape[0]
    row = lax.broadcasted_iota(jnp.int32, v.shape, 0)
    k = 1
    while k < n:
        if reverse:
            v = v + jnp.where(row < n - k, pltpu.roll(v, n - k, axis=0), 0.0)
        else:
            v = v + jnp.where(row >= k, pltpu.roll(v, k, axis=0), 0.0)
        k *= 2
    return v


def _ssd_chunk_index(s, nl, nc, reverse):
    if reverse:
        return nc - 1 - s
    return lax.rem(s + nl, nc)


def _ssd_kernel(*refs, reverse, finish, nl, nc):
    st_ref = refs[-1]

    @pl.when(pl.program_id(1) == 0)
    def _():
        st_ref[...] = jnp.zeros_like(st_ref)

    for bb in range(refs[0].shape[0]):
        _ssd_chunk(bb, *refs, reverse=reverse, finish=finish, nl=nl, nc=nc)


def _ssd_conv_act(bb, xm_ref, xp_ref, xn_ref, cw_ref, cb_ref, chunk, nl, nc):
    q = SSD_CHUNK
    xmb = xm_ref[bb]
    xm = xmb.astype(F32)
    halo = xp_ref.shape[1]
    seq_first = jnp.logical_or(chunk == 0, chunk == nl)
    seq_last = jnp.logical_or(chunk == nl - 1, chunk == nc - 1)
    prev_row = xp_ref[bb, halo - 1:halo, :].astype(F32) * jnp.where(seq_first, 0.0, 1.0)
    next_row = xn_ref[bb, 0:1, :].astype(F32) * jnp.where(seq_last, 0.0, 1.0)
    ii = lax.broadcasted_iota(jnp.int32, (q, q), 0)
    jj = lax.broadcasted_iota(jnp.int32, (q, q), 1)
    shift_dn = jnp.where(ii == jj + 1, 1.0, 0.0).astype(BF16)
    shift_up = jnp.where(ii + 1 == jj, 1.0, 0.0).astype(BF16)
    x_prev = jnp.dot(shift_dn, xmb, preferred_element_type=F32)
    x_next = jnp.dot(shift_up, xmb, preferred_element_type=F32)
    row8 = lax.broadcasted_iota(jnp.int32, (SUBLANES, 1), 0)
    x_prev = jnp.concatenate(
        [x_prev[:SUBLANES] + jnp.where(row8 == 0, prev_row, 0.0), x_prev[SUBLANES:]], axis=0)
    x_next = jnp.concatenate(
        [x_next[:q - SUBLANES], x_next[q - SUBLANES:] + jnp.where(row8 == SUBLANES - 1, next_row, 0.0)], axis=0)
    hw = 0.5 * cw_ref[...]
    hc = x_prev * hw[0:1, :] + xm * hw[1:2, :] + x_next * hw[2:3, :] + 0.5 * cb_ref[...]
    return hc + hc * jnp.tanh(hc)


def _ssd_chunk(bb, *refs, reverse, finish, nl, nc):
    q = SSD_CHUNK
    p = SSD_HEADDIM
    hpg = SSD_HEADS // SSD_GROUPS
    gw = hpg * p
    chunk = _ssd_chunk_index(pl.program_id(1), nl, nc, reverse)
    if finish:
        act_ref, dt_ref, dtb_ref, alog_ref, z_ref, yf_ref, dsk_ref, ng_ref, o_ref, st_ref = refs
        actb = act_ref[bb]
        act = actb.astype(F32)
    else:
        xm_ref, xp_ref, xn_ref, dt_ref, cw_ref, cb_ref, dtb_ref, alog_ref, o_ref, act_ref, st_ref = refs
        act = _ssd_conv_act(bb, xm_ref, xp_ref, xn_ref, cw_ref, cb_ref, chunk, nl, nc)
        actb = act.astype(BF16)
        act_ref[bb] = actb
    ii = lax.broadcasted_iota(jnp.int32, (q, q), 0)
    jj = lax.broadcasted_iota(jnp.int32, (q, q), 1)
    xs = act[:, :SSD_INNER]
    xsb = actb[:, :SSD_INNER]
    bm = act[:, SSD_INNER:SSD_INNER + SSD_GN]
    cm = act[:, SSD_INNER + SSD_GN:]

    dtr = dt_ref[bb] + dtb_ref[...]
    dt = jnp.maximum(dtr, 0.0) + jnp.log(1.0 + jnp.exp(-jnp.abs(dtr)))
    a = -jnp.exp(alog_ref[...])
    acs = _cumsum_rows(dt * a, reverse) * LOG2E
    col_t = (acs - jnp.log2(dt)).T
    edge = acs[0:1, :] if reverse else acs[q - 1:q, :]
    dtd = dt * jnp.exp2(edge - acs)
    state_rows = finish
    dtd_t = dtd.T if state_rows else None
    cdec = jnp.exp2(edge)
    mask = (ii <= jj) if reverse else (ii >= jj)
    lane_h = lax.broadcasted_iota(jnp.int32, (q, LANES), 1)
    lane_res = lax.broadcasted_iota(jnp.int32, (q + SSD_STATE if state_rows else q, gw), 1)

    def expand(v, g):
        rows = v.shape[0]
        parts = []
        for k in range(0, hpg, LANES // p):
            h0 = g * hpg + k
            lo = jnp.broadcast_to(v[:, h0:h0 + 1], (rows, LANES))
            hi = jnp.broadcast_to(v[:, h0 + 1:h0 + 2], (rows, LANES))
            parts.append(jnp.where(lane_h[:rows] < p, lo, hi))
        return jnp.concatenate(parts, axis=1)

    ys = []
    for g in range(SSD_GROUPS):
        cg32 = cm[:, g * SSD_STATE:(g + 1) * SSD_STATE]
        bg32 = bm[:, g * SSD_STATE:(g + 1) * SSD_STATE]
        cb = lax.dot_general(cg32.astype(BF16), bg32.astype(BF16), (((1,), (1,)), ((), ())),
                             preferred_element_type=F32)
        st = st_ref[bb * SSD_GROUPS + g]
        bgt = bg32.T
        rhs = jnp.concatenate([xsb[:, g * gw:(g + 1) * gw], st.astype(BF16)], axis=0)
        res_g = None
        for k in reversed(range(hpg)):
            h = g * hpg + k
            a_col = jnp.broadcast_to(acs[:, h:h + 1], (q, q))
            seg = a_col - jnp.broadcast_to(col_t[h:h + 1, :], (q, q))
            m_h = (cb * jnp.exp2(jnp.where(mask, seg, -jnp.inf))).astype(BF16)
            c_h = (cg32 * jnp.exp2(a_col)).astype(BF16)
            lhs = jnp.concatenate([m_h, c_h], axis=1)
            if state_rows:
                w_h = (bgt * jnp.broadcast_to(dtd_t[h:h + 1, :], (SSD_STATE, q))).astype(BF16)
                lhs = jnp.concatenate(
                    [lhs, jnp.concatenate([w_h, jnp.zeros((SSD_STATE, SSD_STATE), BF16)], axis=1)], axis=0)
            res = jnp.dot(lhs, rhs, preferred_element_type=F32)
            res_g = res if res_g is None else jnp.where(lane_res < (k + 1) * p, res, res_g)
        ys.append(res_g[:q])
        if state_rows:
            new = res_g[q:]
        else:
            xdd = (xs[:, g * gw:(g + 1) * gw] * expand(dtd, g)).astype(BF16)
            new = jnp.dot(bgt.astype(BF16), xdd, preferred_element_type=F32)
        st_ref[bb * SSD_GROUPS + g] = st * expand(cdec, g) + new
    y = jnp.concatenate(ys, axis=1)

    if finish:
        y = yf_ref[bb] + y + xs * dsk_ref[...]
        y = y * _silu(z_ref[bb].astype(F32))
        o_ref[bb] = _rms(y, ng_ref[...]).astype(o_ref.dtype)
    else:
        o_ref[bb] = y


def _ssd_direction(p0, s0, dt_bias, a_log, n_lat, reverse, conv=None, extra=None):
    b, n, _ = p0.shape
    q = SSD_CHUNK
    nc = n // q
    nl = n_lat // q
    halo = BF16_ROWS
    hpc = q // halo
    nh = n // halo
    d = 1 if reverse else 0
    nb = _pick(b, (SSD_BATCH_PER_STEP, 1))
    cidx = functools.partial(_ssd_chunk_index, nl=nl, nc=nc, reverse=reverse)
    finish = extra is not None
    row_spec = lambda w, col=0: pl.BlockSpec((nb, q, w), lambda i, s: (i, cidx(s), col))
    vec_spec = lambda w: pl.BlockSpec((1, w), lambda i, s: (0, 0))
    if finish:
        act, yf, d_skip, norm_g = extra
        in_specs = [row_spec(SSD_CONV_DIM), row_spec(LANES, d), vec_spec(LANES), vec_spec(LANES),
                    row_spec(SSD_INNER, SSD_CONV_DIM // SSD_INNER), row_spec(SSD_INNER),
                    vec_spec(SSD_INNER), vec_spec(SSD_INNER)]
        args = [act, s0, dt_bias, a_log, p0, yf, d_skip, norm_g]
        out_specs = row_spec(SSD_INNER)
        out_shape = jax.ShapeDtypeStruct((b, n, SSD_INNER), BF16)
    else:
        conv_w, conv_b = conv
        in_specs = [
            row_spec(SSD_CONV_DIM),
            pl.BlockSpec((nb, halo, SSD_CONV_DIM), lambda i, s: (i, jnp.maximum(cidx(s) * hpc - 1, 0), 0)),
            pl.BlockSpec((nb, halo, SSD_CONV_DIM), lambda i, s: (i, jnp.minimum(cidx(s) * hpc + hpc, nh - 1), 0)),
            row_spec(LANES, d),
            pl.BlockSpec((3, SSD_CONV_DIM), lambda i, s: (0, 0)),
            vec_spec(SSD_CONV_DIM), vec_spec(LANES), vec_spec(LANES),
        ]
        args = [p0, p0, p0, s0, conv_w, conv_b, dt_bias, a_log]
        out_specs = [row_spec(SSD_INNER), row_spec(SSD_CONV_DIM)]
        out_shape = [jax.ShapeDtypeStruct((b, n, SSD_INNER), F32), jax.ShapeDtypeStruct((b, n, SSD_CONV_DIM), BF16)]
    kern = functools.partial(_ssd_kernel, reverse=reverse, finish=finish, nl=nl, nc=nc)
    return pl.pallas_call(
        kern,
        grid=(b // nb, nc),
        in_specs=in_specs,
        out_specs=out_specs,
        out_shape=out_shape,
        scratch_shapes=[pltpu.VMEM((nb * SSD_GROUPS, SSD_STATE, (SSD_HEADS // SSD_GROUPS) * SSD_HEADDIM), F32)],
        compiler_params=_params(("arbitrary", "arbitrary")),
        name="ssd_bwd_finish" if finish else "ssd_fwd",
    )(*args)


def _rope_tables(rows, rot_dim, n_ctx):
    n_freq = rot_dim // 4
    row = jnp.repeat(jnp.arange(rows, dtype=F32), GRID_W)
    col = jnp.tile(jnp.arange(GRID_W, dtype=F32), rows)
    inv = ROPE_THETA ** (-jnp.arange(n_freq, dtype=F32) / n_freq)
    ang = jnp.concatenate([row[:, None] * inv, col[:, None] * inv], axis=-1)
    cos, sin = jnp.cos(ang), jnp.sin(ang)
    reps = LANES // rot_dim
    cos_t = jnp.tile(jnp.concatenate([cos, cos], axis=-1), (1, reps))
    sin_t = jnp.tile(jnp.concatenate([-sin, sin], axis=-1), (1, reps))
    cos_t = jnp.concatenate([cos_t, jnp.ones((n_ctx, LANES), F32)], axis=0)
    sin_t = jnp.concatenate([sin_t, jnp.zeros((n_ctx, LANES), F32)], axis=0)
    return cos_t, sin_t


def _rope_block(xb, cos, sin, rot_dim):
    half = rot_dim // 2
    lane = lax.broadcasted_iota(jnp.int32, xb.shape, 1)
    first = jnp.bitwise_and(lane, rot_dim - 1) < half
    partner = jnp.where(first, pltpu.roll(xb, LANES - half, axis=1), pltpu.roll(xb, half, axis=1))
    return xb * cos + partner * sin


def _segment_mean_matrix(seg):
    sh = seg.bit_length() - 1
    i = jnp.right_shift(lax.broadcasted_iota(jnp.int32, (LANES, LANES), 0), sh)
    j = jnp.right_shift(lax.broadcasted_iota(jnp.int32, (LANES, LANES), 1), sh)
    return jnp.where(i == j, 1.0 / seg, 0.0).astype(BF16)


def _segment_mean_sq(xb, smat):
    x2 = xb * xb
    hi = x2.astype(BF16)
    lo = (x2 - hi.astype(F32)).astype(BF16)
    return (jnp.dot(hi, smat, preferred_element_type=F32) + jnp.dot(lo, smat, preferred_element_type=F32))


def _store_key_norms(ksq_ref, c, sumsq):
    mx = jnp.max(sumsq, axis=0, keepdims=True)
    ksq_ref[0, 0, :, (2 * c) * LANES:(2 * c + 1) * LANES] = mx
    ksq_ref[0, 0, :, (2 * c + 1) * LANES:(2 * c + 2) * LANES] = pltpu.roll(mx, LANES // 2, axis=1)


def _qk_prep_kernel(q_ref, k_ref, v_ref, cos_ref, sin_ref, qg_ref, kg_ref, qo_ref, ko_ref, vo_ref, ksq_ref, *, scale):
    smat = _segment_mean_matrix(ATT_HEADDIM)
    cos = cos_ref[...]
    sin = sin_ref[...]
    hpb = LANES // ATT_HEADDIM

    def prep(src_ref, g_ref, dst_ref, n_heads, mul, transposed):
        for c in range(n_heads // hpb):
            xb = src_ref[0, :, c * LANES:(c + 1) * LANES].astype(F32)
            xn = xb * lax.rsqrt(_segment_mean_sq(xb, smat) + EPS) * g_ref[...]
            r = _rope_block(xn, cos, sin, ATT_HEADDIM)
            if mul != 1.0:
                r = r * mul
            if transposed:
                r = r.T
            else:
                _store_key_norms(ksq_ref, c, _segment_mean_sq(r, smat) * float(ATT_HEADDIM))
            r = r.astype(dst_ref.dtype)
            for k in range(hpb):
                if transposed:
                    dst_ref[0, c * hpb + k] = r[k * ATT_HEADDIM:(k + 1) * ATT_HEADDIM, :]
                else:
                    dst_ref[0, c * hpb + k] = r[:, k * ATT_HEADDIM:(k + 1) * ATT_HEADDIM]

    prep(q_ref, qg_ref, qo_ref, ATT_HEADS, scale, True)
    prep(k_ref, kg_ref, ko_ref, ATT_KV_HEADS, 1.0, False)
    for c in range(ATT_KV_HEADS // hpb):
        vt = v_ref[0, :, c * LANES:(c + 1) * LANES].astype(F32).T.astype(vo_ref.dtype)
        for k in range(hpb):
            vo_ref[0, c * hpb + k] = vt[k * ATT_HEADDIM:(k + 1) * ATT_HEADDIM, :]


def _qk_prep(p0, cos_t, sin_t, q_g, k_g, q_col, k_col, v_col):
    b, n, _ = p0.shape
    tm = _pick(n, (768, 384, 256, 128))
    kern = functools.partial(_qk_prep_kernel, scale=ATT_HEADDIM ** -0.5 * LOG2E)
    return pl.pallas_call(
        kern,
        grid=(b, n // tm),
        in_specs=[
            pl.BlockSpec((1, tm, ATT_Q), lambda i, t: (i, t, q_col // ATT_Q)),
            pl.BlockSpec((1, tm, ATT_KV), lambda i, t: (i, t, k_col // ATT_KV)),
            pl.BlockSpec((1, tm, ATT_KV), lambda i, t: (i, t, v_col // ATT_KV)),
            pl.BlockSpec((tm, LANES), lambda i, t: (t, 0)),
            pl.BlockSpec((tm, LANES), lambda i, t: (t, 0)),
            pl.BlockSpec((1, LANES), lambda i, t: (0, 0)),
            pl.BlockSpec((1, LANES), lambda i, t: (0, 0)),
        ],
        out_specs=[
            pl.BlockSpec((1, ATT_HEADS, ATT_HEADDIM, tm), lambda i, t: (i, 0, 0, t)),
            pl.BlockSpec((1, ATT_KV_HEADS, tm, ATT_HEADDIM), lambda i, t: (i, 0, t, 0)),
            pl.BlockSpec((1, ATT_KV_HEADS, ATT_HEADDIM, tm), lambda i, t: (i, 0, 0, t)),
            pl.BlockSpec((1, 1, 1, ATT_KV_HEADS * LANES), lambda i, t: (i, t, 0, 0)),
        ],
        out_shape=[
            jax.ShapeDtypeStruct((b, ATT_HEADS, ATT_HEADDIM, n), BF16),
            jax.ShapeDtypeStruct((b, ATT_KV_HEADS, n, ATT_HEADDIM), BF16),
            jax.ShapeDtypeStruct((b, ATT_KV_HEADS, ATT_HEADDIM, n), BF16),
            jax.ShapeDtypeStruct((b, n // tm, 1, ATT_KV_HEADS * LANES), F32),
        ],
        compiler_params=_params(("arbitrary", "arbitrary")),
        name="gqa_qk_prep",
    )(p0, p0, p0, cos_t, sin_t, q_g, k_g)


def _attn_kernel(qt_ref, k_ref, vt_ref, ksq_ref, o_ref, s_scr, m_scr, e_scr, gap_scr, ot_scr, *,
                 group, hpb, n_lat, n_all, lat_tiles, ctx_self):
    t = pl.program_id(2)
    dv = vt_ref.shape[2]
    tq = qt_ref.shape[3]
    n_blocks = qt_ref.shape[1] // hpb

    ksq_max = jnp.max(ksq_ref[0], axis=0)

    def queries(i):
        return jnp.concatenate([qt_ref[0, i * hpb + j] for j in range(hpb)], axis=1)

    def finish(i, ot):
        ot = ot[:dv] * (1.0 / ot[dv:dv + 1])
        for j in range(hpb):
            ot_scr[(i * hpb + j) * dv:(i * hpb + j + 1) * dv, :] = ot[:, j * tq:(j + 1) * tq]

    def run(k_lo, k_len, exact):
        row = lax.broadcasted_iota(jnp.int32, (BF16_ROWS, k_len), 0)
        ones_row = jnp.where(row == 0, 1.0, 0.0).astype(BF16)

        def scores(i):
            kv = i * hpb // group
            qt = queries(i)
            s = jnp.dot(k_ref[0, kv, k_lo:k_lo + k_len, :], qt, preferred_element_type=F32)
            m = jnp.max(s, axis=0, keepdims=True)
            if exact:
                s_scr[i % 2, :k_len, :] = s
                m_scr[i % 2] = m
            else:
                qf = qt.astype(F32)
                ksq = ksq_max[:, kv * LANES:kv * LANES + 1]
                bound = jnp.sqrt(jnp.sum(qf * qf, axis=0, keepdims=True) * ksq) * ATTN_BOUND_SLACK
                e_scr[i % 2, :k_len, :] = jnp.exp2(s - bound).astype(BF16)
                gap_scr[i] = bound - m

        def combine(i):
            if exact:
                e = jnp.exp2(s_scr[i % 2, :k_len, :] - m_scr[i % 2]).astype(BF16)
            else:
                e = e_scr[i % 2, :k_len, :]
            vt = vt_ref[0, i * hpb // group, :, k_lo:k_lo + k_len]
            finish(i, jnp.dot(jnp.concatenate([vt, ones_row], axis=0), e, preferred_element_type=F32))

        scores(0)
        for i in range(n_blocks):
            if i + 1 < n_blocks:
                scores(i + 1)
            combine(i)

    def attend(k_lo, k_len):
        run(k_lo, k_len, exact=False)

        @pl.when(jnp.max(gap_scr[...]) > ATTN_MAX_SHIFT)
        def _():
            run(k_lo, k_len, exact=True)

        o_ref[0] = ot_scr[...].T.astype(o_ref.dtype)

    if ctx_self:
        @pl.when(t < lat_tiles)
        def _():
            attend(0, n_all)

        @pl.when(t >= lat_tiles)
        def _():
            attend(n_lat, n_all - n_lat)
    else:
        attend(0, n_all)


def _attention(qt, k, vt, ksq, n_lat, ctx_self):
    b, h, dqk, nq = qt.shape
    hkv, dv, n_all = vt.shape[1], vt.shape[2], vt.shape[3]
    group = h // hkv
    if ctx_self:
        tq = _pick(math.gcd(n_lat, n_all - n_lat), (256, 128))
    else:
        tq = _pick(n_lat, (ATTN_BLOCK_LANES, 256, 128))
    hpb = max(1, min(group, ATTN_BLOCK_LANES // tq))
    heads = min(h, ATTN_BLOCKS_PER_STEP * hpb)
    kvb = max(heads // group, 1)
    assert nq == (n_all if ctx_self else n_lat)
    kern = functools.partial(_attn_kernel, group=group, hpb=hpb, n_lat=n_lat, n_all=n_all,
                             lat_tiles=n_lat // tq, ctx_self=ctx_self)
    return pl.pallas_call(
        kern,
        grid=(b, h // heads, nq // tq),
        in_specs=[
            pl.BlockSpec((1, heads, dqk, tq), lambda i, hb, t: (i, hb, 0, t)),
            pl.BlockSpec((1, kvb, n_all, dqk), lambda i, hb, t: (i, hb, 0, 0)),
            pl.BlockSpec((1, kvb, dv, n_all), lambda i, hb, t: (i, hb, 0, 0)),
            pl.BlockSpec((1, ksq.shape[1], 1, kvb * LANES), lambda i, hb, t: (i, 0, 0, hb)),
        ],
        out_specs=pl.BlockSpec((1, tq, heads * dv), lambda i, hb, t: (i, t, hb)),
        out_shape=jax.ShapeDtypeStruct((b, nq, h * dv), BF16),
        scratch_shapes=[pltpu.VMEM((2, n_all, tq * hpb), F32), pltpu.VMEM((2, 1, tq * hpb), F32),
                        pltpu.VMEM((2, n_all, tq * hpb), BF16), pltpu.VMEM((heads // hpb, 1, tq * hpb), F32),
                        pltpu.VMEM((heads * dv, tq), F32)],
        compiler_params=_params(("arbitrary", "arbitrary", "arbitrary")),
        name="attention",
    )(qt, k, vt, ksq)


def _hidden_chunks(hidden):
    chunks, lo = [], 0
    while lo < hidden:
        w = min(1024, hidden - lo)
        chunks.append((lo, w))
        lo += w
    return chunks


def _mix_ffn_kernel(*refs, n_in, tm, n_lat, final):
    x_ref, ml_ref, mc_ref, g_ref = refs[:4]
    a_refs = refs[4:4 + n_in]
    w_refs = refs[4 + n_in:4 + 2 * n_in]
    wg_ref, wu_ref, wd_ref, fg_ref, o_ref = refs[4 + 2 * n_in:]
    t = pl.program_id(1)
    is_lat = _is_lat(t, tm, n_lat)
    mod = functools.partial(_mod_row, ml_ref, mc_ref, is_lat=is_lat)
    mix = jnp.dot(a_refs[0][0], w_refs[0][...], preferred_element_type=F32)
    for a_ref, w_ref in zip(a_refs[1:], w_refs[1:]):
        mix = mix + jnp.dot(a_ref[0], w_ref[...], preferred_element_type=F32)
    x = x_ref[0] + mod(2) * mix
    h = (_rms(x, g_ref[...]) * (1.0 + mod(4)) + mod(3)).astype(BF16)
    acc = None
    for lo, w in _hidden_chunks(wd_ref.shape[0]):
        gate = jnp.dot(h, wg_ref[:, lo:lo + w], preferred_element_type=F32)
        up = jnp.dot(h, wu_ref[:, lo:lo + w], preferred_element_type=F32)
        a = (_silu(gate) * up).astype(BF16)
        part = jnp.dot(a, wd_ref[lo:lo + w, :], preferred_element_type=F32)
        acc = part if acc is None else acc + part
    y = x + mod(5) * acc
    if final:
        y = _rms(y, fg_ref[...])
    o_ref[0] = y


def _mix_ffn(xa, mods_l, mods_c, g, acts, w_mix, w_gate_up, w_down, final_g, n_rows, n_lat, final):
    b, _, d = xa.shape
    hid = w_down.shape[0]
    tm = _pick(n_rows, (576, 512, 384, 256, 128))
    n_in = len(acts)
    k_act = acts[0].shape[2]
    assert all(a.shape[2] == k_act for a in acts) and w_mix.shape[0] == n_in * k_act
    kern = functools.partial(_mix_ffn_kernel, n_in=n_in, tm=tm, n_lat=n_lat, final=final)
    resident = dict(pipeline_mode=pl.Buffered(1))
    in_specs = [
        pl.BlockSpec((1, tm, d), lambda i, t: (i, t, 0)),
        pl.BlockSpec((1, ADA_CHUNKS, d), lambda i, t: (i, 0, 0)),
        pl.BlockSpec((ADA_CHUNKS, d), lambda i, t: (0, 0)),
        pl.BlockSpec((1, d), lambda i, t: (0, 0)),
    ]
    in_specs += [pl.BlockSpec((1, tm, a.shape[2]), lambda i, t: (i, t, 0)) for a in acts]
    in_specs += [pl.BlockSpec((k_act, d), lambda i, t, j=j: (j, 0), **resident) for j in range(n_in)]
    in_specs += [
        pl.BlockSpec((d, hid), lambda i, t: (0, 0), **resident),
        pl.BlockSpec((d, hid), lambda i, t: (0, 1), **resident),
        pl.BlockSpec((hid, d), lambda i, t: (0, 0), **resident),
        pl.BlockSpec((1, d), lambda i, t: (0, 0)),
    ]
    return pl.pallas_call(
        kern,
        grid=(b, n_rows // tm),
        in_specs=in_specs,
        out_specs=pl.BlockSpec((1, tm, d), lambda i, t: (i, t, 0)),
        out_shape=jax.ShapeDtypeStruct((b, n_rows, d), F32),
        compiler_params=_params(("arbitrary", "arbitrary")),
        name="mix_ffn_final" if final else "mix_ffn",
    )(xa, mods_l, mods_c, g, *acts, *([w_mix] * n_in), w_gate_up, w_gate_up, w_down, final_g)


def _mla_q_kernel(p_ref, g_ref, w_ref, cos_ref, sin_ref, o_ref, *, q_rank, scale):
    n_nope = MLA_HEADS * MLA_NOPE
    ppb = LANES // MLA_ROPE
    npb = LANES // MLA_NOPE
    gs = g_ref[...] * scale
    rc = _pick(p_ref.shape[1], (256, 128))
    for r in range(0, p_ref.shape[1], rc):
        cq = p_ref[0, r:r + rc, :q_rank].astype(F32)
        qn = _rms(cq, gs).astype(BF16)
        qq = jnp.dot(qn, w_ref[...], preferred_element_type=F32)
        cos, sin = cos_ref[r:r + rc, :], sin_ref[r:r + rc, :]
        pes_t = [_rope_block(qq[:, n_nope + c * LANES:n_nope + (c + 1) * LANES], cos, sin, MLA_ROPE).T
                 for c in range(MLA_HEADS // ppb)]
        for c in range(MLA_HEADS // npb):
            nope_t = qq[:, c * LANES:(c + 1) * LANES].T
            for k in range(npb):
                h = c * npb + k
                pe_t = pes_t[h // ppb][(h % ppb) * MLA_ROPE:(h % ppb + 1) * MLA_ROPE, :]
                qh = jnp.concatenate([nope_t[k * MLA_NOPE:(k + 1) * MLA_NOPE, :], pe_t], axis=0)
                o_ref[0, h, :, r:r + rc] = qh.astype(o_ref.dtype)


def _mla_q(p1, g, w_uq, cos_t, sin_t, n_lat, q_rank):
    b = p1.shape[0]
    cols = p1.shape[2]
    tm = _pick(n_lat, (512, 256, 128))
    kern = functools.partial(_mla_q_kernel, q_rank=q_rank, scale=MLA_QK ** -0.5 * LOG2E)
    return pl.pallas_call(
        kern,
        grid=(b, n_lat // tm),
        in_specs=[
            pl.BlockSpec((1, tm, cols), lambda i, t: (i, t, 0)),
            pl.BlockSpec((1, q_rank), lambda i, t: (0, 0)),
            pl.BlockSpec(w_uq.shape, lambda i, t: (0, 0)),
            pl.BlockSpec((tm, LANES), lambda i, t: (t, 0)),
            pl.BlockSpec((tm, LANES), lambda i, t: (t, 0)),
        ],
        out_specs=pl.BlockSpec((1, MLA_HEADS, MLA_QK, tm), lambda i, t: (i, 0, 0, t)),
        out_shape=jax.ShapeDtypeStruct((b, MLA_HEADS, MLA_QK, n_lat), BF16),
        compiler_params=_params(("arbitrary", "arbitrary")),
        name="mla_q",
    )(p1, g, w_uq, cos_t, sin_t)


def _mla_kv_kernel(p_ref, pe_ref, g_ref, w_ref, cos_ref, sin_ref, ko_ref, vo_ref, ksq_ref, *, q_rank):
    ckv = p_ref[0, :, q_rank:].astype(F32)
    kvn = _rms(ckv, g_ref[...]).astype(BF16)
    kv = jnp.dot(kvn, w_ref[...], preferred_element_type=F32)
    pe_block = _rope_block(pe_ref[0], cos_ref[...], sin_ref[...], MLA_ROPE)
    pe = pe_block[:, :MLA_ROPE]
    n_nope = MLA_HEADS * MLA_NOPE
    for h in range(MLA_HEADS):
        kh = jnp.concatenate([kv[:, h * MLA_NOPE:(h + 1) * MLA_NOPE], pe], axis=1)
        ko_ref[0, h] = kh.astype(ko_ref.dtype)
    pe_sq = jnp.sum(pe_block * pe_block, axis=1, keepdims=True)
    smat = _segment_mean_matrix(MLA_NOPE)
    for c in range(MLA_HEADS * MLA_NOPE // LANES):
        nope_sq = _segment_mean_sq(kv[:, c * LANES:(c + 1) * LANES], smat) * float(MLA_NOPE)
        _store_key_norms(ksq_ref, c, nope_sq + pe_sq)
    hpb = LANES // MLA_V
    for c in range(MLA_HEADS // hpb):
        v_t = kv[:, n_nope + c * LANES:n_nope + (c + 1) * LANES].T.astype(vo_ref.dtype)
        for k in range(hpb):
            vo_ref[0, c * hpb + k] = v_t[k * MLA_V:(k + 1) * MLA_V, :]


def _mla_kv(p1, s1, g, w_ukv, cos_t, sin_t, q_rank):
    b, n, cols = p1.shape
    kv_rank = cols - q_rank
    tm = _pick(n, (768, 384, 256, 128))
    kern = functools.partial(_mla_kv_kernel, q_rank=q_rank)
    return pl.pallas_call(
        kern,
        grid=(b, n // tm),
        in_specs=[
            pl.BlockSpec((1, tm, cols), lambda i, t: (i, t, 0)),
            pl.BlockSpec((1, tm, LANES), lambda i, t: (i, t, 0)),
            pl.BlockSpec((1, kv_rank), lambda i, t: (0, 0)),
            pl.BlockSpec(w_ukv.shape, lambda i, t: (0, 0)),
            pl.BlockSpec((tm, LANES), lambda i, t: (t, 0)),
            pl.BlockSpec((tm, LANES), lambda i, t: (t, 0)),
        ],
        out_specs=[
            pl.BlockSpec((1, MLA_HEADS, tm, MLA_QK), lambda i, t: (i, 0, t, 0)),
            pl.BlockSpec((1, MLA_HEADS, MLA_V, tm), lambda i, t: (i, 0, 0, t)),
            pl.BlockSpec((1, 1, 1, MLA_HEADS * LANES), lambda i, t: (i, t, 0, 0)),
        ],
        out_shape=[
            jax.ShapeDtypeStruct((b, MLA_HEADS, n, MLA_QK), BF16),
            jax.ShapeDtypeStruct((b, MLA_HEADS, MLA_V, n), BF16),
            jax.ShapeDtypeStruct((b, n // tm, 1, MLA_HEADS * LANES), F32),
        ],
        compiler_params=_params(("arbitrary", "arbitrary")),
        name="mla_kv",
    )(p1, s1, g, w_ukv, cos_t, sin_t)


def _pad_lanes(v, width=LANES):
    v = v.reshape(1, -1).astype(F32)
    return jnp.pad(v, ((0, 0), (0, width - v.shape[1])))


def kernel(x, c, ctx, c_ctx, ada_w, ada_b, norm1_g, norm2_g, ffn_w_up, ffn_w_down, ab_w_in, ab_w_out, ssd_conv_w, ssd_conv_b, ssd_a_log, ssd_dt_bias, ssd_d, ssd_norm_g, att_q_g, att_k_g, mla_w_in, mla_q_norm_g, mla_w_uq, mla_kv_norm_g, mla_w_ukv, mla_w_o, final_norm_g):
    b, t_lat, d = x.shape
    n_ctx = ctx.shape[1]
    rows = t_lat // GRID_W
    row2 = lambda v: v.reshape(1, -1).astype(F32)

    bp = -(-(b + 1) // 8) * 8
    cvec = jnp.concatenate([c, c_ctx[None, :], jnp.zeros((bp - b - 1, d), F32)], axis=0)
    mods = _ada_mods(cvec, ada_w, ada_b)
    mods_l = [mods[i, :b].reshape(b, ADA_CHUNKS, d) for i in range(2)]
    mods_c = [mods[i, b].reshape(ADA_CHUNKS, d) for i in range(2)]

    w_in = ab_w_in[0]
    o_xbc, o_dt = SSD_INNER, SSD_INNER + SSD_CONV_DIM
    o_att = o_dt + 2 * SSD_HEADS
    w_main = jnp.concatenate([w_in[:, o_xbc:o_dt], w_in[:, :SSD_INNER], w_in[:, o_att:]], axis=1).astype(BF16)
    pad_heads = lambda w: jnp.pad(w, ((0, 0), (0, LANES - SSD_HEADS)))
    w_dt = jnp.concatenate([pad_heads(w_in[:, o_dt:o_dt + SSD_HEADS]), pad_heads(w_in[:, o_dt + SSD_HEADS:o_att])],
                           axis=1).astype(BF16)
    p0, s0, xa = _in_proj([x, ctx], mods_l[0], mods_c[0], row2(norm1_g[0]), w_main, w_dt, t_lat, n_ctx)

    yf, act = _ssd_direction(p0, s0, _pad_lanes(ssd_dt_bias[0, 0]), _pad_lanes(ssd_a_log[0, 0]), t_lat,
                             reverse=False, conv=(ssd_conv_w[0].astype(F32), row2(ssd_conv_b[0])))
    d_skip = jnp.repeat(ssd_d[0].astype(F32), SSD_HEADDIM).reshape(1, SSD_INNER)
    y_ssd = _ssd_direction(p0, s0, _pad_lanes(ssd_dt_bias[0, 1]), _pad_lanes(ssd_a_log[0, 1]), t_lat,
                           reverse=True, extra=(act, yf, d_skip, row2(ssd_norm_g[0])))

    q_col = SSD_CONV_DIM + SSD_INNER
    k_col = q_col + ATT_Q
    cos_a, sin_a = _rope_tables(rows, ATT_HEADDIM, n_ctx)
    hpb = LANES // ATT_HEADDIM
    qh, kh, vh, ksq0 = _qk_prep(p0, cos_a, sin_a, jnp.tile(row2(att_q_g[0]), (1, hpb)),
                                jnp.tile(row2(att_k_g[0]), (1, hpb)), q_col, k_col, k_col + ATT_KV)
    o_att_l0 = _attention(qh, kh, vh, ksq0, t_lat, ctx_self=True)

    xa = _mix_ffn(xa, mods_l[0], mods_c[0], row2(norm2_g[0]), [y_ssd, o_att_l0], ab_w_out[0].astype(BF16),
                  ffn_w_up[0].astype(BF16), ffn_w_down[0].astype(BF16), row2(final_norm_g), t_lat + n_ctx, t_lat,
                  final=False)

    q_rank = mla_q_norm_g.shape[1]
    kv_rank = mla_kv_norm_g.shape[1]
    w_in1 = mla_w_in[0]
    w_pe = jnp.pad(w_in1[:, q_rank + kv_rank:], ((0, 0), (0, LANES - MLA_ROPE))).astype(BF16)
    p1, s1 = _in_proj([xa], mods_l[1], mods_c[1], row2(norm1_g[1]), w_in1[:, :q_rank + kv_rank].astype(BF16), w_pe,
                      t_lat, n_ctx)
    cos_m, sin_m = _rope_tables(rows, MLA_ROPE, n_ctx)
    w_uq = mla_w_uq[0].reshape(q_rank, MLA_HEADS, MLA_QK)
    w_uq = jnp.concatenate([w_uq[:, :, :MLA_NOPE].reshape(q_rank, -1), w_uq[:, :, MLA_NOPE:].reshape(q_rank, -1)],
                           axis=1).astype(BF16)
    w_ukv = mla_w_ukv[0].reshape(kv_rank, MLA_HEADS, MLA_NOPE + MLA_V)
    w_ukv = jnp.concatenate([w_ukv[:, :, :MLA_NOPE].reshape(kv_rank, -1), w_ukv[:, :, MLA_NOPE:].reshape(kv_rank, -1)],
                            axis=1).astype(BF16)
    qm = _mla_q(p1, row2(mla_q_norm_g[0]), w_uq, cos_m, sin_m, t_lat, q_rank)
    km, vm, ksq1 = _mla_kv(p1, s1, row2(mla_kv_norm_g[0]), w_ukv, cos_m, sin_m, q_rank)
    o_mla = _attention(qm, km, vm, ksq1, t_lat, ctx_self=False)

    return _mix_ffn(xa, mods_l[1], mods_c[1], row2(norm2_g[1]), [o_mla], mla_w_o[0].astype(BF16),
                    ffn_w_up[1].astype(BF16), ffn_w_down[1].astype(BF16), row2(final_norm_g), t_lat, t_lat,
                    final=True)
```

```python
import functools
import math

import jax
import jax.numpy as jnp
from jax import lax
from jax.experimental import pallas as pl
from jax.experimental.pallas import tpu as pltpu

F32 = jnp.float32
BF16 = jnp.bfloat16

EPS = 1e-6
ROPE_THETA = 10000.0
GRID_W = 64
ADA_CHUNKS = 6

SSD_HEADS = 16
SSD_HEADDIM = 64
SSD_GROUPS = 4
SSD_STATE = 128
SSD_CHUNK = 128
SSD_BATCH_PER_STEP = 4
SSD_INNER = SSD_HEADS * SSD_HEADDIM
SSD_GN = SSD_GROUPS * SSD_STATE
SSD_CONV_DIM = SSD_INNER + 2 * SSD_GN

ATT_HEADS = 16
ATT_KV_HEADS = 4
ATT_HEADDIM = 64
ATT_Q = ATT_HEADS * ATT_HEADDIM
ATT_KV = ATT_KV_HEADS * ATT_HEADDIM

MLA_HEADS = 16
MLA_NOPE = 64
MLA_ROPE = 32
MLA_V = 64
MLA_QK = MLA_NOPE + MLA_ROPE

LANES = 128
SUBLANES = 8
BF16_ROWS = 16
ATTN_BLOCKS_PER_STEP = 8
ATTN_BOUND_SLACK = 1.0 + 2.0 ** -6
ATTN_MAX_SHIFT = 64.0
ATTN_BLOCK_LANES = 512
LOG2E = math.log2(math.e)
VMEM_LIMIT = 56 * 1024 * 1024


def _pick(n, candidates):
    for c in candidates:
        if n % c == 0:
            return c
    raise ValueError(f"no tile for {n} in {candidates}")


def _params(sem):
    return pltpu.CompilerParams(dimension_semantics=sem, vmem_limit_bytes=VMEM_LIMIT)


def _silu(x):
    hx = 0.5 * x
    return hx + hx * jnp.tanh(hx)


def _rms(x, g):
    ms = jnp.mean(x * x, axis=-1, keepdims=True)
    return x * lax.rsqrt(ms + EPS) * g


def _mod_row(ml_ref, mc_ref, idx, is_lat):
    return jnp.where(is_lat, ml_ref[0, idx:idx + 1, :], mc_ref[idx:idx + 1, :])


def _is_lat(t, tm, n_lat):
    rows = t * tm + lax.broadcasted_iota(jnp.int32, (tm, 1), 0)
    return rows < n_lat


def _ada_kernel(c_ref, w_ref, b_ref, o_ref):
    s = _silu(c_ref[...]).astype(BF16)
    o_ref[0] = jnp.dot(s, w_ref[0].astype(BF16), preferred_element_type=F32) + b_ref[0]


def _ada_mods(cvec, ada_w, ada_b):
    depth, d, n = ada_w.shape
    bp = cvec.shape[0]
    tn = _pick(n, (1536, 1024, 512, 256, 128))
    return pl.pallas_call(
        _ada_kernel,
        grid=(depth, n // tn),
        in_specs=[
            pl.BlockSpec((bp, d), lambda i, j: (0, 0)),
            pl.BlockSpec((1, d, tn), lambda i, j: (i, 0, j)),
            pl.BlockSpec((1, 1, tn), lambda i, j: (i, 0, j)),
        ],
        out_specs=pl.BlockSpec((1, bp, tn), lambda i, j: (i, 0, j)),
        out_shape=jax.ShapeDtypeStruct((depth, bp, n), F32),
        compiler_params=_params(("arbitrary", "arbitrary")),
        name="ada_mods",
    )(cvec, ada_w, ada_b.reshape(depth, 1, n))


def _in_proj_kernel(*refs, n_src, n_lat, emit_rows):
    srcs = refs[:n_src]
    ml_ref, mc_ref, g_ref, w_ref, ws_ref = refs[n_src:n_src + 5]
    o_ref, os_ref = refs[n_src + 5:n_src + 7]
    t = pl.program_id(1)
    tm = o_ref.shape[1]
    rc = _pick(tm, (128,))
    for r in range(0, tm, rc):
        lat = (t * tm + r + lax.broadcasted_iota(jnp.int32, (rc, 1), 0)) < n_lat
        shift = _mod_row(ml_ref, mc_ref, 0, lat)
        scale = _mod_row(ml_ref, mc_ref, 1, lat)
        xin = srcs[0][0, r:r + rc, :]
        if n_src == 2:
            xin = jnp.where(lat, xin, srcs[1][0, r:r + rc, :])
        if emit_rows:
            refs[n_src + 7][0, r:r + rc, :] = xin
        hb = (_rms(xin, g_ref[...]) * (1.0 + scale) + shift).astype(BF16)
        os_ref[0, r:r + rc, :] = jnp.dot(hb, ws_ref[...], preferred_element_type=F32)
        o_ref[0, r:r + rc, :] = jnp.dot(hb, w_ref[...], preferred_element_type=F32).astype(o_ref.dtype)


def _in_proj(srcs, mods_l, mods_c, g, w_main, w_side, n_lat, n_ctx):
    b, _, d = srcs[0].shape
    n = n_lat + n_ctx
    nm = w_main.shape[1]
    ns = w_side.shape[1]
    split = len(srcs) == 2
    tm = _pick(math.gcd(n_lat, n_ctx), (256, 128)) if split else _pick(n, (768, 384, 256, 128))
    lat_tiles = n_lat // tm
    kern = functools.partial(_in_proj_kernel, n_src=len(srcs), n_lat=n_lat, emit_rows=split)
    if split:
        src_specs = [pl.BlockSpec((1, tm, d), lambda i, t: (i, jnp.minimum(t, lat_tiles - 1), 0)),
                     pl.BlockSpec((1, tm, d), lambda i, t: (i, jnp.maximum(t - lat_tiles, 0), 0))]
    else:
        src_specs = [pl.BlockSpec((1, tm, d), lambda i, t: (i, t, 0))]
    resident = dict(pipeline_mode=pl.Buffered(1))
    out_specs = [pl.BlockSpec((1, tm, nm), lambda i, t: (i, t, 0)),
                 pl.BlockSpec((1, tm, ns), lambda i, t: (i, t, 0))]
    out_shape = [jax.ShapeDtypeStruct((b, n, nm), BF16), jax.ShapeDtypeStruct((b, n, ns), F32)]
    if split:
        out_specs.append(pl.BlockSpec((1, tm, d), lambda i, t: (i, t, 0)))
        out_shape.append(jax.ShapeDtypeStruct((b, n, d), F32))
    return pl.pallas_call(
        kern,
        grid=(b, n // tm),
        in_specs=src_specs + [
            pl.BlockSpec((1, ADA_CHUNKS, d), lambda i, t: (i, 0, 0)),
            pl.BlockSpec((ADA_CHUNKS, d), lambda i, t: (0, 0)),
            pl.BlockSpec((1, d), lambda i, t: (0, 0)),
            pl.BlockSpec((d, nm), lambda i, t: (0, 0), **resident),
            pl.BlockSpec((d, ns), lambda i, t: (0, 0), **resident),
        ],
        out_specs=out_specs,
        out_shape=out_shape,
        compiler_params=_params(("arbitrary", "arbitrary")),
        name="in_proj",
    )(*srcs, mods_l, mods_c, g, w_main, w_side)


def _cumsum_rows(v, reverse):
    n = v.shape[0]
    row = lax.broadcasted_iota(jnp.int32, v.shape, 0)
    k = 1
    while k < n:
        if reverse:
            v = v + jnp.where(row < n - k, pltpu.roll(v, n - k, axis=0), 0.0)
        else:
            v = v + jnp.where(row >= k, pltpu.roll(v, k, axis=0), 0.0)
        k *= 2
    return v


def _ssd_chunk_index(s, nl, nc, reverse):
    if reverse:
        return nc - 1 - s
    return lax.rem(s + nl, nc)


def _ssd_kernel(*refs, reverse, finish, nl, nc):
    st_ref = refs[-1]

    @pl.when(pl.program_id(1) == 0)
    def _():
        st_ref[...] = jnp.zeros_like(st_ref)

    for bb in range(refs[0].shape[0]):
        _ssd_chunk(bb, *refs, reverse=reverse, finish=finish, nl=nl, nc=nc)


def _ssd_conv_act(bb, xm_ref, xp_ref, xn_ref, cw_ref, cb_ref, chunk, nl, nc):
    q = SSD_CHUNK
    xmb = xm_ref[bb]
    halo = xp_ref.shape[1]
    seq_first = jnp.logical_or(chunk == 0, chunk == nl)
    seq_last = jnp.logical_or(chunk == nl - 1, chunk == nc - 1)
    prev_row = xp_ref[bb, halo - 1:halo, :].astype(F32) * jnp.where(seq_first, 0.0, 1.0)
    next_row = xn_ref[bb, 0:1, :].astype(F32) * jnp.where(seq_last, 0.0, 1.0)
    ii = lax.broadcasted_iota(jnp.int32, (q, 3 * q), 0)
    jj = lax.broadcasted_iota(jnp.int32, (q, 3 * q), 1)
    taps = jnp.where((jj + 1 == ii) | (jj == ii + q) | (jj == ii + 2 * q + 1), 1.0, 0.0).astype(BF16)
    hw = 0.5 * cw_ref[...]
    hwb = hw.astype(BF16)
    scaled = jnp.concatenate([xmb * hwb[0:1, :], xmb * hwb[1:2, :], xmb * hwb[2:3, :]], axis=0)
    hc = jnp.dot(taps, scaled, preferred_element_type=F32) + 0.5 * cb_ref[...]
    row8 = lax.broadcasted_iota(jnp.int32, (SUBLANES, 1), 0)
    hc = jnp.concatenate(
        [hc[:SUBLANES] + jnp.where(row8 == 0, prev_row * hw[0:1, :], 0.0), hc[SUBLANES:q - SUBLANES],
         hc[q - SUBLANES:] + jnp.where(row8 == SUBLANES - 1, next_row * hw[2:3, :], 0.0)], axis=0)
    return hc + hc * jnp.tanh(hc)


def _ssd_chunk(bb, *refs, reverse, finish, nl, nc):
    q = SSD_CHUNK
    p = SSD_HEADDIM
    hpg = SSD_HEADS // SSD_GROUPS
    gw = hpg * p
    chunk = _ssd_chunk_index(pl.program_id(1), nl, nc, reverse)
    if finish:
        act_ref, dt_ref, dtb_ref, alog_ref, z_ref, yf_ref, dsk_ref, ng_ref, o_ref, st_ref = refs
        actb = act_ref[bb]
        act = actb.astype(F32)
    else:
        xm_ref, xp_ref, xn_ref, dt_ref, cw_ref, cb_ref, dtb_ref, alog_ref, o_ref, act_ref, st_ref = refs
        act = _ssd_conv_act(bb, xm_ref, xp_ref, xn_ref, cw_ref, cb_ref, chunk, nl, nc)
        actb = act.astype(BF16)
        act_ref[bb] = actb
    ii = lax.broadcasted_iota(jnp.int32, (q, q), 0)
    jj = lax.broadcasted_iota(jnp.int32, (q, q), 1)
    xs = act[:, :SSD_INNER]
    xsb = actb[:, :SSD_INNER]
    bm = act[:, SSD_INNER:SSD_INNER + SSD_GN]
    cm = act[:, SSD_INNER + SSD_GN:]

    dtr = dt_ref[bb] + dtb_ref[...]
    dt = jnp.maximum(dtr, 0.0) + jnp.log(1.0 + jnp.exp(-jnp.abs(dtr)))
    a = -jnp.exp(alog_ref[...])
    acs = _cumsum_rows(dt * a, reverse) * LOG2E
    col_t = (acs - jnp.log2(dt)).T
    edge = acs[0:1, :] if reverse else acs[q - 1:q, :]
    dtd = dt * jnp.exp2(edge - acs)
    state_rows = finish
    dtd_t = dtd.T if state_rows else None
    cdec = jnp.exp2(edge)
    mask = (ii <= jj) if reverse else (ii >= jj)
    lane_h = lax.broadcasted_iota(jnp.int32, (q, LANES), 1)
    lane_res = lax.broadcasted_iota(jnp.int32, (q + SSD_STATE if state_rows else q, gw), 1)

    def expand(v, g):
        rows = v.shape[0]
        parts = []
        for k in range(0, hpg, LANES // p):
            h0 = g * hpg + k
            lo = jnp.broadcast_to(v[:, h0:h0 + 1], (rows, LANES))
            hi = jnp.broadcast_to(v[:, h0 + 1:h0 + 2], (rows, LANES))
            parts.append(jnp.where(lane_h[:rows] < p, lo, hi))
        return jnp.concatenate(parts, axis=1)

    ys = []
    for g in range(SSD_GROUPS):
        cg32 = cm[:, g * SSD_STATE:(g + 1) * SSD_STATE]
        bg32 = bm[:, g * SSD_STATE:(g + 1) * SSD_STATE]
        cb = lax.dot_general(cg32.astype(BF16), bg32.astype(BF16), (((1,), (1,)), ((), ())),
                             preferred_element_type=F32)
        st = st_ref[bb * SSD_GROUPS + g]
        bgt = bg32.T
        rhs = jnp.concatenate([xsb[:, g * gw:(g + 1) * gw], st.astype(BF16)], axis=0)
        res_g = None
        for k in reversed(range(hpg)):
            h = g * hpg + k
            a_col = jnp.broadcast_to(acs[:, h:h + 1], (q, q))
            seg = a_col - jnp.broadcast_to(col_t[h:h + 1, :], (q, q))
            m_h = (cb * jnp.exp2(jnp.where(mask, seg, -jnp.inf))).astype(BF16)
            c_h = (cg32 * jnp.exp2(a_col)).astype(BF16)
            lhs = jnp.concatenate([m_h, c_h], axis=1)
            if state_rows:
                w_h = (bgt * jnp.broadcast_to(dtd_t[h:h + 1, :], (SSD_STATE, q))).astype(BF16)
                lhs = jnp.concatenate(
                    [lhs, jnp.concatenate([w_h, jnp.zeros((SSD_STATE, SSD_STATE), BF16)], axis=1)], axis=0)
            res = jnp.dot(lhs, rhs, preferred_element_type=F32)
            res_g = res if res_g is None else jnp.where(lane_res < (k + 1) * p, res, res_g)
        ys.append(res_g[:q])
        if state_rows:
            new = res_g[q:]
        else:
            xdd = (xs[:, g * gw:(g + 1) * gw] * expand(dtd, g)).astype(BF16)
            new = jnp.dot(bgt.astype(BF16), xdd, preferred_element_type=F32)
        st_ref[bb * SSD_GROUPS + g] = st * expand(cdec, g) + new
    y = jnp.concatenate(ys, axis=1)

    if finish:
        y = yf_ref[bb] + y + xs * dsk_ref[...]
        y = y * _silu(z_ref[bb].astype(F32))
        o_ref[bb] = _rms(y, ng_ref[...]).astype(o_ref.dtype)
    else:
        o_ref[bb] = y


def _ssd_direction(p0, s0, dt_bias, a_log, n_lat, reverse, conv=None, extra=None):
    b, n, _ = p0.shape
    q = SSD_CHUNK
    nc = n // q
    nl = n_lat // q
    halo = BF16_ROWS
    hpc = q // halo
    nh = n // halo
    d = 1 if reverse else 0
    nb = _pick(b, (SSD_BATCH_PER_STEP, 1))
    cidx = functools.partial(_ssd_chunk_index, nl=nl, nc=nc, reverse=reverse)
    finish = extra is not None
    row_spec = lambda w, col=0: pl.BlockSpec((nb, q, w), lambda i, s: (i, cidx(s), col))
    vec_spec = lambda w: pl.BlockSpec((1, w), lambda i, s: (0, 0))
    if finish:
        act, yf, d_skip, norm_g = extra
        in_specs = [row_spec(SSD_CONV_DIM), row_spec(LANES, d), vec_spec(LANES), vec_spec(LANES),
                    row_spec(SSD_INNER, SSD_CONV_DIM // SSD_INNER), row_spec(SSD_INNER),
                    vec_spec(SSD_INNER), vec_spec(SSD_INNER)]
        args = [act, s0, dt_bias, a_log, p0, yf, d_skip, norm_g]
        out_specs = row_spec(SSD_INNER)
        out_shape = jax.ShapeDtypeStruct((b, n, SSD_INNER), BF16)
    else:
        conv_w, conv_b = conv
        in_specs = [
            row_spec(SSD_CONV_DIM),
            pl.BlockSpec((nb, halo, SSD_CONV_DIM), lambda i, s: (i, jnp.maximum(cidx(s) * hpc - 1, 0), 0)),
            pl.BlockSpec((nb, halo, SSD_CONV_DIM), lambda i, s: (i, jnp.minimum(cidx(s) * hpc + hpc, nh - 1), 0)),
            row_spec(LANES, d),
            pl.BlockSpec((3, SSD_CONV_DIM), lambda i, s: (0, 0)),
            vec_spec(SSD_CONV_DIM), vec_spec(LANES), vec_spec(LANES),
        ]
        args = [p0, p0, p0, s0, conv_w, conv_b, dt_bias, a_log]
        out_specs = [row_spec(SSD_INNER), row_spec(SSD_CONV_DIM)]
        out_shape = [jax.ShapeDtypeStruct((b, n, SSD_INNER), F32), jax.ShapeDtypeStruct((b, n, SSD_CONV_DIM), BF16)]
    kern = functools.partial(_ssd_kernel, reverse=reverse, finish=finish, nl=nl, nc=nc)
    return pl.pallas_call(
        kern,
        grid=(b // nb, nc),
        in_specs=in_specs,
        out_specs=out_specs,
        out_shape=out_shape,
        scratch_shapes=[pltpu.VMEM((nb * SSD_GROUPS, SSD_STATE, (SSD_HEADS // SSD_GROUPS) * SSD_HEADDIM), F32)],
        compiler_params=_params(("arbitrary", "arbitrary")),
        name="ssd_bwd_finish" if finish else "ssd_fwd",
    )(*args)


def _rope_tables(rows, rot_dim, n_ctx):
    n_freq = rot_dim // 4
    row = jnp.repeat(jnp.arange(rows, dtype=F32), GRID_W)
    col = jnp.tile(jnp.arange(GRID_W, dtype=F32), rows)
    inv = ROPE_THETA ** (-jnp.arange(n_freq, dtype=F32) / n_freq)
    ang = jnp.concatenate([row[:, None] * inv, col[:, None] * inv], axis=-1)
    cos, sin = jnp.cos(ang), jnp.sin(ang)
    reps = LANES // rot_dim
    cos_t = jnp.tile(jnp.concatenate([cos, cos], axis=-1), (1, reps))
    sin_t = jnp.tile(jnp.concatenate([-sin, sin], axis=-1), (1, reps))
    cos_t = jnp.concatenate([cos_t, jnp.ones((n_ctx, LANES), F32)], axis=0)
    sin_t = jnp.concatenate([sin_t, jnp.zeros((n_ctx, LANES), F32)], axis=0)
    return cos_t, sin_t


def _rope_block(xb, cos, sin, rot_dim):
    half = rot_dim // 2
    lane = lax.broadcasted_iota(jnp.int32, xb.shape, 1)
    first = jnp.bitwise_and(lane, rot_dim - 1) < half
    partner = jnp.where(first, pltpu.roll(xb, LANES - half, axis=1), pltpu.roll(xb, half, axis=1))
    return xb * cos + partner * sin


def _segment_mean_matrix(seg):
    sh = seg.bit_length() - 1
    i = jnp.right_shift(lax.broadcasted_iota(jnp.int32, (LANES, LANES), 0), sh)
    j = jnp.right_shift(lax.broadcasted_iota(jnp.int32, (LANES, LANES), 1), sh)
    return jnp.where(i == j, 1.0 / seg, 0.0).astype(BF16)


def _segment_mean_sq(xb, smat):
    x2 = xb * xb
    hi = x2.astype(BF16)
    lo = (x2 - hi.astype(F32)).astype(BF16)
    return (jnp.dot(hi, smat, preferred_element_type=F32) + jnp.dot(lo, smat, preferred_element_type=F32))


def _store_key_norms(ksq_ref, c, sumsq):
    mx = jnp.max(sumsq, axis=0, keepdims=True)
    ksq_ref[0, 0, :, (2 * c) * LANES:(2 * c + 1) * LANES] = mx
    ksq_ref[0, 0, :, (2 * c + 1) * LANES:(2 * c + 2) * LANES] = pltpu.roll(mx, LANES // 2, axis=1)


def _qk_prep_kernel(q_ref, k_ref, v_ref, cos_ref, sin_ref, qg_ref, kg_ref, qo_ref, ko_ref, vo_ref, ksq_ref, *, scale):
    smat = _segment_mean_matrix(ATT_HEADDIM)
    cos = cos_ref[...]
    sin = sin_ref[...]
    hpb = LANES // ATT_HEADDIM

    def prep(src_ref, g_ref, dst_ref, n_heads, mul, transposed):
        for c in range(n_heads // hpb):
            xb = src_ref[0, :, c * LANES:(c + 1) * LANES].astype(F32)
            xn = xb * lax.rsqrt(_segment_mean_sq(xb, smat) + EPS) * g_ref[...]
            r = _rope_block(xn, cos, sin, ATT_HEADDIM)
            if mul != 1.0:
                r = r * mul
            if transposed:
                r = r.T
            else:
                _store_key_norms(ksq_ref, c, _segment_mean_sq(r, smat) * float(ATT_HEADDIM))
            r = r.astype(dst_ref.dtype)
            for k in range(hpb):
                if transposed:
                    dst_ref[0, c * hpb + k] = r[k * ATT_HEADDIM:(k + 1) * ATT_HEADDIM, :]
                else:
                    dst_ref[0, c * hpb + k] = r[:, k * ATT_HEADDIM:(k + 1) * ATT_HEADDIM]

    prep(q_ref, qg_ref, qo_ref, ATT_HEADS, scale, True)
    prep(k_ref, kg_ref, ko_ref, ATT_KV_HEADS, 1.0, False)
    for c in range(ATT_KV_HEADS // hpb):
        vt = v_ref[0, :, c * LANES:(c + 1) * LANES].astype(F32).T.astype(vo_ref.dtype)
        for k in range(hpb):
            vo_ref[0, c * hpb + k] = vt[k * ATT_HEADDIM:(k + 1) * ATT_HEADDIM, :]


def _qk_prep(p0, cos_t, sin_t, q_g, k_g, q_col, k_col, v_col):
    b, n, _ = p0.shape
    tm = _pick(n, (768, 384, 256, 128))
    kern = functools.partial(_qk_prep_kernel, scale=ATT_HEADDIM ** -0.5 * LOG2E)
    return pl.pallas_call(
        kern,
        grid=(b, n // tm),
        in_specs=[
            pl.BlockSpec((1, tm, ATT_Q), lambda i, t: (i, t, q_col // ATT_Q)),
            pl.BlockSpec((1, tm, ATT_KV), lambda i, t: (i, t, k_col // ATT_KV)),
            pl.BlockSpec((1, tm, ATT_KV), lambda i, t: (i, t, v_col // ATT_KV)),
            pl.BlockSpec((tm, LANES), lambda i, t: (t, 0)),
            pl.BlockSpec((tm, LANES), lambda i, t: (t, 0)),
            pl.BlockSpec((1, LANES), lambda i, t: (0, 0)),
            pl.BlockSpec((1, LANES), lambda i, t: (0, 0)),
        ],
        out_specs=[
            pl.BlockSpec((1, ATT_HEADS, ATT_HEADDIM, tm), lambda i, t: (i, 0, 0, t)),
            pl.BlockSpec((1, ATT_KV_HEADS, tm, ATT_HEADDIM), lambda i, t: (i, 0, t, 0)),
            pl.BlockSpec((1, ATT_KV_HEADS, ATT_HEADDIM, tm), lambda i, t: (i, 0, 0, t)),
            pl.BlockSpec((1, 1, 1, ATT_KV_HEADS * LANES), lambda i, t: (i, t, 0, 0)),
        ],
        out_shape=[
            jax.ShapeDtypeStruct((b, ATT_HEADS, ATT_HEADDIM, n), BF16),
            jax.ShapeDtypeStruct((b, ATT_KV_HEADS, n, ATT_HEADDIM), BF16),
            jax.ShapeDtypeStruct((b, ATT_KV_HEADS, ATT_HEADDIM, n), BF16),
            jax.ShapeDtypeStruct((b, n // tm, 1, ATT_KV_HEADS * LANES), F32),
        ],
        compiler_params=_params(("arbitrary", "arbitrary")),
        name="gqa_qk_prep",
    )(p0, p0, p0, cos_t, sin_t, q_g, k_g)


def _attn_kernel(qt_ref, k_ref, vt_ref, ksq_ref, o_ref, s_scr, m_scr, e_scr, gap_scr, ot_scr, *,
                 group, hpb, n_lat, n_all, lat_tiles, ctx_self):
    t = pl.program_id(2)
    dv = vt_ref.shape[2]
    tq = qt_ref.shape[3]
    n_blocks = qt_ref.shape[1] // hpb

    ksq_max = jnp.max(ksq_ref[0], axis=0)

    def queries(i):
        return jnp.concatenate([qt_ref[0, i * hpb + j] for j in range(hpb)], axis=1)

    def finish(i, ot):
        ot = ot[:dv] * (1.0 / ot[dv:dv + 1])
        for j in range(hpb):
            ot_scr[(i * hpb + j) * dv:(i * hpb + j + 1) * dv, :] = ot[:, j * tq:(j + 1) * tq]

    def run(k_lo, k_len, exact):
        row = lax.broadcasted_iota(jnp.int32, (BF16_ROWS, k_len), 0)
        ones_row = jnp.where(row == 0, 1.0, 0.0).astype(BF16)

        def scores(i):
            kv = i * hpb // group
            qt = queries(i)
            s = jnp.dot(k_ref[0, kv, k_lo:k_lo + k_len, :], qt, preferred_element_type=F32)
            m = jnp.max(s, axis=0, keepdims=True)
            if exact:
                s_scr[i % 2, :k_len, :] = s
                m_scr[i % 2] = m
            else:
                qf = qt.astype(F32)
                ksq = ksq_max[:, kv * LANES:kv * LANES + 1]
                bound = jnp.sqrt(jnp.sum(qf * qf, axis=0, keepdims=True) * ksq) * ATTN_BOUND_SLACK
                e_scr[i % 2, :k_len, :] = jnp.exp2(s - bound).astype(BF16)
                gap_scr[i] = bound - m

        def combine(i):
            if exact:
                e = jnp.exp2(s_scr[i % 2, :k_len, :] - m_scr[i % 2]).astype(BF16)
            else:
                e = e_scr[i % 2, :k_len, :]
            vt = vt_ref[0, i * hpb // group, :, k_lo:k_lo + k_len]
            finish(i, jnp.dot(jnp.concatenate([vt, ones_row], axis=0), e, preferred_element_type=F32))

        scores(0)
        for i in range(n_blocks):
            if i + 1 < n_blocks:
                scores(i + 1)
            combine(i)

    def attend(k_lo, k_len):
        run(k_lo, k_len, exact=False)

        @pl.when(jnp.max(gap_scr[...]) > ATTN_MAX_SHIFT)
        def _():
            run(k_lo, k_len, exact=True)

        o_ref[0] = ot_scr[...].T.astype(o_ref.dtype)

    if ctx_self:
        @pl.when(t < lat_tiles)
        def _():
            attend(0, n_all)

        @pl.when(t >= lat_tiles)
        def _():
            attend(n_lat, n_all - n_lat)
    else:
        attend(0, n_all)


def _attention(qt, k, vt, ksq, n_lat, ctx_self):
    b, h, dqk, nq = qt.shape
    hkv, dv, n_all = vt.shape[1], vt.shape[2], vt.shape[3]
    group = h // hkv
    if ctx_self:
        tq = _pick(math.gcd(n_lat, n_all - n_lat), (256, 128))
    else:
        tq = _pick(n_lat, (ATTN_BLOCK_LANES, 256, 128))
    hpb = max(1, min(group, ATTN_BLOCK_LANES // tq))
    heads = min(h, ATTN_BLOCKS_PER_STEP * hpb)
    kvb = max(heads // group, 1)
    assert nq == (n_all if ctx_self else n_lat)
    kern = functools.partial(_attn_kernel, group=group, hpb=hpb, n_lat=n_lat, n_all=n_all,
                             lat_tiles=n_lat // tq, ctx_self=ctx_self)
    return pl.pallas_call(
        kern,
        grid=(b, h // heads, nq // tq),
        in_specs=[
            pl.BlockSpec((1, heads, dqk, tq), lambda i, hb, t: (i, hb, 0, t)),
            pl.BlockSpec((1, kvb, n_all, dqk), lambda i, hb, t: (i, hb, 0, 0)),
            pl.BlockSpec((1, kvb, dv, n_all), lambda i, hb, t: (i, hb, 0, 0)),
            pl.BlockSpec((1, ksq.shape[1], 1, kvb * LANES), lambda i, hb, t: (i, 0, 0, hb)),
        ],
        out_specs=pl.BlockSpec((1, tq, heads * dv), lambda i, hb, t: (i, t, hb)),
        out_shape=jax.ShapeDtypeStruct((b, nq, h * dv), BF16),
        scratch_shapes=[pltpu.VMEM((2, n_all, tq * hpb), F32), pltpu.VMEM((2, 1, tq * hpb), F32),
                        pltpu.VMEM((2, n_all, tq * hpb), BF16), pltpu.VMEM((heads // hpb, 1, tq * hpb), F32),
                        pltpu.VMEM((heads * dv, tq), F32)],
        compiler_params=_params(("arbitrary", "arbitrary", "arbitrary")),
        name="attention",
    )(qt, k, vt, ksq)


def _hidden_chunks(hidden):
    chunks, lo = [], 0
    while lo < hidden:
        w = min(1024, hidden - lo)
        chunks.append((lo, w))
        lo += w
    return chunks


def _mix_ffn_kernel(*refs, n_in, tm, n_lat, final):
    x_ref, ml_ref, mc_ref, g_ref = refs[:4]
    a_refs = refs[4:4 + n_in]
    w_refs = refs[4 + n_in:4 + 2 * n_in]
    wg_ref, wu_ref, wd_ref, fg_ref, o_ref = refs[4 + 2 * n_in:]
    t = pl.program_id(1)
    is_lat = _is_lat(t, tm, n_lat)
    mod = functools.partial(_mod_row, ml_ref, mc_ref, is_lat=is_lat)
    mix = jnp.dot(a_refs[0][0], w_refs[0][...], preferred_element_type=F32)
    for a_ref, w_ref in zip(a_refs[1:], w_refs[1:]):
        mix = mix + jnp.dot(a_ref[0], w_ref[...], preferred_element_type=F32)
    x = x_ref[0] + mod(2) * mix
    h = (_rms(x, g_ref[...]) * (1.0 + mod(4)) + mod(3)).astype(BF16)
    acc = None
    for lo, w in _hidden_chunks(wd_ref.shape[0]):
        gate = jnp.dot(h, wg_ref[:, lo:lo + w], preferred_element_type=F32)
        up = jnp.dot(h, wu_ref[:, lo:lo + w], preferred_element_type=F32)
        a = (_silu(gate) * up).astype(BF16)
        part = jnp.dot(a, wd_ref[lo:lo + w, :], preferred_element_type=F32)
        acc = part if acc is None else acc + part
    y = x + mod(5) * acc
    if final:
        y = _rms(y, fg_ref[...])
    o_ref[0] = y


def _mix_ffn(xa, mods_l, mods_c, g, acts, w_mix, w_gate_up, w_down, final_g, n_rows, n_lat, final):
    b, _, d = xa.shape
    hid = w_down.shape[0]
    tm = _pick(n_rows, (576, 512, 384, 256, 128))
    n_in = len(acts)
    k_act = acts[0].shape[2]
    assert all(a.shape[2] == k_act for a in acts) and w_mix.shape[0] == n_in * k_act
    kern = functools.partial(_mix_ffn_kernel, n_in=n_in, tm=tm, n_lat=n_lat, final=final)
    resident = dict(pipeline_mode=pl.Buffered(1))
    in_specs = [
        pl.BlockSpec((1, tm, d), lambda i, t: (i, t, 0)),
        pl.BlockSpec((1, ADA_CHUNKS, d), lambda i, t: (i, 0, 0)),
        pl.BlockSpec((ADA_CHUNKS, d), lambda i, t: (0, 0)),
        pl.BlockSpec((1, d), lambda i, t: (0, 0)),
    ]
    in_specs += [pl.BlockSpec((1, tm, a.shape[2]), lambda i, t: (i, t, 0)) for a in acts]
    in_specs += [pl.BlockSpec((k_act, d), lambda i, t, j=j: (j, 0), **resident) for j in range(n_in)]
    in_specs += [
        pl.BlockSpec((d, hid), lambda i, t: (0, 0), **resident),
        pl.BlockSpec((d, hid), lambda i, t: (0, 1), **resident),
        pl.BlockSpec((hid, d), lambda i, t: (0, 0), **resident),
        pl.BlockSpec((1, d), lambda i, t: (0, 0)),
    ]
    return pl.pallas_call(
        kern,
        grid=(b, n_rows // tm),
        in_specs=in_specs,
        out_specs=pl.BlockSpec((1, tm, d), lambda i, t: (i, t, 0)),
        out_shape=jax.ShapeDtypeStruct((b, n_rows, d), F32),
        compiler_params=_params(("arbitrary", "arbitrary")),
        name="mix_ffn_final" if final else "mix_ffn",
    )(xa, mods_l, mods_c, g, *acts, *([w_mix] * n_in), w_gate_up, w_gate_up, w_down, final_g)


def _mla_q_kernel(p_ref, g_ref, w_ref, cos_ref, sin_ref, o_ref, *, q_rank, scale):
    n_nope = MLA_HEADS * MLA_NOPE
    ppb = LANES // MLA_ROPE
    npb = LANES // MLA_NOPE
    gs = g_ref[...] * scale
    rc = _pick(p_ref.shape[1], (256, 128))
    for r in range(0, p_ref.shape[1], rc):
        cq = p_ref[0, r:r + rc, :q_rank].astype(F32)
        qn = _rms(cq, gs).astype(BF16)
        qq = jnp.dot(qn, w_ref[...], preferred_element_type=F32)
        cos, sin = cos_ref[r:r + rc, :], sin_ref[r:r + rc, :]
        pes_t = [_rope_block(qq[:, n_nope + c * LANES:n_nope + (c + 1) * LANES], cos, sin, MLA_ROPE).T
                 for c in range(MLA_HEADS // ppb)]
        for c in range(MLA_HEADS // npb):
            nope_t = qq[:, c * LANES:(c + 1) * LANES].T
            for k in range(npb):
                h = c * npb + k
                pe_t = pes_t[h // ppb][(h % ppb) * MLA_ROPE:(h % ppb + 1) * MLA_ROPE, :]
                qh = jnp.concatenate([nope_t[k * MLA_NOPE:(k + 1) * MLA_NOPE, :], pe_t], axis=0)
                o_ref[0, h, :, r:r + rc] = qh.astype(o_ref.dtype)


def _mla_q(p1, g, w_uq, cos_t, sin_t, n_lat, q_rank):
    b = p1.shape[0]
    cols = p1.shape[2]
    tm = _pick(n_lat, (512, 256, 128))
    kern = functools.partial(_mla_q_kernel, q_rank=q_rank, scale=MLA_QK ** -0.5 * LOG2E)
    return pl.pallas_call(
        kern,
        grid=(b, n_lat // tm),
        in_specs=[
            pl.BlockSpec((1, tm, cols), lambda i, t: (i, t, 0)),
            pl.BlockSpec((1, q_rank), lambda i, t: (0, 0)),
            pl.BlockSpec(w_uq.shape, lambda i, t: (0, 0)),
            pl.BlockSpec((tm, LANES), lambda i, t: (t, 0)),
            pl.BlockSpec((tm, LANES), lambda i, t: (t, 0)),
        ],
        out_specs=pl.BlockSpec((1, MLA_HEADS, MLA_QK, tm), lambda i, t: (i, 0, 0, t)),
        out_shape=jax.ShapeDtypeStruct((b, MLA_HEADS, MLA_QK, n_lat), BF16),
        compiler_params=_params(("arbitrary", "arbitrary")),
        name="mla_q",
    )(p1, g, w_uq, cos_t, sin_t)


def _mla_kv_kernel(p_ref, pe_ref, g_ref, w_ref, cos_ref, sin_ref, ko_ref, vo_ref, ksq_ref, *, q_rank):
    ckv = p_ref[0, :, q_rank:].astype(F32)
    kvn = _rms(ckv, g_ref[...]).astype(BF16)
    kv = jnp.dot(kvn, w_ref[...], preferred_element_type=F32)
    pe_block = _rope_block(pe_ref[0], cos_ref[...], sin_ref[...], MLA_ROPE)
    pe = pe_block[:, :MLA_ROPE]
    n_nope = MLA_HEADS * MLA_NOPE
    for h in range(MLA_HEADS):
        kh = jnp.concatenate([kv[:, h * MLA_NOPE:(h + 1) * MLA_NOPE], pe], axis=1)
        ko_ref[0, h] = kh.astype(ko_ref.dtype)
    pe_sq = jnp.sum(pe_block * pe_block, axis=1, keepdims=True)
    smat = _segment_mean_matrix(MLA_NOPE)
    for c in range(MLA_HEADS * MLA_NOPE // LANES):
        nope_sq = _segment_mean_sq(kv[:, c * LANES:(c + 1) * LANES], smat) * float(MLA_NOPE)
        _store_key_norms(ksq_ref, c, nope_sq + pe_sq)
    hpb = LANES // MLA_V
    for c in range(MLA_HEADS // hpb):
        v_t = kv[:, n_nope + c * LANES:n_nope + (c + 1) * LANES].T.astype(vo_ref.dtype)
        for k in range(hpb):
            vo_ref[0, c * hpb + k] = v_t[k * MLA_V:(k + 1) * MLA_V, :]


def _mla_kv(p1, s1, g, w_ukv, cos_t, sin_t, q_rank):
    b, n, cols = p1.shape
    kv_rank = cols - q_rank
    tm = _pick(n, (768, 384, 256, 128))
    kern = functools.partial(_mla_kv_kernel, q_rank=q_rank)
    return pl.pallas_call(
        kern,
        grid=(b, n // tm),
        in_specs=[
            pl.BlockSpec((1, tm, cols), lambda i, t: (i, t, 0)),
            pl.BlockSpec((1, tm, LANES), lambda i, t: (i, t, 0)),
            pl.BlockSpec((1, kv_rank), lambda i, t: (0, 0)),
            pl.BlockSpec(w_ukv.shape, lambda i, t: (0, 0)),
            pl.BlockSpec((tm, LANES), lambda i, t: (t, 0)),
            pl.BlockSpec((tm, LANES), lambda i, t: (t, 0)),
        ],
        out_specs=[
            pl.BlockSpec((1, MLA_HEADS, tm, MLA_QK), lambda i, t: (i, 0, t, 0)),
            pl.BlockSpec((1, MLA_HEADS, MLA_V, tm), lambda i, t: (i, 0, 0, t)),
            pl.BlockSpec((1, 1, 1, MLA_HEADS * LANES), lambda i, t: (i, t, 0, 0)),
        ],
        out_shape=[
            jax.ShapeDtypeStruct((b, MLA_HEADS, n, MLA_QK), BF16),
            jax.ShapeDtypeStruct((b, MLA_HEADS, MLA_V, n), BF16),
            jax.ShapeDtypeStruct((b, n // tm, 1, MLA_HEADS * LANES), F32),
        ],
        compiler_params=_params(("arbitrary", "arbitrary")),
        name="mla_kv",
    )(p1, s1, g, w_ukv, cos_t, sin_t)


def _pad_lanes(v, width=LANES):
    v = v.reshape(1, -1).astype(F32)
    return jnp.pad(v, ((0, 0), (0, width - v.shape[1])))


def kernel(x, c, ctx, c_ctx, ada_w, ada_b, norm1_g, norm2_g, ffn_w_up, ffn_w_down, ab_w_in, ab_w_out, ssd_conv_w, ssd_conv_b, ssd_a_log, ssd_dt_bias, ssd_d, ssd_norm_g, att_q_g, att_k_g, mla_w_in, mla_q_norm_g, mla_w_uq, mla_kv_norm_g, mla_w_ukv, mla_w_o, final_norm_g):
    b, t_lat, d = x.shape
    n_ctx = ctx.shape[1]
    rows = t_lat // GRID_W
    row2 = lambda v: v.reshape(1, -1).astype(F32)

    bp = -(-(b + 1) // 8) * 8
    cvec = jnp.concatenate([c, c_ctx[None, :], jnp.zeros((bp - b - 1, d), F32)], axis=0)
    mods = _ada_mods(cvec, ada_w, ada_b)
    mods_l = [mods[i, :b].reshape(b, ADA_CHUNKS, d) for i in range(2)]
    mods_c = [mods[i, b].reshape(ADA_CHUNKS, d) for i in range(2)]

    w_in = ab_w_in[0]
    o_xbc, o_dt = SSD_INNER, SSD_INNER + SSD_CONV_DIM
    o_att = o_dt + 2 * SSD_HEADS
    w_main = jnp.concatenate([w_in[:, o_xbc:o_dt], w_in[:, :SSD_INNER], w_in[:, o_att:]], axis=1).astype(BF16)
    pad_heads = lambda w: jnp.pad(w, ((0, 0), (0, LANES - SSD_HEADS)))
    w_dt = jnp.concatenate([pad_heads(w_in[:, o_dt:o_dt + SSD_HEADS]), pad_heads(w_in[:, o_dt + SSD_HEADS:o_att])],
                           axis=1).astype(BF16)
    p0, s0, xa = _in_proj([x, ctx], mods_l[0], mods_c[0], row2(norm1_g[0]), w_main, w_dt, t_lat, n_ctx)

    yf, act = _ssd_direction(p0, s0, _pad_lanes(ssd_dt_bias[0, 0]), _pad_lanes(ssd_a_log[0, 0]), t_lat,
                             reverse=False, conv=(ssd_conv_w[0].astype(F32), row2(ssd_conv_b[0])))
    d_skip = jnp.repeat(ssd_d[0].astype(F32), SSD_HEADDIM).reshape(1, SSD_INNER)
    y_ssd = _ssd_direction(p0, s0, _pad_lanes(ssd_dt_bias[0, 1]), _pad_lanes(ssd_a_log[0, 1]), t_lat,
                           reverse=True, extra=(act, yf, d_skip, row2(ssd_norm_g[0])))

    q_col = SSD_CONV_DIM + SSD_INNER
    k_col = q_col + ATT_Q
    cos_a, sin_a = _rope_tables(rows, ATT_HEADDIM, n_ctx)
    hpb = LANES // ATT_HEADDIM
    qh, kh, vh, ksq0 = _qk_prep(p0, cos_a, sin_a, jnp.tile(row2(att_q_g[0]), (1, hpb)),
                                jnp.tile(row2(att_k_g[0]), (1, hpb)), q_col, k_col, k_col + ATT_KV)
    o_att_l0 = _attention(qh, kh, vh, ksq0, t_lat, ctx_self=True)

    xa = _mix_ffn(xa, mods_l[0], mods_c[0], row2(norm2_g[0]), [y_ssd, o_att_l0], ab_w_out[0].astype(BF16),
                  ffn_w_up[0].astype(BF16), ffn_w_down[0].astype(BF16), row2(final_norm_g), t_lat + n_ctx, t_lat,
                  final=False)

    q_rank = mla_q_norm_g.shape[1]
    kv_rank = mla_kv_norm_g.shape[1]
    w_in1 = mla_w_in[0]
    w_pe = jnp.pad(w_in1[:, q_rank + kv_rank:], ((0, 0), (0, LANES - MLA_ROPE))).astype(BF16)
    p1, s1 = _in_proj([xa], mods_l[1], mods_c[1], row2(norm1_g[1]), w_in1[:, :q_rank + kv_rank].astype(BF16), w_pe,
                      t_lat, n_ctx)
    cos_m, sin_m = _rope_tables(rows, MLA_ROPE, n_ctx)
    w_uq = mla_w_uq[0].reshape(q_rank, MLA_HEADS, MLA_QK)
    w_uq = jnp.concatenate([w_uq[:, :, :MLA_NOPE].reshape(q_rank, -1), w_uq[:, :, MLA_NOPE:].reshape(q_rank, -1)],
                           axis=1).astype(BF16)
    w_ukv = mla_w_ukv[0].reshape(kv_rank, MLA_HEADS, MLA_NOPE + MLA_V)
    w_ukv = jnp.concatenate([w_ukv[:, :, :MLA_NOPE].reshape(kv_rank, -1), w_ukv[:, :, MLA_NOPE:].reshape(kv_rank, -1)],
                            axis=1).astype(BF16)
    qm = _mla_q(p1, row2(mla_q_norm_g[0]), w_uq, cos_m, sin_m, t_lat, q_rank)
    km, vm, ksq1 = _mla_kv(p1, s1, row2(mla_kv_norm_g[0]), w_ukv, cos_m, sin_m, q_rank)
    o_mla = _attention(qm, km, vm, ksq1, t_lat, ctx_self=False)

    return _mix_ffn(xa, mods_l[1], mods_c[1], row2(norm2_g[1]), [o_mla], mla_w_o[0].astype(BF16),
                    ffn_w_up[1].astype(BF16), ffn_w_down[1].astype(BF16), row2(final_norm_g), t_lat, t_lat,
                    final=True)
```

```python
import functools
import math

import jax
import jax.numpy as jnp
from jax import lax
from jax.experimental import pallas as pl
from jax.experimental.pallas import tpu as pltpu

F32 = jnp.float32
BF16 = jnp.bfloat16

EPS = 1e-6
ROPE_THETA = 10000.0
GRID_W = 64
ADA_CHUNKS = 6

SSD_HEADS = 16
SSD_HEADDIM = 64
SSD_GROUPS = 4
SSD_STATE = 128
SSD_CHUNK = 128
SSD_BATCH_PER_STEP = 8
SSD_INNER = SSD_HEADS * SSD_HEADDIM
SSD_GN = SSD_GROUPS * SSD_STATE
SSD_CONV_DIM = SSD_INNER + 2 * SSD_GN

ATT_HEADS = 16
ATT_KV_HEADS = 4
ATT_HEADDIM = 64
ATT_Q = ATT_HEADS * ATT_HEADDIM
ATT_KV = ATT_KV_HEADS * ATT_HEADDIM

MLA_HEADS = 16
MLA_NOPE = 64
MLA_ROPE = 32
MLA_V = 64
MLA_QK = MLA_NOPE + MLA_ROPE

LANES = 128
SUBLANES = 8
BF16_ROWS = 16
ATTN_BLOCKS_PER_STEP = 8
ATTN_BOUND_SLACK = 1.0 + 2.0 ** -6
ATTN_MAX_SHIFT = 64.0
ATTN_BLOCK_LANES = 512
LOG2E = math.log2(math.e)
VMEM_LIMIT = 56 * 1024 * 1024


def _pick(n, candidates):
    for c in candidates:
        if n % c == 0:
            return c
    raise ValueError(f"no tile for {n} in {candidates}")


def _params(sem):
    return pltpu.CompilerParams(dimension_semantics=sem, vmem_limit_bytes=VMEM_LIMIT)


def _silu(x):
    hx = 0.5 * x
    return hx + hx * jnp.tanh(hx)


def _rms(x, g):
    ms = jnp.mean(x * x, axis=-1, keepdims=True)
    return x * lax.rsqrt(ms + EPS) * g


def _mod_row(ml_ref, mc_ref, idx, is_lat):
    return jnp.where(is_lat, ml_ref[0, idx:idx + 1, :], mc_ref[idx:idx + 1, :])


def _is_lat(t, tm, n_lat):
    rows = t * tm + lax.broadcasted_iota(jnp.int32, (tm, 1), 0)
    return rows < n_lat


def _ada_kernel(c_ref, w_ref, b_ref, o_ref):
    s = _silu(c_ref[...]).astype(BF16)
    o_ref[0] = jnp.dot(s, w_ref[0].astype(BF16), preferred_element_type=F32) + b_ref[0]


def _ada_mods(cvec, ada_w, ada_b):
    depth, d, n = ada_w.shape
    bp = cvec.shape[0]
    tn = _pick(n, (1536, 1024, 512, 256, 128))
    return pl.pallas_call(
        _ada_kernel,
        grid=(depth, n // tn),
        in_specs=[
            pl.BlockSpec((bp, d), lambda i, j: (0, 0)),
            pl.BlockSpec((1, d, tn), lambda i, j: (i, 0, j)),
            pl.BlockSpec((1, 1, tn), lambda i, j: (i, 0, j)),
        ],
        out_specs=pl.BlockSpec((1, bp, tn), lambda i, j: (i, 0, j)),
        out_shape=jax.ShapeDtypeStruct((depth, bp, n), F32),
        compiler_params=_params(("arbitrary", "arbitrary")),
        name="ada_mods",
    )(cvec, ada_w, ada_b.reshape(depth, 1, n))


def _in_proj_kernel(*refs, n_src, n_lat, emit_rows):
    srcs = refs[:n_src]
    ml_ref, mc_ref, g_ref, w_ref, ws_ref = refs[n_src:n_src + 5]
    o_ref, os_ref = refs[n_src + 5:n_src + 7]
    t = pl.program_id(1)
    tm = o_ref.shape[1]
    rc = _pick(tm, (128,))
    for r in range(0, tm, rc):
        lat = (t * tm + r + lax.broadcasted_iota(jnp.int32, (rc, 1), 0)) < n_lat
        shift = _mod_row(ml_ref, mc_ref, 0, lat)
        scale = _mod_row(ml_ref, mc_ref, 1, lat)
        xin = srcs[0][0, r:r + rc, :]
        if n_src == 2:
            xin = jnp.where(lat, xin, srcs[1][0, r:r + rc, :])
        if emit_rows:
            refs[n_src + 7][0, r:r + rc, :] = xin
        hb = (_rms(xin, g_ref[...]) * (1.0 + scale) + shift).astype(BF16)
        os_ref[0, r:r + rc, :] = jnp.dot(hb, ws_ref[...], preferred_element_type=F32)
        o_ref[0, r:r + rc, :] = jnp.dot(hb, w_ref[...], preferred_element_type=F32).astype(o_ref.dtype)


def _in_proj(srcs, mods_l, mods_c, g, w_main, w_side, n_lat, n_ctx):
    b, _, d = srcs[0].shape
    n = n_lat + n_ctx
    nm = w_main.shape[1]
    ns = w_side.shape[1]
    split = len(srcs) == 2
    tm = _pick(math.gcd(n_lat, n_ctx), (256, 128)) if split else _pick(n, (768, 384, 256, 128))
    lat_tiles = n_lat // tm
    kern = functools.partial(_in_proj_kernel, n_src=len(srcs), n_lat=n_lat, emit_rows=split)
    if split:
        src_specs = [pl.BlockSpec((1, tm, d), lambda i, t: (i, jnp.minimum(t, lat_tiles - 1), 0)),
                     pl.BlockSpec((1, tm, d), lambda i, t: (i, jnp.maximum(t - lat_tiles, 0), 0))]
    else:
        src_specs = [pl.BlockSpec((1, tm, d), lambda i, t: (i, t, 0))]
    resident = dict(pipeline_mode=pl.Buffered(1))
    out_specs = [pl.BlockSpec((1, tm, nm), lambda i, t: (i, t, 0)),
                 pl.BlockSpec((1, tm, ns), lambda i, t: (i, t, 0))]
    out_shape = [jax.ShapeDtypeStruct((b, n, nm), BF16), jax.ShapeDtypeStruct((b, n, ns), F32)]
    if split:
        out_specs.append(pl.BlockSpec((1, tm, d), lambda i, t: (i, t, 0)))
        out_shape.append(jax.ShapeDtypeStruct((b, n, d), F32))
    return pl.pallas_call(
        kern,
        grid=(b, n // tm),
        in_specs=src_specs + [
            pl.BlockSpec((1, ADA_CHUNKS, d), lambda i, t: (i, 0, 0)),
            pl.BlockSpec((ADA_CHUNKS, d), lambda i, t: (0, 0)),
            pl.BlockSpec((1, d), lambda i, t: (0, 0)),
            pl.BlockSpec((d, nm), lambda i, t: (0, 0), **resident),
            pl.BlockSpec((d, ns), lambda i, t: (0, 0), **resident),
        ],
        out_specs=out_specs,
        out_shape=out_shape,
        compiler_params=_params(("arbitrary", "arbitrary")),
        name="in_proj",
    )(*srcs, mods_l, mods_c, g, w_main, w_side)


def _cumsum_rows(v, reverse):
    n = v.shape[0]
    row = lax.broadcasted_iota(jnp.int32, v.shape, 0)
    k = 1
    while k < n:
        if reverse:
            v = v + jnp.where(row < n - k, pltpu.roll(v, n - k, axis=0), 0.0)
        else:
            v = v + jnp.where(row >= k, pltpu.roll(v, k, axis=0), 0.0)
        k *= 2
    return v


def _ssd_chunk_index(s, nl, nc, reverse):
    if reverse:
        return nc - 1 - s
    return lax.rem(s + nl, nc)


def _ssd_kernel(*refs, reverse, finish, nl, nc):
    st_ref = refs[-1]

    @pl.when(pl.program_id(1) == 0)
    def _():
        st_ref[...] = jnp.zeros_like(st_ref)

    for bb in range(refs[0].shape[0]):
        _ssd_chunk(bb, *refs, reverse=reverse, finish=finish, nl=nl, nc=nc)


def _ssd_conv_act(bb, xm_ref, xp_ref, xn_ref, cw_ref, cb_ref, chunk, nl, nc):
    q = SSD_CHUNK
    xmb = xm_ref[bb]
    halo = xp_ref.shape[1]
    seq_first = jnp.logical_or(chunk == 0, chunk == nl)
    seq_last = jnp.logical_or(chunk == nl - 1, chunk == nc - 1)
    prev_row = xp_ref[bb, halo - 1:halo, :].astype(F32) * jnp.where(seq_first, 0.0, 1.0)
    next_row = xn_ref[bb, 0:1, :].astype(F32) * jnp.where(seq_last, 0.0, 1.0)
    ii = lax.broadcasted_iota(jnp.int32, (q, 3 * q), 0)
    jj = lax.broadcasted_iota(jnp.int32, (q, 3 * q), 1)
    taps = jnp.where((jj + 1 == ii) | (jj == ii + q) | (jj == ii + 2 * q + 1), 1.0, 0.0).astype(BF16)
    hw = 0.5 * cw_ref[...]
    hwb = hw.astype(BF16)
    scaled = jnp.concatenate([xmb * hwb[0:1, :], xmb * hwb[1:2, :], xmb * hwb[2:3, :]], axis=0)
    hc = jnp.dot(taps, scaled, preferred_element_type=F32) + 0.5 * cb_ref[...]
    row8 = lax.broadcasted_iota(jnp.int32, (SUBLANES, 1), 0)
    hc = jnp.concatenate(
        [hc[:SUBLANES] + jnp.where(row8 == 0, prev_row * hw[0:1, :], 0.0), hc[SUBLANES:q - SUBLANES],
         hc[q - SUBLANES:] + jnp.where(row8 == SUBLANES - 1, next_row * hw[2:3, :], 0.0)], axis=0)
    return hc + hc * jnp.tanh(hc)


def _ssd_chunk(bb, *refs, reverse, finish, nl, nc):
    q = SSD_CHUNK
    p = SSD_HEADDIM
    hpg = SSD_HEADS // SSD_GROUPS
    gw = hpg * p
    chunk = _ssd_chunk_index(pl.program_id(1), nl, nc, reverse)
    if finish:
        act_ref, dt_ref, dtb_ref, alog_ref, z_ref, yf_ref, dsk_ref, ng_ref, o_ref, st_ref = refs
        actb = act_ref[bb]
        act = actb.astype(F32)
    else:
        xm_ref, xp_ref, xn_ref, dt_ref, cw_ref, cb_ref, dtb_ref, alog_ref, o_ref, act_ref, st_ref = refs
        act = _ssd_conv_act(bb, xm_ref, xp_ref, xn_ref, cw_ref, cb_ref, chunk, nl, nc)
        actb = act.astype(BF16)
        act_ref[bb] = actb
    ii = lax.broadcasted_iota(jnp.int32, (q, q), 0)
    jj = lax.broadcasted_iota(jnp.int32, (q, q), 1)
    xs = act[:, :SSD_INNER]
    xsb = actb[:, :SSD_INNER]
    bm = act[:, SSD_INNER:SSD_INNER + SSD_GN]
    cm = act[:, SSD_INNER + SSD_GN:]

    dtr = dt_ref[bb] + dtb_ref[...]
    dt = jnp.maximum(dtr, 0.0) + jnp.log(1.0 + jnp.exp(-jnp.abs(dtr)))
    a = -jnp.exp(alog_ref[...])
    acs = _cumsum_rows(dt * a, reverse) * LOG2E
    col_t = (acs - jnp.log2(dt)).T
    edge = acs[0:1, :] if reverse else acs[q - 1:q, :]
    dtd = dt * jnp.exp2(edge - acs)
    state_rows = finish
    dtd_t = dtd.T if state_rows else None
    cdec = jnp.exp2(edge)
    mask = (ii <= jj) if reverse else (ii >= jj)
    lane_h = lax.broadcasted_iota(jnp.int32, (q, LANES), 1)
    lane_res = lax.broadcasted_iota(jnp.int32, (q + SSD_STATE if state_rows else q, gw), 1)

    def expand(v, g):
        rows = v.shape[0]
        parts = []
        for k in range(0, hpg, LANES // p):
            h0 = g * hpg + k
            lo = jnp.broadcast_to(v[:, h0:h0 + 1], (rows, LANES))
            hi = jnp.broadcast_to(v[:, h0 + 1:h0 + 2], (rows, LANES))
            parts.append(jnp.where(lane_h[:rows] < p, lo, hi))
        return jnp.concatenate(parts, axis=1)

    ys = []
    for g in range(SSD_GROUPS):
        cg32 = cm[:, g * SSD_STATE:(g + 1) * SSD_STATE]
        bg32 = bm[:, g * SSD_STATE:(g + 1) * SSD_STATE]
        cb = lax.dot_general(cg32.astype(BF16), bg32.astype(BF16), (((1,), (1,)), ((), ())),
                             preferred_element_type=F32)
        st = st_ref[bb * SSD_GROUPS + g]
        bgt = bg32.T
        rhs = jnp.concatenate([xsb[:, g * gw:(g + 1) * gw], st.astype(BF16)], axis=0)
        res_g = None
        for k in reversed(range(hpg)):
            h = g * hpg + k
            a_col = jnp.broadcast_to(acs[:, h:h + 1], (q, q))
            seg = a_col - jnp.broadcast_to(col_t[h:h + 1, :], (q, q))
            m_h = (cb * jnp.exp2(jnp.where(mask, seg, -jnp.inf))).astype(BF16)
            c_h = (cg32 * jnp.exp2(a_col)).astype(BF16)
            lhs = jnp.concatenate([m_h, c_h], axis=1)
            if state_rows:
                w_h = (bgt * jnp.broadcast_to(dtd_t[h:h + 1, :], (SSD_STATE, q))).astype(BF16)
                lhs = jnp.concatenate(
                    [lhs, jnp.concatenate([w_h, jnp.zeros((SSD_STATE, SSD_STATE), BF16)], axis=1)], axis=0)
            res = jnp.dot(lhs, rhs, preferred_element_type=F32)
            res_g = res if res_g is None else jnp.where(lane_res < (k + 1) * p, res, res_g)
        ys.append(res_g[:q])
        if state_rows:
            new = res_g[q:]
        else:
            xdd = (xs[:, g * gw:(g + 1) * gw] * expand(dtd, g)).astype(BF16)
            new = jnp.dot(bgt.astype(BF16), xdd, preferred_element_type=F32)
        st_ref[bb * SSD_GROUPS + g] = st * expand(cdec, g) + new
    y = jnp.concatenate(ys, axis=1)

    if finish:
        y = yf_ref[bb] + y + xs * dsk_ref[...]
        y = y * _silu(z_ref[bb].astype(F32))
        o_ref[bb] = _rms(y, ng_ref[...]).astype(o_ref.dtype)
    else:
        o_ref[bb] = y


def _ssd_direction(p0, s0, dt_bias, a_log, n_lat, reverse, conv=None, extra=None):
    b, n, _ = p0.shape
    q = SSD_CHUNK
    nc = n // q
    nl = n_lat // q
    halo = BF16_ROWS
    hpc = q // halo
    nh = n // halo
    d = 1 if reverse else 0
    nb = _pick(b, (SSD_BATCH_PER_STEP, 1))
    cidx = functools.partial(_ssd_chunk_index, nl=nl, nc=nc, reverse=reverse)
    finish = extra is not None
    row_spec = lambda w, col=0: pl.BlockSpec((nb, q, w), lambda i, s: (i, cidx(s), col))
    vec_spec = lambda w: pl.BlockSpec((1, w), lambda i, s: (0, 0))
    if finish:
        act, yf, d_skip, norm_g = extra
        in_specs = [row_spec(SSD_CONV_DIM), row_spec(LANES, d), vec_spec(LANES), vec_spec(LANES),
                    row_spec(SSD_INNER, SSD_CONV_DIM // SSD_INNER), row_spec(SSD_INNER),
                    vec_spec(SSD_INNER), vec_spec(SSD_INNER)]
        args = [act, s0, dt_bias, a_log, p0, yf, d_skip, norm_g]
        out_specs = row_spec(SSD_INNER)
        out_shape = jax.ShapeDtypeStruct((b, n, SSD_INNER), BF16)
    else:
        conv_w, conv_b = conv
        in_specs = [
            row_spec(SSD_CONV_DIM),
            pl.BlockSpec((nb, halo, SSD_CONV_DIM), lambda i, s: (i, jnp.maximum(cidx(s) * hpc - 1, 0), 0)),
            pl.BlockSpec((nb, halo, SSD_CONV_DIM), lambda i, s: (i, jnp.minimum(cidx(s) * hpc + hpc, nh - 1), 0)),
            row_spec(LANES, d),
            pl.BlockSpec((3, SSD_CONV_DIM), lambda i, s: (0, 0)),
            vec_spec(SSD_CONV_DIM), vec_spec(LANES), vec_spec(LANES),
        ]
        args = [p0, p0, p0, s0, conv_w, conv_b, dt_bias, a_log]
        out_specs = [row_spec(SSD_INNER), row_spec(SSD_CONV_DIM)]
        out_shape = [jax.ShapeDtypeStruct((b, n, SSD_INNER), F32), jax.ShapeDtypeStruct((b, n, SSD_CONV_DIM), BF16)]
    kern = functools.partial(_ssd_kernel, reverse=reverse, finish=finish, nl=nl, nc=nc)
    return pl.pallas_call(
        kern,
        grid=(b // nb, nc),
        in_specs=in_specs,
        out_specs=out_specs,
        out_shape=out_shape,
        scratch_shapes=[pltpu.VMEM((nb * SSD_GROUPS, SSD_STATE, (SSD_HEADS // SSD_GROUPS) * SSD_HEADDIM), F32)],
        compiler_params=_params(("arbitrary", "arbitrary")),
        name="ssd_bwd_finish" if finish else "ssd_fwd",
    )(*args)


def _rope_tables(rows, rot_dim, n_ctx):
    n_freq = rot_dim // 4
    row = jnp.repeat(jnp.arange(rows, dtype=F32), GRID_W)
    col = jnp.tile(jnp.arange(GRID_W, dtype=F32), rows)
    inv = ROPE_THETA ** (-jnp.arange(n_freq, dtype=F32) / n_freq)
    ang = jnp.concatenate([row[:, None] * inv, col[:, None] * inv], axis=-1)
    cos, sin = jnp.cos(ang), jnp.sin(ang)
    reps = LANES // rot_dim
    cos_t = jnp.tile(jnp.concatenate([cos, cos], axis=-1), (1, reps))
    sin_t = jnp.tile(jnp.concatenate([-sin, sin], axis=-1), (1, reps))
    cos_t = jnp.concatenate([cos_t, jnp.ones((n_ctx, LANES), F32)], axis=0)
    sin_t = jnp.concatenate([sin_t, jnp.zeros((n_ctx, LANES), F32)], axis=0)
    return cos_t, sin_t


def _rope_block(xb, cos, sin, rot_dim):
    half = rot_dim // 2
    lane = lax.broadcasted_iota(jnp.int32, xb.shape, 1)
    first = jnp.bitwise_and(lane, rot_dim - 1) < half
    partner = jnp.where(first, pltpu.roll(xb, LANES - half, axis=1), pltpu.roll(xb, half, axis=1))
    return xb * cos + partner * sin


def _segment_mean_matrix(seg):
    sh = seg.bit_length() - 1
    i = jnp.right_shift(lax.broadcasted_iota(jnp.int32, (LANES, LANES), 0), sh)
    j = jnp.right_shift(lax.broadcasted_iota(jnp.int32, (LANES, LANES), 1), sh)
    return jnp.where(i == j, 1.0 / seg, 0.0).astype(BF16)


def _segment_mean_sq(xb, smat):
    x2 = xb * xb
    hi = x2.astype(BF16)
    lo = (x2 - hi.astype(F32)).astype(BF16)
    return (jnp.dot(hi, smat, preferred_element_type=F32) + jnp.dot(lo, smat, preferred_element_type=F32))


def _store_key_norms(ksq_ref, c, sumsq):
    mx = jnp.max(sumsq, axis=0, keepdims=True)
    ksq_ref[0, 0, :, (2 * c) * LANES:(2 * c + 1) * LANES] = mx
    ksq_ref[0, 0, :, (2 * c + 1) * LANES:(2 * c + 2) * LANES] = pltpu.roll(mx, LANES // 2, axis=1)


def _qk_prep_kernel(q_ref, k_ref, v_ref, cos_ref, sin_ref, qg_ref, kg_ref, qo_ref, ko_ref, vo_ref, ksq_ref, *, scale):
    smat = _segment_mean_matrix(ATT_HEADDIM)
    cos = cos_ref[...]
    sin = sin_ref[...]
    hpb = LANES // ATT_HEADDIM

    def prep(src_ref, g_ref, dst_ref, n_heads, mul, transposed):
        for c in range(n_heads // hpb):
            xb = src_ref[0, :, c * LANES:(c + 1) * LANES].astype(F32)
            xn = xb * lax.rsqrt(_segment_mean_sq(xb, smat) + EPS) * g_ref[...]
            r = _rope_block(xn, cos, sin, ATT_HEADDIM)
            if mul != 1.0:
                r = r * mul
            if transposed:
                r = r.T
            else:
                _store_key_norms(ksq_ref, c, _segment_mean_sq(r, smat) * float(ATT_HEADDIM))
            r = r.astype(dst_ref.dtype)
            for k in range(hpb):
                if transposed:
                    dst_ref[0, c * hpb + k] = r[k * ATT_HEADDIM:(k + 1) * ATT_HEADDIM, :]
                else:
                    dst_ref[0, c * hpb + k] = r[:, k * ATT_HEADDIM:(k + 1) * ATT_HEADDIM]

    prep(q_ref, qg_ref, qo_ref, ATT_HEADS, scale, True)
    prep(k_ref, kg_ref, ko_ref, ATT_KV_HEADS, 1.0, False)
    for c in range(ATT_KV_HEADS // hpb):
        vt = v_ref[0, :, c * LANES:(c + 1) * LANES].astype(F32).T.astype(vo_ref.dtype)
        for k in range(hpb):
            vo_ref[0, c * hpb + k] = vt[k * ATT_HEADDIM:(k + 1) * ATT_HEADDIM, :]


def _qk_prep(p0, cos_t, sin_t, q_g, k_g, q_col, k_col, v_col):
    b, n, _ = p0.shape
    tm = _pick(n, (768, 384, 256, 128))
    kern = functools.partial(_qk_prep_kernel, scale=ATT_HEADDIM ** -0.5 * LOG2E)
    return pl.pallas_call(
        kern,
        grid=(b, n // tm),
        in_specs=[
            pl.BlockSpec((1, tm, ATT_Q), lambda i, t: (i, t, q_col // ATT_Q)),
            pl.BlockSpec((1, tm, ATT_KV), lambda i, t: (i, t, k_col // ATT_KV)),
            pl.BlockSpec((1, tm, ATT_KV), lambda i, t: (i, t, v_col // ATT_KV)),
            pl.BlockSpec((tm, LANES), lambda i, t: (t, 0)),
            pl.BlockSpec((tm, LANES), lambda i, t: (t, 0)),
            pl.BlockSpec((1, LANES), lambda i, t: (0, 0)),
            pl.BlockSpec((1, LANES), lambda i, t: (0, 0)),
        ],
        out_specs=[
            pl.BlockSpec((1, ATT_HEADS, ATT_HEADDIM, tm), lambda i, t: (i, 0, 0, t)),
            pl.BlockSpec((1, ATT_KV_HEADS, tm, ATT_HEADDIM), lambda i, t: (i, 0, t, 0)),
            pl.BlockSpec((1, ATT_KV_HEADS, ATT_HEADDIM, tm), lambda i, t: (i, 0, 0, t)),
            pl.BlockSpec((1, 1, 1, ATT_KV_HEADS * LANES), lambda i, t: (i, t, 0, 0)),
        ],
        out_shape=[
            jax.ShapeDtypeStruct((b, ATT_HEADS, ATT_HEADDIM, n), BF16),
            jax.ShapeDtypeStruct((b, ATT_KV_HEADS, n, ATT_HEADDIM), BF16),
            jax.ShapeDtypeStruct((b, ATT_KV_HEADS, ATT_HEADDIM, n), BF16),
            jax.ShapeDtypeStruct((b, n // tm, 1, ATT_KV_HEADS * LANES), F32),
        ],
        compiler_params=_params(("arbitrary", "arbitrary")),
        name="gqa_qk_prep",
    )(p0, p0, p0, cos_t, sin_t, q_g, k_g)


def _attn_kernel(qt_ref, k_ref, vt_ref, ksq_ref, o_ref, s_scr, m_scr, e_scr, gap_scr, ot_scr, *,
                 group, hpb, n_lat, n_all, lat_tiles, ctx_self):
    t = pl.program_id(2)
    dv = vt_ref.shape[2]
    tq = qt_ref.shape[3]
    n_blocks = qt_ref.shape[1] // hpb

    ksq_max = jnp.max(ksq_ref[0], axis=0)

    def queries(i):
        return jnp.concatenate([qt_ref[0, i * hpb + j] for j in range(hpb)], axis=1)

    def finish(i, ot):
        ot = ot[:dv] * (1.0 / ot[dv:dv + 1])
        for j in range(hpb):
            ot_scr[(i * hpb + j) * dv:(i * hpb + j + 1) * dv, :] = ot[:, j * tq:(j + 1) * tq]

    def run(k_lo, k_len, exact):
        row = lax.broadcasted_iota(jnp.int32, (BF16_ROWS, k_len), 0)
        ones_row = jnp.where(row == 0, 1.0, 0.0).astype(BF16)

        def scores(i):
            kv = i * hpb // group
            qt = queries(i)
            s = jnp.dot(k_ref[0, kv, k_lo:k_lo + k_len, :], qt, preferred_element_type=F32)
            m = jnp.max(s, axis=0, keepdims=True)
            if exact:
                s_scr[i % 2, :k_len, :] = s
                m_scr[i % 2] = m
            else:
                qf = qt.astype(F32)
                ksq = ksq_max[:, kv * LANES:kv * LANES + 1]
                bound = jnp.sqrt(jnp.sum(qf * qf, axis=0, keepdims=True) * ksq) * ATTN_BOUND_SLACK
                e_scr[i % 2, :k_len, :] = jnp.exp2(s - bound).astype(BF16)
                gap_scr[i] = bound - m

        def combine(i):
            if exact:
                e = jnp.exp2(s_scr[i % 2, :k_len, :] - m_scr[i % 2]).astype(BF16)
            else:
                e = e_scr[i % 2, :k_len, :]
            vt = vt_ref[0, i * hpb // group, :, k_lo:k_lo + k_len]
            finish(i, jnp.dot(jnp.concatenate([vt, ones_row], axis=0), e, preferred_element_type=F32))

        scores(0)
        for i in range(n_blocks):
            if i + 1 < n_blocks:
                scores(i + 1)
            combine(i)

    def attend(k_lo, k_len):
        run(k_lo, k_len, exact=False)

        @pl.when(jnp.max(gap_scr[...]) > ATTN_MAX_SHIFT)
        def _():
            run(k_lo, k_len, exact=True)

        o_ref[0] = ot_scr[...].T.astype(o_ref.dtype)

    if ctx_self:
        @pl.when(t < lat_tiles)
        def _():
            attend(0, n_all)

        @pl.when(t >= lat_tiles)
        def _():
            attend(n_lat, n_all - n_lat)
    else:
        attend(0, n_all)


def _attention(qt, k, vt, ksq, n_lat, ctx_self):
    b, h, dqk, nq = qt.shape
    hkv, dv, n_all = vt.shape[1], vt.shape[2], vt.shape[3]
    group = h // hkv
    if ctx_self:
        tq = _pick(math.gcd(n_lat, n_all - n_lat), (256, 128))
    else:
        tq = _pick(n_lat, (ATTN_BLOCK_LANES, 256, 128))
    hpb = max(1, min(group, ATTN_BLOCK_LANES // tq))
    heads = min(h, ATTN_BLOCKS_PER_STEP * hpb)
    kvb = max(heads // group, 1)
    assert nq == (n_all if ctx_self else n_lat)
    kern = functools.partial(_attn_kernel, group=group, hpb=hpb, n_lat=n_lat, n_all=n_all,
                             lat_tiles=n_lat // tq, ctx_self=ctx_self)
    return pl.pallas_call(
        kern,
        grid=(b, h // heads, nq // tq),
        in_specs=[
            pl.BlockSpec((1, heads, dqk, tq), lambda i, hb, t: (i, hb, 0, t)),
            pl.BlockSpec((1, kvb, n_all, dqk), lambda i, hb, t: (i, hb, 0, 0)),
            pl.BlockSpec((1, kvb, dv, n_all), lambda i, hb, t: (i, hb, 0, 0)),
            pl.BlockSpec((1, ksq.shape[1], 1, kvb * LANES), lambda i, hb, t: (i, 0, 0, hb)),
        ],
        out_specs=pl.BlockSpec((1, tq, heads * dv), lambda i, hb, t: (i, t, hb)),
        out_shape=jax.ShapeDtypeStruct((b, nq, h * dv), BF16),
        scratch_shapes=[pltpu.VMEM((2, n_all, tq * hpb), F32), pltpu.VMEM((2, 1, tq * hpb), F32),
                        pltpu.VMEM((2, n_all, tq * hpb), BF16), pltpu.VMEM((heads // hpb, 1, tq * hpb), F32),
                        pltpu.VMEM((heads * dv, tq), F32)],
        compiler_params=_params(("arbitrary", "arbitrary", "arbitrary")),
        name="attention",
    )(qt, k, vt, ksq)


def _hidden_chunks(hidden):
    chunks, lo = [], 0
    while lo < hidden:
        w = min(1024, hidden - lo)
        chunks.append((lo, w))
        lo += w
    return chunks


def _mix_ffn_kernel(*refs, n_in, tm, n_lat, final):
    x_ref, ml_ref, mc_ref, g_ref = refs[:4]
    a_refs = refs[4:4 + n_in]
    w_refs = refs[4 + n_in:4 + 2 * n_in]
    wg_ref, wu_ref, wd_ref, fg_ref, o_ref = refs[4 + 2 * n_in:]
    t = pl.program_id(1)
    is_lat = _is_lat(t, tm, n_lat)
    mod = functools.partial(_mod_row, ml_ref, mc_ref, is_lat=is_lat)
    mix = jnp.dot(a_refs[0][0], w_refs[0][...], preferred_element_type=F32)
    for a_ref, w_ref in zip(a_refs[1:], w_refs[1:]):
        mix = mix + jnp.dot(a_ref[0], w_ref[...], preferred_element_type=F32)
    x = x_ref[0] + mod(2) * mix
    h = (_rms(x, g_ref[...]) * (1.0 + mod(4)) + mod(3)).astype(BF16)
    acc = None
    for lo, w in _hidden_chunks(wd_ref.shape[0]):
        gate = jnp.dot(h, wg_ref[:, lo:lo + w], preferred_element_type=F32)
        up = jnp.dot(h, wu_ref[:, lo:lo + w], preferred_element_type=F32)
        a = (_silu(gate) * up).astype(BF16)
        part = jnp.dot(a, wd_ref[lo:lo + w, :], preferred_element_type=F32)
        acc = part if acc is None else acc + part
    y = x + mod(5) * acc
    if final:
        y = _rms(y, fg_ref[...])
    o_ref[0] = y


def _mix_ffn(xa, mods_l, mods_c, g, acts, w_mix, w_gate_up, w_down, final_g, n_rows, n_lat, final):
    b, _, d = xa.shape
    hid = w_down.shape[0]
    tm = _pick(n_rows, (576, 512, 384, 256, 128))
    n_in = len(acts)
    k_act = acts[0].shape[2]
    assert all(a.shape[2] == k_act for a in acts) and w_mix.shape[0] == n_in * k_act
    kern = functools.partial(_mix_ffn_kernel, n_in=n_in, tm=tm, n_lat=n_lat, final=final)
    resident = dict(pipeline_mode=pl.Buffered(1))
    in_specs = [
        pl.BlockSpec((1, tm, d), lambda i, t: (i, t, 0)),
        pl.BlockSpec((1, ADA_CHUNKS, d), lambda i, t: (i, 0, 0)),
        pl.BlockSpec((ADA_CHUNKS, d), lambda i, t: (0, 0)),
        pl.BlockSpec((1, d), lambda i, t: (0, 0)),
    ]
    in_specs += [pl.BlockSpec((1, tm, a.shape[2]), lambda i, t: (i, t, 0)) for a in acts]
    in_specs += [pl.BlockSpec((k_act, d), lambda i, t, j=j: (j, 0), **resident) for j in range(n_in)]
    in_specs += [
        pl.BlockSpec((d, hid), lambda i, t: (0, 0), **resident),
        pl.BlockSpec((d, hid), lambda i, t: (0, 1), **resident),
        pl.BlockSpec((hid, d), lambda i, t: (0, 0), **resident),
        pl.BlockSpec((1, d), lambda i, t: (0, 0)),
    ]
    return pl.pallas_call(
        kern,
        grid=(b, n_rows // tm),
        in_specs=in_specs,
        out_specs=pl.BlockSpec((1, tm, d), lambda i, t: (i, t, 0)),
        out_shape=jax.ShapeDtypeStruct((b, n_rows, d), F32),
        compiler_params=_params(("arbitrary", "arbitrary")),
        name="mix_ffn_final" if final else "mix_ffn",
    )(xa, mods_l, mods_c, g, *acts, *([w_mix] * n_in), w_gate_up, w_gate_up, w_down, final_g)


def _mla_q_kernel(p_ref, g_ref, w_ref, cos_ref, sin_ref, o_ref, *, q_rank, scale):
    n_nope = MLA_HEADS * MLA_NOPE
    ppb = LANES // MLA_ROPE
    npb = LANES // MLA_NOPE
    gs = g_ref[...] * scale
    rc = _pick(p_ref.shape[1], (256, 128))
    for r in range(0, p_ref.shape[1], rc):
        cq = p_ref[0, r:r + rc, :q_rank].astype(F32)
        qn = _rms(cq, gs).astype(BF16)
        qq = jnp.dot(qn, w_ref[...], preferred_element_type=F32)
        cos, sin = cos_ref[r:r + rc, :], sin_ref[r:r + rc, :]
        pes_t = [_rope_block(qq[:, n_nope + c * LANES:n_nope + (c + 1) * LANES], cos, sin, MLA_ROPE).T
                 for c in range(MLA_HEADS // ppb)]
        for c in range(MLA_HEADS // npb):
            nope_t = qq[:, c * LANES:(c + 1) * LANES].T
            for k in range(npb):
                h = c * npb + k
                pe_t = pes_t[h // ppb][(h % ppb) * MLA_ROPE:(h % ppb + 1) * MLA_ROPE, :]
                qh = jnp.concatenate([nope_t[k * MLA_NOPE:(k + 1) * MLA_NOPE, :], pe_t], axis=0)
                o_ref[0, h, :, r:r + rc] = qh.astype(o_ref.dtype)


def _mla_q(p1, g, w_uq, cos_t, sin_t, n_lat, q_rank):
    b = p1.shape[0]
    cols = p1.shape[2]
    tm = _pick(n_lat, (512, 256, 128))
    kern = functools.partial(_mla_q_kernel, q_rank=q_rank, scale=MLA_QK ** -0.5 * LOG2E)
    return pl.pallas_call(
        kern,
        grid=(b, n_lat // tm),
        in_specs=[
            pl.BlockSpec((1, tm, cols), lambda i, t: (i, t, 0)),
            pl.BlockSpec((1, q_rank), lambda i, t: (0, 0)),
            pl.BlockSpec(w_uq.shape, lambda i, t: (0, 0)),
            pl.BlockSpec((tm, LANES), lambda i, t: (t, 0)),
            pl.BlockSpec((tm, LANES), lambda i, t: (t, 0)),
        ],
        out_specs=pl.BlockSpec((1, MLA_HEADS, MLA_QK, tm), lambda i, t: (i, 0, 0, t)),
        out_shape=jax.ShapeDtypeStruct((b, MLA_HEADS, MLA_QK, n_lat), BF16),
        compiler_params=_params(("arbitrary", "arbitrary")),
        name="mla_q",
    )(p1, g, w_uq, cos_t, sin_t)


def _mla_kv_kernel(p_ref, pe_ref, g_ref, w_ref, cos_ref, sin_ref, ko_ref, vo_ref, ksq_ref, *, q_rank):
    ckv = p_ref[0, :, q_rank:].astype(F32)
    kvn = _rms(ckv, g_ref[...]).astype(BF16)
    kv = jnp.dot(kvn, w_ref[...], preferred_element_type=F32)
    pe_block = _rope_block(pe_ref[0], cos_ref[...], sin_ref[...], MLA_ROPE)
    pe = pe_block[:, :MLA_ROPE]
    n_nope = MLA_HEADS * MLA_NOPE
    for h in range(MLA_HEADS):
        kh = jnp.concatenate([kv[:, h * MLA_NOPE:(h + 1) * MLA_NOPE], pe], axis=1)
        ko_ref[0, h] = kh.astype(ko_ref.dtype)
    pe_sq = jnp.sum(pe_block * pe_block, axis=1, keepdims=True)
    smat = _segment_mean_matrix(MLA_NOPE)
    for c in range(MLA_HEADS * MLA_NOPE // LANES):
        nope_sq = _segment_mean_sq(kv[:, c * LANES:(c + 1) * LANES], smat) * float(MLA_NOPE)
        _store_key_norms(ksq_ref, c, nope_sq + pe_sq)
    hpb = LANES // MLA_V
    for c in range(MLA_HEADS // hpb):
        v_t = kv[:, n_nope + c * LANES:n_nope + (c + 1) * LANES].T.astype(vo_ref.dtype)
        for k in range(hpb):
            vo_ref[0, c * hpb + k] = v_t[k * MLA_V:(k + 1) * MLA_V, :]


def _mla_kv(p1, s1, g, w_ukv, cos_t, sin_t, q_rank):
    b, n, cols = p1.shape
    kv_rank = cols - q_rank
    tm = _pick(n, (768, 384, 256, 128))
    kern = functools.partial(_mla_kv_kernel, q_rank=q_rank)
    return pl.pallas_call(
        kern,
        grid=(b, n // tm),
        in_specs=[
            pl.BlockSpec((1, tm, cols), lambda i, t: (i, t, 0)),
            pl.BlockSpec((1, tm, LANES), lambda i, t: (i, t, 0)),
            pl.BlockSpec((1, kv_rank), lambda i, t: (0, 0)),
            pl.BlockSpec(w_ukv.shape, lambda i, t: (0, 0)),
            pl.BlockSpec((tm, LANES), lambda i, t: (t, 0)),
            pl.BlockSpec((tm, LANES), lambda i, t: (t, 0)),
        ],
        out_specs=[
            pl.BlockSpec((1, MLA_HEADS, tm, MLA_QK), lambda i, t: (i, 0, t, 0)),
            pl.BlockSpec((1, MLA_HEADS, MLA_V, tm), lambda i, t: (i, 0, 0, t)),
            pl.BlockSpec((1, 1, 1, MLA_HEADS * LANES), lambda i, t: (i, t, 0, 0)),
        ],
        out_shape=[
            jax.ShapeDtypeStruct((b, MLA_HEADS, n, MLA_QK), BF16),
            jax.ShapeDtypeStruct((b, MLA_HEADS, MLA_V, n), BF16),
            jax.ShapeDtypeStruct((b, n // tm, 1, MLA_HEADS * LANES), F32),
        ],
        compiler_params=_params(("arbitrary", "arbitrary")),
        name="mla_kv",
    )(p1, s1, g, w_ukv, cos_t, sin_t)


def _pad_lanes(v, width=LANES):
    v = v.reshape(1, -1).astype(F32)
    return jnp.pad(v, ((0, 0), (0, width - v.shape[1])))


def kernel(x, c, ctx, c_ctx, ada_w, ada_b, norm1_g, norm2_g, ffn_w_up, ffn_w_down, ab_w_in, ab_w_out, ssd_conv_w, ssd_conv_b, ssd_a_log, ssd_dt_bias, ssd_d, ssd_norm_g, att_q_g, att_k_g, mla_w_in, mla_q_norm_g, mla_w_uq, mla_kv_norm_g, mla_w_ukv, mla_w_o, final_norm_g):
    b, t_lat, d = x.shape
    n_ctx = ctx.shape[1]
    rows = t_lat // GRID_W
    row2 = lambda v: v.reshape(1, -1).astype(F32)

    bp = -(-(b + 1) // 8) * 8
    cvec = jnp.concatenate([c, c_ctx[None, :], jnp.zeros((bp - b - 1, d), F32)], axis=0)
    mods = _ada_mods(cvec, ada_w, ada_b)
    mods_l = [mods[i, :b].reshape(b, ADA_CHUNKS, d) for i in range(2)]
    mods_c = [mods[i, b].reshape(ADA_CHUNKS, d) for i in range(2)]

    w_in = ab_w_in[0]
    o_xbc, o_dt = SSD_INNER, SSD_INNER + SSD_CONV_DIM
    o_att = o_dt + 2 * SSD_HEADS
    w_main = jnp.concatenate([w_in[:, o_xbc:o_dt], w_in[:, :SSD_INNER], w_in[:, o_att:]], axis=1).astype(BF16)
    pad_heads = lambda w: jnp.pad(w, ((0, 0), (0, LANES - SSD_HEADS)))
    w_dt = jnp.concatenate([pad_heads(w_in[:, o_dt:o_dt + SSD_HEADS]), pad_heads(w_in[:, o_dt + SSD_HEADS:o_att])],
                           axis=1).astype(BF16)
    p0, s0, xa = _in_proj([x, ctx], mods_l[0], mods_c[0], row2(norm1_g[0]), w_main, w_dt, t_lat, n_ctx)

    yf, act = _ssd_direction(p0, s0, _pad_lanes(ssd_dt_bias[0, 0]), _pad_lanes(ssd_a_log[0, 0]), t_lat,
                             reverse=False, conv=(ssd_conv_w[0].astype(F32), row2(ssd_conv_b[0])))
    d_skip = jnp.repeat(ssd_d[0].astype(F32), SSD_HEADDIM).reshape(1, SSD_INNER)
    y_ssd = _ssd_direction(p0, s0, _pad_lanes(ssd_dt_bias[0, 1]), _pad_lanes(ssd_a_log[0, 1]), t_lat,
                           reverse=True, extra=(act, yf, d_skip, row2(ssd_norm_g[0])))

    q_col = SSD_CONV_DIM + SSD_INNER
    k_col = q_col + ATT_Q
    cos_a, sin_a = _rope_tables(rows, ATT_HEADDIM, n_ctx)
    hpb = LANES // ATT_HEADDIM
    qh, kh, vh, ksq0 = _qk_prep(p0, cos_a, sin_a, jnp.tile(row2(att_q_g[0]), (1, hpb)),
                                jnp.tile(row2(att_k_g[0]), (1, hpb)), q_col, k_col, k_col + ATT_KV)
    o_att_l0 = _attention(qh, kh, vh, ksq0, t_lat, ctx_self=True)

    xa = _mix_ffn(xa, mods_l[0], mods_c[0], row2(norm2_g[0]), [y_ssd, o_att_l0], ab_w_out[0].astype(BF16),
                  ffn_w_up[0].astype(BF16), ffn_w_down[0].astype(BF16), row2(final_norm_g), t_lat + n_ctx, t_lat,
                  final=False)

    q_rank = mla_q_norm_g.shape[1]
    kv_rank = mla_kv_norm_g.shape[1]
    w_in1 = mla_w_in[0]
    w_pe = jnp.pad(w_in1[:, q_rank + kv_rank:], ((0, 0), (0, LANES - MLA_ROPE))).astype(BF16)
    p1, s1 = _in_proj([xa], mods_l[1], mods_c[1], row2(norm1_g[1]), w_in1[:, :q_rank + kv_rank].astype(BF16), w_pe,
                      t_lat, n_ctx)
    cos_m, sin_m = _rope_tables(rows, MLA_ROPE, n_ctx)
    w_uq = mla_w_uq[0].reshape(q_rank, MLA_HEADS, MLA_QK)
    w_uq = jnp.concatenate([w_uq[:, :, :MLA_NOPE].reshape(q_rank, -1), w_uq[:, :, MLA_NOPE:].reshape(q_rank, -1)],
                           axis=1).astype(BF16)
    w_ukv = mla_w_ukv[0].reshape(kv_rank, MLA_HEADS, MLA_NOPE + MLA_V)
    w_ukv = jnp.concatenate([w_ukv[:, :, :MLA_NOPE].reshape(kv_rank, -1), w_ukv[:, :, MLA_NOPE:].reshape(kv_rank, -1)],
                            axis=1).astype(BF16)
    qm = _mla_q(p1, row2(mla_q_norm_g[0]), w_uq, cos_m, sin_m, t_lat, q_rank)
    km, vm, ksq1 = _mla_kv(p1, s1, row2(mla_kv_norm_g[0]), w_ukv, cos_m, sin_m, q_rank)
    o_mla = _attention(qm, km, vm, ksq1, t_lat, ctx_self=False)

    return _mix_ffn(xa, mods_l[1], mods_c[1], row2(norm2_g[1]), [o_mla], mla_w_o[0].astype(BF16),
                    ffn_w_up[1].astype(BF16), ffn_w_down[1].astype(BF16), row2(final_norm_g), t_lat, t_lat,
                    final=True)
```
